```python
import math
import jax, jax.numpy as jnp
from jax import lax
import numpy as np

D_MODEL = 1024
BATCH = 8
SEQ = 2048
DEPTH = 1
DEC_BATCH = 128
DEC_SEQ = 1
PAST_LEN = 16384
PAGE_SIZE = 128

A_HEADS = 4
A_DK = 128
A_DV = 128
A_KWIDTH = A_HEADS * A_DK
A_WIDTH = A_HEADS * A_DV
CONV_WIDTH = 4
CONV_CH = 2 * A_KWIDTH + A_WIDTH
DELTA_CHUNK = 64
B_GROUPS = 4
B_WIDTH = D_MODEL // 2
B_CHUNK = 128
MEM_LEN = 256
C_HEADS = 4
C_DH = 128
C_WIDTH = C_HEADS * C_DH
N_BRANCH = 3
EPS = 1e-6
SPLIT_SIZES = (A_KWIDTH, A_KWIDTH, A_WIDTH, A_HEADS, A_HEADS, A_WIDTH,
               B_WIDTH, B_WIDTH, B_WIDTH, C_WIDTH, C_WIDTH, N_BRANCH * D_MODEL)
IN_COLS = sum(SPLIT_SIZES)

kernel_name = "hybrid_delta_chunkmlp_memxattn_step"


def _split_points():
    pts, acc = [], 0
    for s in SPLIT_SIZES[:-1]:
        acc += s
        pts.append(acc)
    return pts


def rmsnorm(x, g):
    xf = x.astype(jnp.float32)
    y = xf * lax.rsqrt(jnp.mean(xf * xf, axis=-1, keepdims=True) + EPS)
    return (y * g.astype(jnp.float32)).astype(x.dtype)


def layernorm(x, g, b):
    xf = x.astype(jnp.float32)
    mu = jnp.mean(xf, axis=-1, keepdims=True)
    xc = xf - mu
    y = xc * lax.rsqrt(jnp.mean(xc * xc, axis=-1, keepdims=True) + EPS)
    return (y * g.astype(jnp.float32) + b.astype(jnp.float32)).astype(x.dtype)


def l2norm(x):
    return x * lax.rsqrt(jnp.sum(x * x, axis=-1, keepdims=True) + EPS)


def causal_conv(buf, x, w):
    T = x.shape[1]
    xp = jnp.concatenate([buf.astype(x.dtype), x], axis=1)
    y = xp[:, 0:T] * w[0]
    for j in range(1, CONV_WIDTH):
        y = y + xp[:, j:j + T] * w[j]
    return jax.nn.silu(y), xp[:, -(CONV_WIDTH - 1):]


def gated_delta_chunked(q, k, v, beta, g, S0):
    B, T, H, _ = q.shape
    C = DELTA_CHUNK
    Tp = -(-T // C) * C
    N = Tp // C
    pad = Tp - T

    def padt(a):
        return jnp.pad(a, [(0, 0), (0, pad)] + [(0, 0)] * (a.ndim - 2))

    def chunk4(a):
        return padt(a).reshape(B, N, C, H, a.shape[-1]).transpose(1, 0, 3, 2, 4)

    def chunk3(a):
        return padt(a).reshape(B, N, C, H).transpose(1, 0, 3, 2)

    q, k, v = chunk4(q), chunk4(k), chunk4(v)
    beta, g = chunk3(beta), chunk3(g)
    gc = jnp.cumsum(g, axis=-1)
    idx = jnp.arange(C)
    causal = idx[:, None] >= idx[None, :]
    strict = idx[:, None] > idx[None, :]
    decay = jnp.exp(jnp.where(causal, gc[..., :, None] - gc[..., None, :], -jnp.inf))
    k_beta = k * beta[..., None]
    L = jnp.where(strict, jnp.einsum('nbhid,nbhjd->nbhij', k_beta, k) * decay, 0.0)
    eye = jnp.eye(C, dtype=q.dtype)
    Tinv = lax.linalg.triangular_solve(eye + L, jnp.broadcast_to(eye, L.shape),
                                       left_side=True, lower=True)
    u = jnp.einsum('nbhij,nbhjv->nbhiv', Tinv, v * beta[..., None])
    w = jnp.einsum('nbhij,nbhjk->nbhik', Tinv, k_beta * jnp.exp(gc)[..., None])
    attn = jnp.where(causal, jnp.einsum('nbhid,nbhjd->nbhij', q, k) * decay, 0.0)
    q_dec = q * jnp.exp(gc)[..., None]
    k_dec = k * jnp.exp(gc[..., -1:] - gc)[..., None]
    g_last = jnp.exp(gc[..., -1])

    def step(S, xs):
        u_i, w_i, attn_i, qd_i, kd_i, gl_i = xs
        v_new = u_i - jnp.einsum('bhck,bhkv->bhcv', w_i, S)
        o = jnp.einsum('bhck,bhkv->bhcv', qd_i, S) + jnp.einsum('bhij,bhjv->bhiv', attn_i, v_new)
        S = S * gl_i[..., None, None] + jnp.einsum('bhck,bhcv->bhkv', kd_i, v_new)
        return S, o

    S, o = lax.scan(step, S0, (u, w, attn, q_dec, k_dec, g_last))
    o = o.transpose(1, 0, 3, 2, 4).reshape(B, Tp, H, v.shape[-1])[:, :T]
    return o, S


def delta_branch(q, k, v, beta_logit, alpha, gate, S0, conv_buf, conv_w, a_log, dt_bias, a_norm_g):
    B, T, _ = q.shape
    f32 = jnp.float32
    qkv, new_buf = causal_conv(conv_buf, jnp.concatenate([q, k, v], axis=-1), conv_w)
    q, k, v = jnp.split(qkv, [A_KWIDTH, 2 * A_KWIDTH], axis=-1)
    q = l2norm(q.reshape(B, T, A_HEADS, A_DK).astype(f32)) * (A_DK ** -0.5)
    k = l2norm(k.reshape(B, T, A_HEADS, A_DK).astype(f32))
    v = v.reshape(B, T, A_HEADS, A_DV).astype(f32)
    beta = jax.nn.sigmoid(beta_logit.astype(f32))
    g = -jnp.exp(a_log.astype(f32)) * jax.nn.softplus(alpha.astype(f32) + dt_bias.astype(f32))
    o, S = gated_delta_chunked(q, k, v, beta, g, S0.astype(f32))
    o = rmsnorm(o, a_norm_g) * jax.nn.silu(gate.reshape(B, T, A_HEADS, A_DV).astype(f32))
    return o.reshape(B, T, A_WIDTH).astype(gate.dtype), S.astype(S0.dtype), new_buf


def chunk_mlp_branch(u, v, gate, ln_v_g, ln_v_b, w_spatial, b_spatial):
    B, T, _ = u.shape
    vn = layernorm(v, ln_v_g, ln_v_b)
    Tp = -(-T // B_CHUNK) * B_CHUNK
    N = Tp // B_CHUNK
    vp = jnp.pad(vn, ((0, 0), (0, Tp - T), (0, 0))).reshape(B, N, B_CHUNK, B_GROUPS, B_WIDTH // B_GROUPS)
    idx = jnp.arange(B_CHUNK)
    ws = jnp.where(idx[:, None] >= idx[None, :], w_spatial, 0.0)
    s = jnp.einsum('gts,bnsgc->bntgc', ws, vp) + b_spatial.T[None, None, :, :, None]
    s = s.reshape(B, Tp, B_WIDTH)[:, :T]
    return u * s * jax.nn.silu(gate), vn


def memory_kv(mem, mem_norm_g, w_mem_kv):
    B, M, _ = mem.shape
    kv = rmsnorm(mem, mem_norm_g) @ w_mem_kv
    k, v = jnp.split(kv, [C_WIDTH], axis=-1)
    return k.reshape(B, M, C_HEADS, C_DH), v.reshape(B, M, C_HEADS, C_DH)


def memory_branch(q, gate, mem_k, mem_v):
    B, T, _ = q.shape
    f32 = jnp.float32
    qh = q.reshape(B, T, C_HEADS, C_DH).astype(f32)
    s = jnp.einsum('bthd,bmhd->bhtm', qh, mem_k.astype(f32)) * (C_DH ** -0.5)
    p = jax.nn.softmax(s, axis=-1)
    o = jnp.einsum('bhtm,bmhd->bthd', p, mem_v.astype(f32)).reshape(B, T, C_WIDTH)
    return (o * jax.nn.silu(gate.astype(f32))).astype(q.dtype)


def mixer_layer(x, mem_k, mem_v, S0, conv_buf, norm_g, w_in, conv_w, a_log, dt_bias, a_norm_g,
                ln_v_g, ln_v_b, w_spatial, b_spatial, w_br_a, w_br_b, w_br_c, b_gate, w_out):
    B, T, _ = x.shape
    h = rmsnorm(x, norm_g)
    proj = h @ w_in
    (aq, ak, av, abeta, aalpha, agate, bu, bv, bgate, cq, cgate, mg) = jnp.split(proj, _split_points(), axis=-1)
    ya, S, buf = delta_branch(aq, ak, av, abeta, aalpha, agate, S0, conv_buf, conv_w, a_log, dt_bias, a_norm_g)
    yb, vn = chunk_mlp_branch(bu, bv, bgate, ln_v_g, ln_v_b, w_spatial, b_spatial)
    yc = memory_branch(cq, cgate, mem_k, mem_v)
    gates = jax.nn.sigmoid((mg.reshape(B, T, N_BRANCH, D_MODEL) + b_gate).astype(jnp.float32)).astype(x.dtype)
    merged = gates[:, :, 0] * (ya @ w_br_a) + gates[:, :, 1] * (yb @ w_br_b) + gates[:, :, 2] * (yc @ w_br_c)
    return x + merged @ w_out, S, buf, vn


def setup_inputs(seed: int = 0) -> dict:
    key = jax.random.key(seed)
    ks = list(jax.random.split(key, 32))
    f32 = jnp.float32

    def nrm(i, shape, s=1.0):
        return jax.random.normal(ks[i], shape, f32) * s

    dt = jnp.exp(jax.random.uniform(ks[10], (DEPTH, A_HEADS), f32)
                 * (math.log(0.1) - math.log(0.001)) + math.log(0.001))
    return {
        "x_prompt": nrm(0, (BATCH, SEQ, D_MODEL)),
        "x_sample": nrm(1, (DEC_BATCH, DEC_SEQ, D_MODEL)),
        "cache_mem_k": nrm(2, (DEPTH, DEC_BATCH, MEM_LEN, C_HEADS, C_DH)),
        "cache_mem_v": nrm(3, (DEPTH, DEC_BATCH, MEM_LEN, C_HEADS, C_DH)),
        "state_delta": nrm(4, (DEPTH, DEC_BATCH, A_HEADS, A_DK, A_DV), 0.1),
        "state_conv": nrm(5, (DEPTH, DEC_BATCH, CONV_WIDTH - 1, CONV_CH)),
        "mem_prompt": nrm(6, (BATCH, MEM_LEN, D_MODEL)),
        "norm_g": 1.0 + nrm(7, (DEPTH, D_MODEL), 0.02),
        "w_in": nrm(8, (DEPTH, D_MODEL, IN_COLS), D_MODEL ** -0.5),
        "conv_w": nrm(9, (DEPTH, CONV_WIDTH, CONV_CH), 0.5),
        "a_log": jnp.log(jax.random.uniform(ks[11], (DEPTH, A_HEADS), f32, 1.0, 16.0)),
        "dt_bias": dt + jnp.log(-jnp.expm1(-dt)),
        "a_norm_g": 1.0 + nrm(12, (DEPTH, A_DV), 0.02),
        "ln_v_g": 1.0 + nrm(13, (DEPTH, B_WIDTH), 0.02),
        "ln_v_b": nrm(14, (DEPTH, B_WIDTH), 0.02),
        "w_spatial": nrm(15, (DEPTH, B_GROUPS, B_CHUNK, B_CHUNK), 0.5 * B_CHUNK ** -0.5),
        "b_spatial": 1.0 + nrm(16, (DEPTH, B_GROUPS, B_CHUNK), 0.02),
        "mem_norm_g": 1.0 + nrm(17, (DEPTH, D_MODEL), 0.02),
        "w_mem_kv": nrm(18, (DEPTH, D_MODEL, 2 * C_WIDTH), D_MODEL ** -0.5),
        "w_br_a": nrm(19, (DEPTH, A_WIDTH, D_MODEL), A_WIDTH ** -0.5),
        "w_br_b": nrm(20, (DEPTH, B_WIDTH, D_MODEL), B_WIDTH ** -0.5),
        "w_br_c": nrm(21, (DEPTH, C_WIDTH, D_MODEL), C_WIDTH ** -0.5),
        "b_gate": nrm(22, (DEPTH, N_BRANCH, D_MODEL), 0.02),
        "w_out": nrm(23, (DEPTH, D_MODEL, D_MODEL), D_MODEL ** -0.5),
        "final_norm_g": 1.0 + nrm(24, (D_MODEL,), 0.02),
    }


def reference(x_prompt, x_sample, cache_mem_k, cache_mem_v, state_delta, state_conv, mem_prompt,
              norm_g, w_in, conv_w, a_log, dt_bias, a_norm_g, ln_v_g, ln_v_b, w_spatial, b_spatial,
              mem_norm_g, w_mem_kv, w_br_a, w_br_b, w_br_c, b_gate, w_out, final_norm_g):
    hp, hs = x_prompt, x_sample
    bp = x_prompt.shape[0]
    sd_p, sc_p, mk_p, mv_p, sd_s, sc_s, cv_s = [], [], [], [], [], [], []
    for l in range(DEPTH):
        lw = (norm_g[l], w_in[l], conv_w[l], a_log[l], dt_bias[l], a_norm_g[l], ln_v_g[l], ln_v_b[l],
              w_spatial[l], b_spatial[l], w_br_a[l], w_br_b[l], w_br_c[l], b_gate[l], w_out[l])
        mk, mv = memory_kv(mem_prompt, mem_norm_g[l], w_mem_kv[l])
        s0 = jnp.zeros((bp, A_HEADS, A_DK, A_DV), x_prompt.dtype)
        c0 = jnp.zeros((bp, CONV_WIDTH - 1, CONV_CH), x_prompt.dtype)
        hp, s_p, c_p, _ = mixer_layer(hp, mk, mv, s0, c0, *lw)
        hs, s_s, c_s, v_s = mixer_layer(hs, cache_mem_k[l], cache_mem_v[l], state_delta[l], state_conv[l], *lw)
        sd_p.append(s_p)
        sc_p.append(c_p)
        mk_p.append(mk)
        mv_p.append(mv)
        sd_s.append(s_s)
        sc_s.append(c_s)
        cv_s.append(v_s)
    y_prompt = rmsnorm(hp, final_norm_g)
    y_sample = rmsnorm(hs, final_norm_g)
    return (y_prompt, y_sample, jnp.stack(sd_p), jnp.stack(sc_p), jnp.stack(mk_p), jnp.stack(mv_p),
            jnp.stack(sd_s), jnp.stack(sc_s), jnp.stack(cv_s))
```

```python
import functools
import math

import jax
import jax.numpy as jnp
from jax import lax
from jax.experimental import pallas as pl
from jax.experimental.pallas import tpu as pltpu

F32 = jnp.float32
BF16 = jnp.bfloat16

D_MODEL = 1024
HEADS = 4
HEAD_DIM = 128
WIDTH = HEADS * HEAD_DIM
CONV_WIDTH = 4
CONV_CH = 3 * WIDTH
MLP_GROUPS = 4
MLP_CHUNK = 128
MEM_LEN = 256
N_BRANCH = 3
EPS = 1e-6

LANES = 128
SUBLANES = 8

COL_QKV = 0
COL_AGATE = COL_QKV + CONV_CH
COL_B = COL_AGATE + WIDTH
COL_C = COL_B + 3 * WIDTH
COL_MG = COL_C + 2 * WIDTH
COL_BG = COL_MG + N_BRANCH * D_MODEL
PACKED_COLS = 8192

PROMPT_BLOCK = 256
DELTA_CHUNK = 64
SAMPLE_BLOCK = 8
VMEM_LIMIT = 56 * 1024 * 1024


def _dot(a, b):
    return jnp.dot(a.astype(BF16), b.astype(BF16), preferred_element_type=F32)


def _dot_nt(a, b):
    return lax.dot_general(a.astype(BF16), b.astype(BF16), (((1,), (1,)), ((), ())),
                           preferred_element_type=F32)


def _dot_exact(a, b):
    return jnp.dot(a, b, preferred_element_type=F32, precision=lax.Precision.HIGHEST)


def _rms(x):
    return x * lax.rsqrt(jnp.mean(x * x, axis=-1, keepdims=True) + EPS)


def _softplus(x):
    return jnp.maximum(x, 0.0) + jnp.log1p(jnp.exp(-jnp.abs(x)))


def _iota2(shape, dim):
    return lax.broadcasted_iota(jnp.int32, shape, dim)


def _inv_unit_lower(low):
    n = low.shape[0]
    neg = -low
    eye = (_iota2((n, n), 0) == _iota2((n, n), 1)).astype(F32)
    inv = eye + neg
    power = neg
    k = 1
    while 2 * k < n:
        power = _dot(power, power)
        inv = inv + _dot(inv, power)
        k *= 2
    return inv


def _memkv_kernel(mem_ref, g_ref, w_ref, k_ref, v_ref, kb_ref, vb_ref):
    xn = _rms(mem_ref[...]) * g_ref[...]
    kv = _dot(xn, w_ref[...])
    k = kv[:, :WIDTH]
    v = kv[:, WIDTH:]
    k_ref[...] = k
    v_ref[...] = v
    kb_ref[...] = k.astype(BF16)
    vb_ref[...] = v.astype(BF16)


def _memory_kv(mem2d, mem_norm_g, w_mem_kv_bf):
    rows = mem2d.shape[0]
    blk = 256
    full = lambda shape: pl.BlockSpec(shape, lambda i: (0,) * len(shape))
    row_spec = lambda width: pl.BlockSpec((blk, width), lambda i: (i, 0))
    return pl.pallas_call(
        _memkv_kernel,
        grid=(rows // blk,),
        in_specs=[row_spec(D_MODEL), full((1, D_MODEL)), full((D_MODEL, 2 * WIDTH))],
        out_specs=[row_spec(WIDTH)] * 4,
        out_shape=[jax.ShapeDtypeStruct((rows, WIDTH), F32)] * 2
        + [jax.ShapeDtypeStruct((rows, WIDTH), BF16)] * 2,
        compiler_params=pltpu.CompilerParams(dimension_semantics=("arbitrary",),
                                             vmem_limit_bytes=VMEM_LIMIT),
        name="memory_kv",
    )(mem2d, mem_norm_g, w_mem_kv_bf)


def _beta_and_log_decay(bg, alog, dtb):
    beta = jax.nn.sigmoid(bg)
    g = -jnp.exp(alog) * _softplus(bg + dtb)
    return beta, g


def _spatial_weights(ws_ref):
    tril = _iota2((MLP_CHUNK, MLP_CHUNK), 0) >= _iota2((MLP_CHUNK, MLP_CHUNK), 1)
    return [jnp.where(tril, ws_ref[g], 0.0).astype(BF16) for g in range(MLP_GROUPS)]


def _layernorm(v, g, b):
    mu = jnp.mean(v, axis=-1, keepdims=True)
    vc = v - mu
    return vc * lax.rsqrt(jnp.mean(vc * vc, axis=-1, keepdims=True) + EPS) * g + b


def _merge_and_project(x, hb, ya, yb, yc, w_ref, bgate_ref, wbra_ref, wbrb_ref, wbrc_ref, wout_ref,
                       fng_ref):
    merged = None
    for i, (yi, wbr) in enumerate(((ya, wbra_ref), (yb, wbrb_ref), (yc, wbrc_ref))):
        lo = COL_MG + i * D_MODEL
        gate = jax.nn.sigmoid(_dot(hb, w_ref[:, lo:lo + D_MODEL]) + bgate_ref[i:i + 1, :])
        term = gate * _dot(yi, wbr[...])
        merged = term if merged is None else merged + term
    out = x + _dot(merged, wout_ref[...])
    return _rms(out) * fng_ref[...]


def _prompt_kernel(x_ref, mk_ref, mv_ref, w_ref, wbgt_ref, normg_ref, convw_ref, alog_ref, dtb_ref,
                   anormg_ref, lng_ref, lnb_ref, ws_ref, bs_ref, wbra_ref, wbrb_ref, wbrc_ref,
                   bgate_ref, wout_ref, fng_ref,
                   y_ref, sd_ref, sc_ref,
                   convbuf, state):
    tb = PROMPT_BLOCK
    ck = DELTA_CHUNK
    t = pl.program_id(1)

    @pl.when(t == 0)
    def _():
        convbuf[0:SUBLANES, :] = jnp.zeros((SUBLANES, CONV_CH), F32)
        state[...] = jnp.zeros_like(state)

    x = x_ref[0]
    hb = (_rms(x) * normg_ref[...]).astype(BF16)

    pre = _dot(hb, w_ref[:, COL_QKV:COL_QKV + CONV_CH])
    convbuf[SUBLANES:SUBLANES + tb, :] = pre
    conv = pre * convw_ref[CONV_WIDTH - 1:CONV_WIDTH, :]
    for j in range(CONV_WIDTH - 1):
        shift = CONV_WIDTH - 1 - j
        conv = conv + convbuf[SUBLANES - shift:SUBLANES - shift + tb, :] * convw_ref[j:j + 1, :]
    tail = convbuf[tb:tb + SUBLANES, :]
    convbuf[0:SUBLANES, :] = tail
    sc_ref[0] = tail[SUBLANES - (CONV_WIDTH - 1):, :]
    qkv = jax.nn.silu(conv)
    agate = _dot(hb, w_ref[:, COL_AGATE:COL_AGATE + WIDTH])

    bg_col = _dot(hb, w_ref[:, COL_BG:COL_BG + LANES])
    bg_row = _dot_nt(wbgt_ref[...], hb)
    beta_col, g_col = _beta_and_log_decay(bg_col, alog_ref[...], dtb_ref[...])
    alog_c = alog_ref[...]
    g_row = -jnp.exp(_col_from_lanes(alog_c)) * _softplus(bg_row + _col_from_lanes(dtb_ref[...]))

    ri = _iota2((ck, ck), 0)
    ci = _iota2((ck, ck), 1)
    causal = ri >= ci
    strict = ri > ci
    tril_f = causal.astype(F32)
    triu_f = (ri <= ci).astype(F32)

    gc_cols = [_dot_exact(tril_f, g_col[c * ck:(c + 1) * ck, :]) for c in range(tb // ck)]
    gc_rows = [_dot_exact(g_row[:, c * ck:(c + 1) * ck], triu_f) for c in range(tb // ck)]

    o_heads = []
    for hd in range(HEADS):
        lo = hd * HEAD_DIM
        q = qkv[:, lo:lo + HEAD_DIM]
        k = qkv[:, WIDTH + lo:WIDTH + lo + HEAD_DIM]
        v = qkv[:, 2 * WIDTH + lo:2 * WIDTH + lo + HEAD_DIM]
        q = q * lax.rsqrt(jnp.sum(q * q, axis=-1, keepdims=True) + EPS) * (HEAD_DIM ** -0.5)
        k = k * lax.rsqrt(jnp.sum(k * k, axis=-1, keepdims=True) + EPS)
        s_h = state[hd]
        o_chunks = []
        for c in range(tb // ck):
            r0 = c * ck
            qc = q[r0:r0 + ck]
            kc = k[r0:r0 + ck]
            vc = v[r0:r0 + ck]
            kct = kc.T
            bcol = beta_col[r0:r0 + ck, hd:hd + 1]
            gc_col = gc_cols[c][:, HEADS + hd:HEADS + hd + 1]
            gc_row = gc_rows[c][HEADS + hd:HEADS + hd + 1, :]
            gc_last = gc_row[:, ck - 1:ck]
            decay = jnp.exp(jnp.where(causal, gc_col - gc_row, -1e30))
            kb = kc * bcol
            low = jnp.where(strict, _dot(kb, kct) * decay, 0.0)
            inv = _inv_unit_lower(low)
            egc = jnp.exp(gc_col)
            uw = _dot(inv, jnp.concatenate([vc * bcol, kb * egc], axis=1))
            u = uw[:, :HEAD_DIM]
            w = uw[:, HEAD_DIM:]
            attn = jnp.where(causal, _dot(qc, kct) * decay, 0.0)
            v_new = u - _dot(w, s_h)
            o_chunks.append(_dot(qc * egc, s_h) + _dot(attn, v_new))
            kdt = kct * jnp.exp(gc_last - gc_row)
            s_h = s_h * jnp.exp(gc_last) + _dot(kdt, v_new)
        state[hd] = s_h
        o = jnp.concatenate(o_chunks, axis=0)
        o = _rms(o) * anormg_ref[...] * jax.nn.silu(agate[:, lo:lo + HEAD_DIM])
        o_heads.append(o)
    ya = jnp.concatenate(o_heads, axis=1)
    sd_ref[0] = state[...]

    pb = _dot(hb, w_ref[:, COL_B:COL_B + 3 * WIDTH])
    bu = pb[:, :WIDTH]
    vn = _layernorm(pb[:, WIDTH:2 * WIDTH], lng_ref[...], lnb_ref[...])
    bgate = pb[:, 2 * WIDTH:]
    ws = _spatial_weights(ws_ref)
    gw = WIDTH // MLP_GROUPS
    s_rows = []
    for n in range(tb // MLP_CHUNK):
        r0 = n * MLP_CHUNK
        s_rows.append(jnp.concatenate(
            [_dot(ws[g], vn[r0:r0 + MLP_CHUNK, g * gw:(g + 1) * gw]) + bs_ref[g]
             for g in range(MLP_GROUPS)], axis=1))
    yb = bu * jnp.concatenate(s_rows, axis=0) * jax.nn.silu(bgate)

    pc = _dot(hb, w_ref[:, COL_C:COL_C + 2 * WIDTH])
    cgate = pc[:, WIDTH:]
    oc = []
    for hd in range(HEADS):
        lo = hd * HEAD_DIM
        s = _dot_nt(pc[:, lo:lo + HEAD_DIM], mk_ref[0, :, lo:lo + HEAD_DIM]) * (HEAD_DIM ** -0.5)
        s = s - jnp.max(s, axis=-1, keepdims=True)
        p = jnp.exp(s)
        p = p / jnp.sum(p, axis=-1, keepdims=True)
        oc.append(_dot(p, mv_ref[0, :, lo:lo + HEAD_DIM]))
    yc = jnp.concatenate(oc, axis=1) * jax.nn.silu(cgate)

    y_ref[0] = _merge_and_project(x, hb, ya, yb, yc, w_ref, bgate_ref, wbra_ref, wbrb_ref, wbrc_ref,
                                  wout_ref, fng_ref)


def _col_from_lanes(row):
    sel = _iota2((SUBLANES, LANES), 0) == _iota2((SUBLANES, LANES), 1)
    return jnp.sum(jnp.where(sel, row, 0.0), axis=-1, keepdims=True)


def _prompt_layer(x, mkb, mvb, wp, wbgt, params):
    bsz, seq, _ = x.shape
    tb = PROMPT_BLOCK
    nt = seq // tb
    full = lambda a: pl.BlockSpec(a.shape, lambda b, t: (0,) * a.ndim, pipeline_mode=pl.Buffered(1))
    in_specs = [
        pl.BlockSpec((1, tb, D_MODEL), lambda b, t: (b, t, 0)),
        pl.BlockSpec((1, MEM_LEN, WIDTH), lambda b, t: (b, 0, 0)),
        pl.BlockSpec((1, MEM_LEN, WIDTH), lambda b, t: (b, 0, 0)),
        full(wp), full(wbgt),
    ] + [full(p) for p in params]
    out_specs = [
        pl.BlockSpec((1, tb, D_MODEL), lambda b, t: (b, t, 0)),
        pl.BlockSpec((1, HEADS, HEAD_DIM, HEAD_DIM), lambda b, t: (b, 0, 0, 0)),
        pl.BlockSpec((1, CONV_WIDTH - 1, CONV_CH), lambda b, t: (b, 0, 0)),
    ]
    out_shape = [
        jax.ShapeDtypeStruct((bsz, seq, D_MODEL), F32),
        jax.ShapeDtypeStruct((bsz, HEADS, HEAD_DIM, HEAD_DIM), F32),
        jax.ShapeDtypeStruct((bsz, CONV_WIDTH - 1, CONV_CH), F32),
    ]
    return pl.pallas_call(
        _prompt_kernel,
        grid=(bsz, nt),
        in_specs=in_specs,
        out_specs=out_specs,
        out_shape=out_shape,
        scratch_shapes=[pltpu.VMEM((tb + SUBLANES, CONV_CH), F32),
                        pltpu.VMEM((HEADS, HEAD_DIM, HEAD_DIM), F32)],
        compiler_params=pltpu.CompilerParams(dimension_semantics=("arbitrary", "arbitrary"),
                                             vmem_limit_bytes=VMEM_LIMIT),
        name="prompt_layer",
    )(x, mkb, mvb, wp, wbgt, *params)


def _sample_kernel(x_ref, ck_ref, cv_ref, s0_ref, cb_ref, w_ref, normg_ref, convw_ref, alog_ref,
                   dtb_ref, anormg_ref, lng_ref, lnb_ref, ws_ref, bs_ref, wbra_ref, wbrb_ref, wbrc_ref,
                   bgate_ref, wout_ref, fng_ref,
                   y_ref, sd_ref, sc_ref, vn_ref,
                   hb_scr, proj, bg_scr, ya_scr, yc_scr):
    sb = SAMPLE_BLOCK
    i = pl.program_id(0)
    nsteps = pl.num_programs(0)

    @pl.when(i == 0)
    def _():
        hb = (_rms(x_ref[...]) * normg_ref[...]).astype(BF16)
        hb_scr[...] = hb
        proj[...] = _dot(hb, w_ref[:, :COL_MG])
        bg_scr[...] = _dot(hb, w_ref[:, COL_BG:COL_BG + LANES])
        bu = proj[:, COL_B:COL_B + WIDTH]
        vn = _layernorm(proj[:, COL_B + WIDTH:COL_B + 2 * WIDTH], lng_ref[...], lnb_ref[...])
        vn_ref[...] = vn
        bgate = proj[:, COL_B + 2 * WIDTH:COL_B + 3 * WIDTH]
        gw = WIDTH // MLP_GROUPS
        s = jnp.concatenate(
            [vn[:, g * gw:(g + 1) * gw] * ws_ref[g, 0:1, 0:1] + bs_ref[g, 0:1, :]
             for g in range(MLP_GROUPS)], axis=1)
        proj[:, COL_B:COL_B + WIDTH] = bu * s * jax.nn.silu(bgate)

    r0 = pl.multiple_of(i * sb, sb)
    rows = pl.ds(r0, sb)
    pre = proj[rows, COL_QKV:COL_QKV + CONV_CH]
    beta, g = _beta_and_log_decay(bg_scr[rows, :], alog_ref[...], dtb_ref[...])
    decay = jnp.exp(g)
    agate = proj[rows, COL_AGATE:COL_AGATE + WIDTH]
    cq = proj[rows, COL_C:COL_C + WIDTH]
    cgate = proj[rows, COL_C + WIDTH:COL_C + 2 * WIDTH]

    head_of_row = _iota2((LANES, WIDTH), 0)
    head_of_col = _iota2((LANES, WIDTH), 1) // HEAD_DIM
    head_sel = head_of_row == head_of_col
    expand = head_sel.astype(BF16)
    eye = _iota2((HEAD_DIM, HEAD_DIM), 0) == _iota2((HEAD_DIM, HEAD_DIM), 1)

    ya_rows = []
    yc_rows = []
    for s in range(sb):
        cb = cb_ref[s]
        pre_s = pre[s:s + 1, :]
        conv = pre_s * convw_ref[CONV_WIDTH - 1:CONV_WIDTH, :]
        for j in range(CONV_WIDTH - 1):
            conv = conv + cb[j:j + 1, :] * convw_ref[j:j + 1, :]
        sc_ref[s] = jnp.concatenate([cb[1:, :], pre_s], axis=0)
        qkv = jax.nn.silu(conv)
        o_heads = []
        for hd in range(HEADS):
            lo = hd * HEAD_DIM
            q = qkv[:, lo:lo + HEAD_DIM]
            k = qkv[:, WIDTH + lo:WIDTH + lo + HEAD_DIM]
            v = qkv[:, 2 * WIDTH + lo:2 * WIDTH + lo + HEAD_DIM]
            q = q * lax.rsqrt(jnp.sum(q * q, axis=-1, keepdims=True) + EPS) * (HEAD_DIM ** -0.5)
            k = k * lax.rsqrt(jnp.sum(k * k, axis=-1, keepdims=True) + EPS)
            a = decay[s:s + 1, HEADS + hd:HEADS + hd + 1]
            bt = beta[s:s + 1, hd:hd + 1]
            st = s0_ref[s, hd]
            kcol = jnp.sum(jnp.where(eye, k, 0.0), axis=-1, keepdims=True)
            qcol = jnp.sum(jnp.where(eye, q, 0.0), axis=-1, keepdims=True)
            sk = jnp.sum(st * kcol, axis=0, keepdims=True)
            sq = jnp.sum(st * qcol, axis=0, keepdims=True)
            v_new = bt * (v - a * sk)
            sd_ref[s, hd] = a * st + kcol * v_new
            qk = jnp.sum(q * k, axis=-1, keepdims=True)
            o = a * sq + qk * v_new
            o = _rms(o) * anormg_ref[...] * jax.nn.silu(agate[s:s + 1, lo:lo + HEAD_DIM])
            o_heads.append(o)
        ya_rows.append(jnp.concatenate(o_heads, axis=1))

        qbd = jnp.where(head_sel, cq[s:s + 1, :], 0.0)
        sc = _dot_nt(ck_ref[s], qbd) * (HEAD_DIM ** -0.5)
        sc = sc - jnp.max(sc, axis=0, keepdims=True)
        p = jnp.exp(sc)
        p = p / jnp.sum(p, axis=0, keepdims=True)
        pw = _dot(p, expand)
        oc = jnp.sum(pw * cv_ref[s], axis=0, keepdims=True)
        yc_rows.append(oc * jax.nn.silu(cgate[s:s + 1, :]))
    ya_scr[rows, :] = jnp.concatenate(ya_rows, axis=0)
    yc_scr[rows, :] = jnp.concatenate(yc_rows, axis=0)

    @pl.when(i == nsteps - 1)
    def _():
        y_ref[...] = _merge_and_project(x_ref[...], hb_scr[...], ya_scr[...],
                                        proj[:, COL_B:COL_B + WIDTH], yc_scr[...], w_ref, bgate_ref,
                                        wbra_ref, wbrb_ref, wbrc_ref, wout_ref, fng_ref)


def _sample_layer(x, cache_k, cache_v, s0, cb, wp, params):
    n = x.shape[0]
    sb = SAMPLE_BLOCK
    full = lambda a: pl.BlockSpec(a.shape, lambda i: (0,) * a.ndim, pipeline_mode=pl.Buffered(1))
    in_specs = [
        full(x),
        pl.BlockSpec((sb, MEM_LEN, WIDTH), lambda i: (i, 0, 0)),
        pl.BlockSpec((sb, MEM_LEN, WIDTH), lambda i: (i, 0, 0)),
        pl.BlockSpec((sb, HEADS, HEAD_DIM, HEAD_DIM), lambda i: (i, 0, 0, 0)),
        pl.BlockSpec((sb, CONV_WIDTH - 1, CONV_CH), lambda i: (i, 0, 0)),
        full(wp),
    ] + [full(p) for p in params]
    out_specs = [
        pl.BlockSpec((n, D_MODEL), lambda i: (0, 0)),
        pl.BlockSpec((sb, HEADS, HEAD_DIM, HEAD_DIM), lambda i: (i, 0, 0, 0)),
        pl.BlockSpec((sb, CONV_WIDTH - 1, CONV_CH), lambda i: (i, 0, 0)),
        pl.BlockSpec((n, WIDTH), lambda i: (0, 0)),
    ]
    out_shape = [
        jax.ShapeDtypeStruct((n, D_MODEL), F32),
        jax.ShapeDtypeStruct((n, HEADS, HEAD_DIM, HEAD_DIM), F32),
        jax.ShapeDtypeStruct((n, CONV_WIDTH - 1, CONV_CH), F32),
        jax.ShapeDtypeStruct((n, WIDTH), F32),
    ]
    return pl.pallas_call(
        _sample_kernel,
        grid=(n // sb,),
        in_specs=in_specs,
        out_specs=out_specs,
        out_shape=out_shape,
        scratch_shapes=[pltpu.VMEM((n, D_MODEL), BF16),
                        pltpu.VMEM((n, COL_MG), F32),
                        pltpu.VMEM((n, LANES), F32),
                        pltpu.VMEM((n, WIDTH), F32),
                        pltpu.VMEM((n, WIDTH), F32)],
        compiler_params=pltpu.CompilerParams(dimension_semantics=("arbitrary",),
                                             vmem_limit_bytes=VMEM_LIMIT),
        name="sample_layer",
    )(x, cache_k, cache_v, s0, cb, wp, *params)


def _pack_input_projection(w):
    a_end = 2 * WIDTH + WIDTH
    bg_end = a_end + 2 * HEADS
    agate_end = bg_end + WIDTH
    b_end = agate_end + 3 * WIDTH
    c_end = b_end + 2 * WIDTH
    pad = jnp.zeros((D_MODEL, PACKED_COLS - COL_BG - 2 * HEADS), w.dtype)
    packed = jnp.concatenate([w[:, :a_end], w[:, bg_end:agate_end], w[:, agate_end:b_end],
                              w[:, b_end:c_end], w[:, c_end:], w[:, a_end:bg_end], pad], axis=1)
    return packed.astype(BF16), w[:, a_end:bg_end].T.astype(BF16)


def _lanes_4_to_7(vec):
    return jnp.zeros((1, LANES), F32).at[0, HEADS:2 * HEADS].set(vec)


def kernel(x_prompt, x_sample, cache_mem_k, cache_mem_v, state_delta, state_conv, mem_prompt, norm_g, w_in, conv_w, a_log, dt_bias, a_norm_g, ln_v_g, ln_v_b, w_spatial, b_spatial, mem_norm_g, w_mem_kv, w_br_a, w_br_b, w_br_c, b_gate, w_out, final_norm_g):
    depth = norm_g.shape[0]
    assert depth == 1, "single-layer step"
    bsz, seq, _ = x_prompt.shape
    nsmp = x_sample.shape[0]
    assert x_sample.shape[1] == 1 and seq % PROMPT_BLOCK == 0 and nsmp % SAMPLE_BLOCK == 0

    wp, wbgt = _pack_input_projection(w_in[0])
    params = (
        norm_g[0][None, :], conv_w[0], _lanes_4_to_7(a_log[0]), _lanes_4_to_7(dt_bias[0]),
        a_norm_g[0][None, :], ln_v_g[0][None, :], ln_v_b[0][None, :], w_spatial[0],
        jnp.broadcast_to(b_spatial[0][:, :, None], (MLP_GROUPS, MLP_CHUNK, WIDTH // MLP_GROUPS)),
        w_br_a[0].astype(BF16), w_br_b[0].astype(BF16), w_br_c[0].astype(BF16),
        b_gate[0], w_out[0].astype(BF16), final_norm_g[None, :],
    )

    mk, mv, mkb, mvb = _memory_kv(mem_prompt.reshape(bsz * MEM_LEN, D_MODEL), mem_norm_g[0][None, :],
                                  w_mem_kv[0].astype(BF16))
    y_p, sd_p, sc_p = _prompt_layer(x_prompt, mkb.reshape(bsz, MEM_LEN, WIDTH),
                                    mvb.reshape(bsz, MEM_LEN, WIDTH), wp, wbgt, params)
    y_s, sd_s, sc_s, vn_s = _sample_layer(
        x_sample.reshape(nsmp, D_MODEL), cache_mem_k[0].reshape(nsmp, MEM_LEN, WIDTH),
        cache_mem_v[0].reshape(nsmp, MEM_LEN, WIDTH), state_delta[0], state_conv[0], wp, params)

    kv_shape = (1, bsz, MEM_LEN, HEADS, HEAD_DIM)
    return (y_p, y_s.reshape(nsmp, 1, D_MODEL), sd_p[None], sc_p[None], mk.reshape(kv_shape),
            mv.reshape(kv_shape), sd_s[None], sc_s[None], vn_s.reshape(1, nsmp, 1, WIDTH))
```

```python
import jax
import jax.numpy as jnp
from jax import lax
from jax.experimental import pallas as pl
from jax.experimental.pallas import tpu as pltpu

F32 = jnp.float32
BF16 = jnp.bfloat16

D_MODEL = 1024
HEADS = 4
HEAD_DIM = 128
WIDTH = HEADS * HEAD_DIM
CONV_WIDTH = 4
CONV_CH = 3 * WIDTH
MLP_GROUPS = 4
MLP_CHUNK = 128
MEM_LEN = 256
N_BRANCH = 3
EPS = 1e-6

LANES = 128
SUBLANES = 8

REST_AGATE = 0
REST_B = REST_AGATE + WIDTH
REST_C = REST_B + 3 * WIDTH
REST_MG = REST_C + 2 * WIDTH
REST_COLS = REST_MG + N_BRANCH * D_MODEL

PROMPT_BLOCK = 256
DELTA_CHUNK = 64
SAMPLE_BLOCK = 8
VMEM_LIMIT = 56 * 1024 * 1024


def _dot(a, b):
    return jnp.dot(a.astype(BF16), b.astype(BF16), preferred_element_type=F32)


def _dot_nt(a, b):
    return lax.dot_general(a.astype(BF16), b.astype(BF16), (((1,), (1,)), ((), ())),
                           preferred_element_type=F32)


def _dot_exact(a, b):
    return jnp.dot(a, b, preferred_element_type=F32, precision=lax.Precision.HIGHEST)


def _rms(x):
    return x * lax.rsqrt(jnp.mean(x * x, axis=-1, keepdims=True) + EPS)


def _l2norm(x):
    return x * lax.rsqrt(jnp.sum(x * x, axis=-1, keepdims=True) + EPS)


def _softplus(x):
    return jnp.maximum(x, 0.0) + jnp.log1p(jnp.exp(-jnp.abs(x)))


def _iota2(shape, dim):
    return lax.broadcasted_iota(jnp.int32, shape, dim)


def _memkv_kernel(mem_ref, g_ref, w_ref, k_ref, v_ref, kb_ref, vb_ref):
    xn = _rms(mem_ref[...]) * g_ref[...]
    kv = _dot(xn, w_ref[...])
    k = kv[:, :WIDTH]
    v = kv[:, WIDTH:]
    k_ref[...] = k
    v_ref[...] = v
    kb_ref[...] = k.astype(BF16)
    vb_ref[...] = v.astype(BF16)


def _memory_kv(mem2d, mem_norm_g, w_mem_kv_bf):
    rows = mem2d.shape[0]
    blk = 256
    full = lambda shape: pl.BlockSpec(shape, lambda i: (0,) * len(shape))
    row_spec = lambda width: pl.BlockSpec((blk, width), lambda i: (i, 0))
    return pl.pallas_call(
        _memkv_kernel,
        grid=(rows // blk,),
        in_specs=[row_spec(D_MODEL), full((1, D_MODEL)), full((D_MODEL, 2 * WIDTH))],
        out_specs=[row_spec(WIDTH)] * 4,
        out_shape=[jax.ShapeDtypeStruct((rows, WIDTH), F32)] * 2
        + [jax.ShapeDtypeStruct((rows, WIDTH), BF16)] * 2,
        compiler_params=pltpu.CompilerParams(dimension_semantics=("arbitrary",),
                                             vmem_limit_bytes=VMEM_LIMIT),
        name="memory_kv",
    )(mem2d, mem_norm_g, w_mem_kv_bf)


def _beta_and_log_decay(bg, alog, dtb):
    beta = jax.nn.sigmoid(bg)
    g = -jnp.exp(alog) * _softplus(bg + dtb)
    return beta, g


def _spatial_weights(ws_ref):
    tril = _iota2((MLP_CHUNK, MLP_CHUNK), 0) >= _iota2((MLP_CHUNK, MLP_CHUNK), 1)
    return [jnp.where(tril, ws_ref[g], 0.0).astype(BF16) for g in range(MLP_GROUPS)]


def _layernorm(v, g, b):
    mu = jnp.mean(v, axis=-1, keepdims=True)
    vc = v - mu
    return vc * lax.rsqrt(jnp.mean(vc * vc, axis=-1, keepdims=True) + EPS) * g + b


def _merge_and_project(x, hb, ya, yb, yc, wrest_ref, bgate_ref, wbra_ref, wbrb_ref, wbrc_ref, wout_ref,
                       fng_ref):
    merged = None
    for i, (yi, wbr) in enumerate(((ya, wbra_ref), (yb, wbrb_ref), (yc, wbrc_ref))):
        lo = REST_MG + i * D_MODEL
        gate = jax.nn.sigmoid(_dot(hb, wrest_ref[:, lo:lo + D_MODEL]) + bgate_ref[i:i + 1, :])
        term = gate * _dot(yi, wbr[...])
        merged = term if merged is None else merged + term
    out = x + _dot(merged, wout_ref[...])
    return _rms(out) * fng_ref[...]


def _col_from_lanes(row):
    sel = _iota2((SUBLANES, LANES), 0) == _iota2((SUBLANES, LANES), 1)
    return jnp.sum(jnp.where(sel, row, 0.0), axis=-1, keepdims=True)


def _delta_chunk_terms(q, k, v, beta_col, gc_col, gc_row):
    ck = q.shape[0]
    ri = _iota2((ck, ck), 0)
    ci = _iota2((ck, ck), 1)
    kt = k.T
    gc_last = gc_row[:, ck - 1:ck]
    decay = jnp.exp(jnp.where(ri >= ci, gc_col - gc_row, -1e30))
    kb = k * beta_col
    egc = jnp.exp(gc_col)
    return dict(
        low=jnp.where(ri > ci, _dot(kb, kt) * decay, 0.0),
        attn=jnp.where(ri >= ci, _dot(q, kt) * decay, 0.0),
        rhs=jnp.concatenate([v * beta_col, kb * egc], axis=1),
        qd=q * egc,
        kdt=kt * jnp.exp(gc_last - gc_row),
        gl=jnp.exp(gc_last),
    )


def _inv_unit_lower_all(lows):
    n = lows[0].shape[0]
    eye = (_iota2((n, n), 0) == _iota2((n, n), 1)).astype(F32)
    powers = [-low for low in lows]
    invs = [eye + p for p in powers]
    k = 1
    while 2 * k < n:
        powers = [_dot(p, p) for p in powers]
        invs = [inv + _dot(inv, p) for inv, p in zip(invs, powers)]
        k *= 2
    return invs


def _prompt_kernel(x_ref, mk_ref, mv_ref, wqkv_ref, wrest_ref, wbg_ref, wbgt_ref, normg_ref, convw_ref,
                   alog_ref, dtb_ref, anormg_ref, lng_ref, lnb_ref, ws_ref, bs_ref, wbra_ref, wbrb_ref,
                   wbrc_ref, bgate_ref, wout_ref, fng_ref,
                   y_ref, sd_ref, sc_ref,
                   convbuf, state):
    tb = PROMPT_BLOCK
    ck = DELTA_CHUNK
    nck = tb // ck
    t = pl.program_id(1)

    @pl.when(t == 0)
    def _():
        convbuf[0:SUBLANES, :] = jnp.zeros((SUBLANES, CONV_CH), F32)
        state[...] = jnp.zeros_like(state)

    x = x_ref[0]
    hb = (_rms(x) * normg_ref[...]).astype(BF16)

    pre = _dot(hb, wqkv_ref[...])
    convbuf[SUBLANES:SUBLANES + tb, :] = pre
    conv = pre * convw_ref[CONV_WIDTH - 1:CONV_WIDTH, :]
    for j in range(CONV_WIDTH - 1):
        shift = CONV_WIDTH - 1 - j
        conv = conv + convbuf[SUBLANES - shift:SUBLANES - shift + tb, :] * convw_ref[j:j + 1, :]
    tail = convbuf[tb:tb + SUBLANES, :]
    convbuf[0:SUBLANES, :] = tail
    sc_ref[0] = tail[SUBLANES - (CONV_WIDTH - 1):, :]
    qkv = jax.nn.silu(conv)

    bg_col = _dot(hb, wbg_ref[...])
    bg_row = _dot_nt(wbgt_ref[...], hb)
    beta_col, g_col = _beta_and_log_decay(bg_col, alog_ref[...], dtb_ref[...])
    g_row = (-jnp.exp(_col_from_lanes(alog_ref[...]))
             * _softplus(bg_row + _col_from_lanes(dtb_ref[...])))

    tril_f = (_iota2((ck, ck), 0) >= _iota2((ck, ck), 1)).astype(F32)
    triu_f = (_iota2((ck, ck), 0) <= _iota2((ck, ck), 1)).astype(F32)
    gc_cols = [_dot_exact(tril_f, g_col[c * ck:(c + 1) * ck, :]) for c in range(nck)]
    gc_rows = [_dot_exact(g_row[:, c * ck:(c + 1) * ck], triu_f) for c in range(nck)]

    pairs = [(c, hd) for c in range(nck) for hd in range(HEADS)]
    qn, kn, vs = [], [], []
    for hd in range(HEADS):
        lo = hd * HEAD_DIM
        qn.append(_l2norm(qkv[:, lo:lo + HEAD_DIM]) * (HEAD_DIM ** -0.5))
        kn.append(_l2norm(qkv[:, WIDTH + lo:WIDTH + lo + HEAD_DIM]))
        vs.append(qkv[:, 2 * WIDTH + lo:2 * WIDTH + lo + HEAD_DIM])
    terms = {}
    for c, hd in pairs:
        rows = slice(c * ck, (c + 1) * ck)
        terms[c, hd] = _delta_chunk_terms(
            qn[hd][rows], kn[hd][rows], vs[hd][rows], beta_col[rows, hd:hd + 1],
            gc_cols[c][:, HEADS + hd:HEADS + hd + 1], gc_rows[c][HEADS + hd:HEADS + hd + 1, :])
    invs = _inv_unit_lower_all([terms[p]["low"] for p in pairs])
    uws = {p: _dot(inv, terms[p]["rhs"]) for p, inv in zip(pairs, invs)}

    def branch_b():
        pb = _dot(hb, wrest_ref[:, REST_B:REST_B + 3 * WIDTH])
        bu = pb[:, :WIDTH]
        vn = _layernorm(pb[:, WIDTH:2 * WIDTH], lng_ref[...], lnb_ref[...])
        bgate = pb[:, 2 * WIDTH:]
        ws = _spatial_weights(ws_ref)
        gw = WIDTH // MLP_GROUPS
        s_rows = []
        for n in range(tb // MLP_CHUNK):
            r0 = n * MLP_CHUNK
            s_rows.append(jnp.concatenate(
                [_dot(ws[g], vn[r0:r0 + MLP_CHUNK, g * gw:(g + 1) * gw]) + bs_ref[g]
                 for g in range(MLP_GROUPS)], axis=1))
        return bu * jnp.concatenate(s_rows, axis=0) * jax.nn.silu(bgate)

    def attend(cq, hd):
        lo = hd * HEAD_DIM
        s = _dot_nt(cq[:, lo:lo + HEAD_DIM], mk_ref[0, :, lo:lo + HEAD_DIM]) * (HEAD_DIM ** -0.5)
        s = s - jnp.max(s, axis=-1, keepdims=True)
        p = jnp.exp(s)
        p = p / jnp.sum(p, axis=-1, keepdims=True)
        return _dot(p, mv_ref[0, :, lo:lo + HEAD_DIM])

    s_heads = [state[hd] for hd in range(HEADS)]
    o_chunks = [[] for _ in range(HEADS)]
    side = {}
    for c in range(nck):
        for hd in range(HEADS):
            tm = terms[c, hd]
            uw = uws[c, hd]
            v_new = uw[:, :HEAD_DIM] - _dot(uw[:, HEAD_DIM:], s_heads[hd])
            o_chunks[hd].append(_dot(tm["qd"], s_heads[hd]) + _dot(tm["attn"], v_new))
            s_heads[hd] = s_heads[hd] * tm["gl"] + _dot(tm["kdt"], v_new)
        if c == 0:
            side["yb"] = branch_b()
        elif c == 1:
            side["pc"] = _dot(hb, wrest_ref[:, REST_C:REST_C + 2 * WIDTH])
            side["oc"] = [attend(side["pc"][:, :WIDTH], hd) for hd in range(2)]
        elif c == 2:
            side["oc"] += [attend(side["pc"][:, :WIDTH], hd) for hd in range(2, HEADS)]
        elif c == 3:
            side["agate"] = _dot(hb, wrest_ref[:, REST_AGATE:REST_AGATE + WIDTH])
    for hd in range(HEADS):
        state[hd] = s_heads[hd]
    sd_ref[0] = state[...]

    o_heads = []
    for hd in range(HEADS):
        lo = hd * HEAD_DIM
        o = jnp.concatenate(o_chunks[hd], axis=0)
        o_heads.append(_rms(o) * anormg_ref[...] * jax.nn.silu(side["agate"][:, lo:lo + HEAD_DIM]))
    ya = jnp.concatenate(o_heads, axis=1)
    yc = jnp.concatenate(side["oc"], axis=1) * jax.nn.silu(side["pc"][:, WIDTH:])

    y_ref[0] = _merge_and_project(x, hb, ya, side["yb"], yc, wrest_ref, bgate_ref, wbra_ref, wbrb_ref,
                                  wbrc_ref, wout_ref, fng_ref)


def _prompt_layer(x, mkb, mvb, weights, params):
    bsz, seq, _ = x.shape
    tb = PROMPT_BLOCK
    assert tb // DELTA_CHUNK == 4, "the side work of the recurrence is split over four chunk steps"
    nt = seq // tb
    full = lambda a: pl.BlockSpec(a.shape, lambda b, t: (0,) * a.ndim, pipeline_mode=pl.Buffered(1))
    in_specs = [
        pl.BlockSpec((1, tb, D_MODEL), lambda b, t: (b, t, 0)),
        pl.BlockSpec((1, MEM_LEN, WIDTH), lambda b, t: (b, 0, 0)),
        pl.BlockSpec((1, MEM_LEN, WIDTH), lambda b, t: (b, 0, 0)),
    ] + [full(w) for w in weights] + [full(p) for p in params]
    out_specs = [
        pl.BlockSpec((1, tb, D_MODEL), lambda b, t: (b, t, 0)),
        pl.BlockSpec((1, HEADS, HEAD_DIM, HEAD_DIM), lambda b, t: (b, 0, 0, 0)),
        pl.BlockSpec((1, CONV_WIDTH - 1, CONV_CH), lambda b, t: (b, 0, 0)),
    ]
    out_shape = [
        jax.ShapeDtypeStruct((bsz, seq, D_MODEL), F32),
        jax.ShapeDtypeStruct((bsz, HEADS, HEAD_DIM, HEAD_DIM), F32),
        jax.ShapeDtypeStruct((bsz, CONV_WIDTH - 1, CONV_CH), F32),
    ]
    return pl.pallas_call(
        _prompt_kernel,
        grid=(bsz, nt),
        in_specs=in_specs,
        out_specs=out_specs,
        out_shape=out_shape,
        scratch_shapes=[pltpu.VMEM((tb + SUBLANES, CONV_CH), F32),
                        pltpu.VMEM((HEADS, HEAD_DIM, HEAD_DIM), F32)],
        compiler_params=pltpu.CompilerParams(dimension_semantics=("arbitrary", "arbitrary"),
                                             vmem_limit_bytes=VMEM_LIMIT),
        name="prompt_layer",
    )(x, mkb, mvb, *weights, *params)


def _sample_kernel(x_ref, ck_ref, cv_ref, s0_ref, cb_ref, wqkv_ref, wrest_ref, wbg_ref, wbgt_ref,
                   normg_ref, convw_ref, alog_ref, dtb_ref, anormg_ref, lng_ref, lnb_ref, ws_ref, bs_ref,
                   wbra_ref, wbrb_ref, wbrc_ref, bgate_ref, wout_ref, fng_ref,
                   y_ref, sd_ref, sc_ref, vn_ref,
                   hb_scr, pre_scr, rest_scr, bg_scr, ya_scr, yc_scr):
    del wbgt_ref
    sb = SAMPLE_BLOCK
    i = pl.program_id(0)
    nsteps = pl.num_programs(0)

    @pl.when(i == 0)
    def _():
        hb = (_rms(x_ref[...]) * normg_ref[...]).astype(BF16)
        hb_scr[...] = hb
        pre_scr[...] = _dot(hb, wqkv_ref[...])
        rest_scr[...] = _dot(hb, wrest_ref[:, :REST_MG])
        bg_scr[...] = _dot(hb, wbg_ref[...])
        bu = rest_scr[:, REST_B:REST_B + WIDTH]
        vn = _layernorm(rest_scr[:, REST_B + WIDTH:REST_B + 2 * WIDTH], lng_ref[...], lnb_ref[...])
        vn_ref[...] = vn
        bgate = rest_scr[:, REST_B + 2 * WIDTH:REST_B + 3 * WIDTH]
        gw = WIDTH // MLP_GROUPS
        s = jnp.concatenate(
            [vn[:, g * gw:(g + 1) * gw] * ws_ref[g, 0:1, 0:1] + bs_ref[g, 0:1, :]
             for g in range(MLP_GROUPS)], axis=1)
        rest_scr[:, REST_B:REST_B + WIDTH] = bu * s * jax.nn.silu(bgate)

    r0 = pl.multiple_of(i * sb, sb)
    rows = pl.ds(r0, sb)
    pre = pre_scr[rows, :]
    beta, g = _beta_and_log_decay(bg_scr[rows, :], alog_ref[...], dtb_ref[...])
    decay = jnp.exp(g)
    agate = rest_scr[rows, REST_AGATE:REST_AGATE + WIDTH]
    cq = rest_scr[rows, REST_C:REST_C + WIDTH]
    cgate = rest_scr[rows, REST_C + WIDTH:REST_C + 2 * WIDTH]

    eye = _iota2((HEAD_DIM, HEAD_DIM), 0) == _iota2((HEAD_DIM, HEAD_DIM), 1)
    own_head = ((_iota2((SUBLANES, MEM_LEN * HEADS), 1) % HEADS)
                == (_iota2((SUBLANES, MEM_LEN * HEADS), 0) % HEADS))

    ya_rows = []
    yc_rows = []
    for s in range(sb):
        cb = cb_ref[s]
        pre_s = pre[s:s + 1, :]
        conv = pre_s * convw_ref[CONV_WIDTH - 1:CONV_WIDTH, :]
        for j in range(CONV_WIDTH - 1):
            conv = conv + cb[j:j + 1, :] * convw_ref[j:j + 1, :]
        sc_ref[s] = jnp.concatenate([cb[1:, :], pre_s], axis=0)
        qkv = jax.nn.silu(conv)
        o_heads = []
        for hd in range(HEADS):
            lo = hd * HEAD_DIM
            q = _l2norm(qkv[:, lo:lo + HEAD_DIM]) * (HEAD_DIM ** -0.5)
            k = _l2norm(qkv[:, WIDTH + lo:WIDTH + lo + HEAD_DIM])
            v = qkv[:, 2 * WIDTH + lo:2 * WIDTH + lo + HEAD_DIM]
            a = decay[s:s + 1, HEADS + hd:HEADS + hd + 1]
            bt = beta[s:s + 1, hd:hd + 1]
            st = s0_ref[s, hd]
            kcol = jnp.sum(jnp.where(eye, k, 0.0), axis=-1, keepdims=True)
            qcol = jnp.sum(jnp.where(eye, q, 0.0), axis=-1, keepdims=True)
            sk = jnp.sum(st * kcol, axis=0, keepdims=True)
            sq = jnp.sum(st * qcol, axis=0, keepdims=True)
            v_new = bt * (v - a * sk)
            sd_ref[s, hd] = a * st + kcol * v_new
            qk = jnp.sum(q * k, axis=-1, keepdims=True)
            o = a * sq + qk * v_new
            o_heads.append(_rms(o) * anormg_ref[...] * jax.nn.silu(agate[s:s + 1, lo:lo + HEAD_DIM]))
        ya_rows.append(jnp.concatenate(o_heads, axis=1))

        qh = jnp.concatenate([cq[s:s + 1, hd * HEAD_DIM:(hd + 1) * HEAD_DIM] for hd in range(HEADS)]
                             + [jnp.zeros((SUBLANES - HEADS, HEAD_DIM), F32)], axis=0)
        sc = _dot_nt(qh, ck_ref[s]) * (HEAD_DIM ** -0.5)
        sc = jnp.where(own_head, sc, -1e30)
        sc = sc - jnp.max(sc, axis=-1, keepdims=True)
        p = jnp.where(own_head, jnp.exp(sc), 0.0)
        p = p / jnp.sum(p, axis=-1, keepdims=True)
        oc = _dot(p, cv_ref[s])
        oc = jnp.concatenate([oc[hd:hd + 1, :] for hd in range(HEADS)], axis=1)
        yc_rows.append(oc * jax.nn.silu(cgate[s:s + 1, :]))
    ya_scr[rows, :] = jnp.concatenate(ya_rows, axis=0)
    yc_scr[rows, :] = jnp.concatenate(yc_rows, axis=0)

    @pl.when(i == nsteps - 1)
    def _():
        y_ref[...] = _merge_and_project(x_ref[...], hb_scr[...], ya_scr[...],
                                        rest_scr[:, REST_B:REST_B + WIDTH], yc_scr[...], wrest_ref,
                                        bgate_ref, wbra_ref, wbrb_ref, wbrc_ref, wout_ref, fng_ref)


def _sample_layer(x, cache_k, cache_v, s0, cb, weights, params):
    n = x.shape[0]
    sb = SAMPLE_BLOCK
    full = lambda a: pl.BlockSpec(a.shape, lambda i: (0,) * a.ndim, pipeline_mode=pl.Buffered(1))
    in_specs = [
        full(x),
        pl.BlockSpec((sb, MEM_LEN * HEADS, HEAD_DIM), lambda i: (i, 0, 0)),
        pl.BlockSpec((sb, MEM_LEN * HEADS, HEAD_DIM), lambda i: (i, 0, 0)),
        pl.BlockSpec((sb, HEADS, HEAD_DIM, HEAD_DIM), lambda i: (i, 0, 0, 0)),
        pl.BlockSpec((sb, CONV_WIDTH - 1, CONV_CH), lambda i: (i, 0, 0)),
    ] + [full(w) for w in weights] + [full(p) for p in params]
    out_specs = [
        pl.BlockSpec((n, D_MODEL), lambda i: (0, 0)),
        pl.BlockSpec((sb, HEADS, HEAD_DIM, HEAD_DIM), lambda i: (i, 0, 0, 0)),
        pl.BlockSpec((sb, CONV_WIDTH - 1, CONV_CH), lambda i: (i, 0, 0)),
        pl.BlockSpec((n, WIDTH), lambda i: (0, 0)),
    ]
    out_shape = [
        jax.ShapeDtypeStruct((n, D_MODEL), F32),
        jax.ShapeDtypeStruct((n, HEADS, HEAD_DIM, HEAD_DIM), F32),
        jax.ShapeDtypeStruct((n, CONV_WIDTH - 1, CONV_CH), F32),
        jax.ShapeDtypeStruct((n, WIDTH), F32),
    ]
    return pl.pallas_call(
        _sample_kernel,
        grid=(n // sb,),
        in_specs=in_specs,
        out_specs=out_specs,
        out_shape=out_shape,
        scratch_shapes=[pltpu.VMEM((n, D_MODEL), BF16),
                        pltpu.VMEM((n, CONV_CH), F32),
                        pltpu.VMEM((n, REST_MG), F32),
                        pltpu.VMEM((n, LANES), F32),
                        pltpu.VMEM((n, WIDTH), F32),
                        pltpu.VMEM((n, WIDTH), F32)],
        compiler_params=pltpu.CompilerParams(dimension_semantics=("arbitrary",),
                                             vmem_limit_bytes=VMEM_LIMIT),
        name="sample_layer",
    )(x, cache_k, cache_v, s0, cb, *weights, *params)


def _split_input_projection(w):
    qkv_end = CONV_CH
    bg_end = qkv_end + 2 * HEADS
    wbg = w[:, qkv_end:bg_end].astype(BF16)
    wbg_pad = jnp.zeros((D_MODEL, LANES), BF16).at[:, :2 * HEADS].set(wbg)
    return w[:, :qkv_end].astype(BF16), w[:, bg_end:].astype(BF16), wbg_pad, wbg.T


def _lanes_4_to_7(vec):
    return jnp.zeros((1, LANES), F32).at[0, HEADS:2 * HEADS].set(vec)


def kernel(x_prompt, x_sample, cache_mem_k, cache_mem_v, state_delta, state_conv, mem_prompt, norm_g, w_in, conv_w, a_log, dt_bias, a_norm_g, ln_v_g, ln_v_b, w_spatial, b_spatial, mem_norm_g, w_mem_kv, w_br_a, w_br_b, w_br_c, b_gate, w_out, final_norm_g):
    depth = norm_g.shape[0]
    assert depth == 1, "single-layer step"
    bsz, seq, _ = x_prompt.shape
    nsmp = x_sample.shape[0]
    assert x_sample.shape[1] == 1 and seq % PROMPT_BLOCK == 0 and nsmp % SAMPLE_BLOCK == 0
    assert w_in.shape[2] == CONV_CH + 2 * HEADS + REST_COLS

    weights = _split_input_projection(w_in[0])
    params = (
        norm_g[0][None, :], conv_w[0], _lanes_4_to_7(a_log[0]), _lanes_4_to_7(dt_bias[0]),
        a_norm_g[0][None, :], ln_v_g[0][None, :], ln_v_b[0][None, :], w_spatial[0],
        jnp.broadcast_to(b_spatial[0][:, :, None], (MLP_GROUPS, MLP_CHUNK, WIDTH // MLP_GROUPS)),
        w_br_a[0].astype(BF16), w_br_b[0].astype(BF16), w_br_c[0].astype(BF16),
        b_gate[0], w_out[0].astype(BF16), final_norm_g[None, :],
    )

    mk, mv, mkb, mvb = _memory_kv(mem_prompt.reshape(bsz * MEM_LEN, D_MODEL), mem_norm_g[0][None, :],
                                  w_mem_kv[0].astype(BF16))
    y_p, sd_p, sc_p = _prompt_layer(x_prompt, mkb.reshape(bsz, MEM_LEN, WIDTH),
                                    mvb.reshape(bsz, MEM_LEN, WIDTH), weights, params)
    y_s, sd_s, sc_s, vn_s = _sample_layer(
        x_sample.reshape(nsmp, D_MODEL), cache_mem_k.reshape(nsmp, MEM_LEN * HEADS, HEAD_DIM),
        cache_mem_v.reshape(nsmp, MEM_LEN * HEADS, HEAD_DIM), state_delta.reshape(state_delta.shape[1:]),
        state_conv.reshape(state_conv.shape[1:]), weights, params)

    kv_shape = (1, bsz, MEM_LEN, HEADS, HEAD_DIM)
    return (y_p, y_s.reshape(nsmp, 1, D_MODEL), sd_p[None], sc_p[None], mk.reshape(kv_shape),
            mv.reshape(kv_shape), sd_s[None], sc_s[None], vn_s.reshape(1, nsmp, 1, WIDTH))
```

```python
import jax
import jax.numpy as jnp
from jax import lax
from jax.experimental import pallas as pl
from jax.experimental.pallas import tpu as pltpu

F32 = jnp.float32
BF16 = jnp.bfloat16

D_MODEL = 1024
HEADS = 4
HEAD_DIM = 128
WIDTH = HEADS * HEAD_DIM
CONV_WIDTH = 4
CONV_CH = 3 * WIDTH
MLP_GROUPS = 4
MLP_CHUNK = 128
MEM_LEN = 256
N_BRANCH = 3
EPS = 1e-6

LANES = 128
SUBLANES = 8

REST_AGATE = 0
REST_B = REST_AGATE + WIDTH
REST_C = REST_B + 3 * WIDTH
REST_MG = REST_C + 2 * WIDTH
REST_COLS = REST_MG + N_BRANCH * D_MODEL

PROMPT_BLOCK = 512
DELTA_CHUNK = 64
SAMPLE_BLOCK = 8
VMEM_LIMIT = 56 * 1024 * 1024


def _dot(a, b):
    return jnp.dot(a.astype(BF16), b.astype(BF16), preferred_element_type=F32)


def _dot_nt(a, b):
    return lax.dot_general(a.astype(BF16), b.astype(BF16), (((1,), (1,)), ((), ())),
                           preferred_element_type=F32)


def _dot_exact(a, b):
    return jnp.dot(a, b, preferred_element_type=F32, precision=lax.Precision.HIGHEST)


def _rms(x):
    return x * lax.rsqrt(jnp.mean(x * x, axis=-1, keepdims=True) + EPS)


def _l2norm(x):
    return x * lax.rsqrt(jnp.sum(x * x, axis=-1, keepdims=True) + EPS)


def _softplus(x):
    return jnp.maximum(x, 0.0) + jnp.log1p(jnp.exp(-jnp.abs(x)))


def _iota2(shape, dim):
    return lax.broadcasted_iota(jnp.int32, shape, dim)


def _memkv_kernel(mem_ref, g_ref, w_ref, k_ref, v_ref, kb_ref, vb_ref):
    xn = _rms(mem_ref[...]) * g_ref[...]
    kv = _dot(xn, w_ref[...])
    k = kv[:, :WIDTH]
    v = kv[:, WIDTH:]
    npos = k.shape[0]
    for hd in range(HEADS):
        k_ref[pl.ds(hd, npos, stride=HEADS), :] = k[:, hd * HEAD_DIM:(hd + 1) * HEAD_DIM]
        v_ref[pl.ds(hd, npos, stride=HEADS), :] = v[:, hd * HEAD_DIM:(hd + 1) * HEAD_DIM]
    kb_ref[...] = k.astype(BF16)
    vb_ref[...] = v.astype(BF16)


def _memory_kv(mem2d, mem_norm_g, w_mem_kv_bf):
    rows = mem2d.shape[0]
    blk = 256
    full = lambda shape: pl.BlockSpec(shape, lambda i: (0,) * len(shape))
    row_spec = lambda width: pl.BlockSpec((blk, width), lambda i: (i, 0))
    by_head = pl.BlockSpec((blk * HEADS, HEAD_DIM), lambda i: (i, 0))
    return pl.pallas_call(
        _memkv_kernel,
        grid=(rows // blk,),
        in_specs=[row_spec(D_MODEL), full((1, D_MODEL)), full((D_MODEL, 2 * WIDTH))],
        out_specs=[by_head] * 2 + [row_spec(WIDTH)] * 2,
        out_shape=[jax.ShapeDtypeStruct((rows * HEADS, HEAD_DIM), F32)] * 2
        + [jax.ShapeDtypeStruct((rows, WIDTH), BF16)] * 2,
        compiler_params=pltpu.CompilerParams(dimension_semantics=("arbitrary",),
                                             vmem_limit_bytes=VMEM_LIMIT),
        name="memory_kv",
    )(mem2d, mem_norm_g, w_mem_kv_bf)


def _beta_and_log_decay(bg, alog, dtb):
    beta = jax.nn.sigmoid(bg)
    g = -jnp.exp(alog) * _softplus(bg + dtb)
    return beta, g


def _spatial_weights(ws_ref):
    tril = _iota2((MLP_CHUNK, MLP_CHUNK), 0) >= _iota2((MLP_CHUNK, MLP_CHUNK), 1)
    return [jnp.where(tril, ws_ref[g], 0.0).astype(BF16) for g in range(MLP_GROUPS)]


def _layernorm(v, g, b):
    mu = jnp.mean(v, axis=-1, keepdims=True)
    vc = v - mu
    return vc * lax.rsqrt(jnp.mean(vc * vc, axis=-1, keepdims=True) + EPS) * g + b


def _merge_and_project(x, hb, ya, yb, yc, wrest_ref, bgate_ref, wbra_ref, wbrb_ref, wbrc_ref, wout_ref,
                       fng_ref):
    merged = None
    for i, (yi, wbr) in enumerate(((ya, wbra_ref), (yb, wbrb_ref), (yc, wbrc_ref))):
        lo = i * D_MODEL
        gate = jax.nn.sigmoid(_dot(hb, wrest_ref[:, REST_MG + lo:REST_MG + lo + D_MODEL])
                              + bgate_ref[:, lo:lo + D_MODEL])
        term = gate * _dot(yi, wbr[...])
        merged = term if merged is None else merged + term
    out = x + _dot(merged, wout_ref[...])
    return _rms(out) * fng_ref[...]


def _col_from_lanes(row):
    sel = _iota2((SUBLANES, LANES), 0) == _iota2((SUBLANES, LANES), 1)
    return jnp.sum(jnp.where(sel, row, 0.0), axis=-1, keepdims=True)


def _delta_chunk_terms(q, k, v, beta_col, gc_col, gc_row):
    ck = q.shape[0]
    ri = _iota2((ck, ck), 0)
    ci = _iota2((ck, ck), 1)
    kt = k.T
    gc_last = gc_row[:, ck - 1:ck]
    decay = jnp.exp(jnp.where(ri >= ci, gc_col - gc_row, -1e30))
    kb = k * beta_col
    egc = jnp.exp(gc_col)
    return dict(
        low=jnp.where(ri > ci, _dot(kb, kt) * decay, 0.0),
        attn=jnp.where(ri >= ci, _dot(q, kt) * decay, 0.0),
        rhs=jnp.concatenate([v * beta_col, kb * egc], axis=1),
        qd=q * egc,
        kdt=kt * jnp.exp(gc_last - gc_row),
        gl=jnp.exp(gc_last),
    )


def _prompt_kernel(x_ref, mk_ref, mv_ref, wqkv_ref, wrest_ref, wbg_ref, normg_ref, convw_ref,
                   alog_ref, dtb_ref, anormg_ref, lng_ref, lnb_ref, ws_ref, bs_ref, wbra_ref, wbrb_ref,
                   wbrc_ref, bgate_ref, wout_ref, fng_ref,
                   y_ref, sd_ref, sc_ref,
                   convbuf, state):
    tb = PROMPT_BLOCK
    ck = DELTA_CHUNK
    nck = tb // ck
    t = pl.program_id(1)

    @pl.when(t == 0)
    def _():
        convbuf[0:SUBLANES, :] = jnp.zeros((SUBLANES, CONV_CH), F32)
        state[...] = jnp.zeros_like(state)

    x = x_ref[0]
    hb = (_rms(x) * normg_ref[...]).astype(BF16)

    pre = _dot(hb, wqkv_ref[...])
    bg_col = _dot(hb, wbg_ref[...])
    bg_row = jnp.concatenate([bg_col[r:r + LANES, :].T[:SUBLANES, :] for r in range(0, tb, LANES)],
                             axis=1)
    pb = _dot(hb, wrest_ref[:, REST_B:REST_B + 3 * WIDTH])
    pc = _dot(hb, wrest_ref[:, REST_C:REST_C + 2 * WIDTH])
    agate = _dot(hb, wrest_ref[:, REST_AGATE:REST_AGATE + WIDTH])

    convbuf[SUBLANES:SUBLANES + tb, :] = pre
    conv = pre * convw_ref[CONV_WIDTH - 1:CONV_WIDTH, :]
    for j in range(CONV_WIDTH - 1):
        shift = CONV_WIDTH - 1 - j
        conv = conv + convbuf[SUBLANES - shift:SUBLANES - shift + tb, :] * convw_ref[j:j + 1, :]
    tail = convbuf[tb:tb + SUBLANES, :]
    convbuf[0:SUBLANES, :] = tail
    sc_ref[0] = tail[SUBLANES - (CONV_WIDTH - 1):, :]
    qkv = jax.nn.silu(conv)

    beta_col, g_col = _beta_and_log_decay(bg_col, alog_ref[...], dtb_ref[...])
    g_row = (-jnp.exp(_col_from_lanes(alog_ref[...]))
             * _softplus(bg_row + _col_from_lanes(dtb_ref[...])))
    tril_f = (_iota2((ck, ck), 0) >= _iota2((ck, ck), 1)).astype(F32)
    triu_f = (_iota2((ck, ck), 0) <= _iota2((ck, ck), 1)).astype(F32)
    gc_cols = [_dot_exact(tril_f, g_col[c * ck:(c + 1) * ck, :]) for c in range(nck)]
    gc_rows = [_dot_exact(g_row[:, c * ck:(c + 1) * ck], triu_f) for c in range(nck)]

    pairs = [(c, hd) for c in range(nck) for hd in range(HEADS)]
    qn, kn, vs = [], [], []
    for hd in range(HEADS):
        lo = hd * HEAD_DIM
        qn.append(_l2norm(qkv[:, lo:lo + HEAD_DIM]) * (HEAD_DIM ** -0.5))
        kn.append(_l2norm(qkv[:, WIDTH + lo:WIDTH + lo + HEAD_DIM]))
        vs.append(qkv[:, 2 * WIDTH + lo:2 * WIDTH + lo + HEAD_DIM])
    terms = {}
    for c, hd in pairs:
        rows = slice(c * ck, (c + 1) * ck)
        terms[c, hd] = _delta_chunk_terms(
            qn[hd][rows], kn[hd][rows], vs[hd][rows], beta_col[rows, hd:hd + 1],
            gc_cols[c][:, HEADS + hd:HEADS + hd + 1], gc_rows[c][HEADS + hd:HEADS + hd + 1, :])

    eye = (_iota2((ck, ck), 0) == _iota2((ck, ck), 1)).astype(F32)
    powers = [-terms[p]["low"] for p in pairs]
    invs = [eye + p for p in powers]

    def inverse_level():
        nonlocal powers, invs
        powers = [_dot(p, p) for p in powers]
        invs = [inv + _dot(inv, p) for inv, p in zip(invs, powers)]

    levels = 0
    while 2 ** (levels + 1) < ck:
        levels += 1
    assert levels == 5, "the side work below is placed for five squaring levels"

    scores = [_dot_nt(pc[:, hd * HEAD_DIM:(hd + 1) * HEAD_DIM], mk_ref[0, :, hd * HEAD_DIM:(hd + 1) * HEAD_DIM])
              * (HEAD_DIM ** -0.5) for hd in range(HEADS)]
    inverse_level()
    bu = pb[:, :WIDTH]
    vn = _layernorm(pb[:, WIDTH:2 * WIDTH], lng_ref[...], lnb_ref[...])
    ws = _spatial_weights(ws_ref)
    gw = WIDTH // MLP_GROUPS
    s_rows = []
    for n in range(tb // MLP_CHUNK):
        r0 = n * MLP_CHUNK
        s_rows.append(jnp.concatenate(
            [_dot(ws[g], vn[r0:r0 + MLP_CHUNK, g * gw:(g + 1) * gw]) + bs_ref[g]
             for g in range(MLP_GROUPS)], axis=1))
    yb = bu * jnp.concatenate(s_rows, axis=0) * jax.nn.silu(pb[:, 2 * WIDTH:])
    inverse_level()
    inverse_level()
    oc = []
    for hd in range(HEADS):
        s = scores[hd] - jnp.max(scores[hd], axis=-1, keepdims=True)
        p = jnp.exp(s)
        p = p / jnp.sum(p, axis=-1, keepdims=True)
        oc.append(_dot(p, mv_ref[0, :, hd * HEAD_DIM:(hd + 1) * HEAD_DIM]))
    yc = jnp.concatenate(oc, axis=1) * jax.nn.silu(pc[:, WIDTH:])
    inverse_level()
    inverse_level()
    uws = {p: _dot(inv, terms[p]["rhs"]) for p, inv in zip(pairs, invs)}

    n_slots = 2 * nck
    gate_cols = 2 * D_MODEL // nck
    n_gate = N_BRANCH * D_MODEL // gate_cols
    out_parts = (n_slots - n_gate) // 2
    assert n_gate + 2 * out_parts == n_slots and D_MODEL % gate_cols == 0
    out_cols = D_MODEL // out_parts
    gate_parts = []
    out_b, out_c = [], []

    def gate_piece(j):
        lo = j * gate_cols
        gate_parts.append(jax.nn.sigmoid(_dot(hb, wrest_ref[:, REST_MG + lo:REST_MG + lo + gate_cols])
                                         + bgate_ref[:, lo:lo + gate_cols]))

    side = [lambda j=j: gate_piece(j) for j in range(n_gate)]
    side += [lambda j=j: out_b.append(_dot(yb, wbrb_ref[:, j * out_cols:(j + 1) * out_cols]))
             for j in range(out_parts)]
    side += [lambda j=j: out_c.append(_dot(yc, wbrc_ref[:, j * out_cols:(j + 1) * out_cols]))
             for j in range(out_parts)]

    s_heads = [state[hd] for hd in range(HEADS)]
    o_chunks = [[] for _ in range(HEADS)]
    for c in range(nck):
        ws_, qs_ = [], []
        for hd in range(HEADS):
            uw = uws[c, hd]
            ws_.append(_dot(uw[:, HEAD_DIM:], s_heads[hd]))
            qs_.append(_dot(terms[c, hd]["qd"], s_heads[hd]))
        side[2 * c]()
        for hd in range(HEADS):
            tm = terms[c, hd]
            v_new = uws[c, hd][:, :HEAD_DIM] - ws_[hd]
            o_chunks[hd].append(qs_[hd] + _dot(tm["attn"], v_new))
            s_heads[hd] = s_heads[hd] * tm["gl"] + _dot(tm["kdt"], v_new)
        side[2 * c + 1]()
    for hd in range(HEADS):
        state[hd] = s_heads[hd]
    sd_ref[0] = state[...]

    o_heads = []
    for hd in range(HEADS):
        lo = hd * HEAD_DIM
        o = jnp.concatenate(o_chunks[hd], axis=0)
        o_heads.append(_rms(o) * anormg_ref[...] * jax.nn.silu(agate[:, lo:lo + HEAD_DIM]))
    branch_out = [_dot(jnp.concatenate(o_heads, axis=1), wbra_ref[...]),
                  jnp.concatenate(out_b, axis=1), jnp.concatenate(out_c, axis=1)]
    gates = jnp.concatenate(gate_parts, axis=1)
    merged = None
    for i in range(N_BRANCH):
        term = gates[:, i * D_MODEL:(i + 1) * D_MODEL] * branch_out[i]
        merged = term if merged is None else merged + term
    out = x + _dot(merged, wout_ref[...])
    y_ref[0] = _rms(out) * fng_ref[...]


def _prompt_layer(x, mkb, mvb, weights, params):
    bsz, seq, _ = x.shape
    tb = PROMPT_BLOCK
    nt = seq // tb
    full = lambda a: pl.BlockSpec(a.shape, lambda b, t: (0,) * a.ndim, pipeline_mode=pl.Buffered(1))
    in_specs = [
        pl.BlockSpec((1, tb, D_MODEL), lambda b, t: (b, t, 0)),
        pl.BlockSpec((1, MEM_LEN, WIDTH), lambda b, t: (b, 0, 0)),
        pl.BlockSpec((1, MEM_LEN, WIDTH), lambda b, t: (b, 0, 0)),
    ] + [full(w) for w in weights] + [full(p) for p in params]
    out_specs = [
        pl.BlockSpec((1, tb, D_MODEL), lambda b, t: (b, t, 0)),
        pl.BlockSpec((1, HEADS, HEAD_DIM, HEAD_DIM), lambda b, t: (b, 0, 0, 0)),
        pl.BlockSpec((1, CONV_WIDTH - 1, CONV_CH), lambda b, t: (b, 0, 0)),
    ]
    out_shape = [
        jax.ShapeDtypeStruct((bsz, seq, D_MODEL), F32),
        jax.ShapeDtypeStruct((bsz, HEADS, HEAD_DIM, HEAD_DIM), F32),
        jax.ShapeDtypeStruct((bsz, CONV_WIDTH - 1, CONV_CH), F32),
    ]
    return pl.pallas_call(
        _prompt_kernel,
        grid=(bsz, nt),
        in_specs=in_specs,
        out_specs=out_specs,
        out_shape=out_shape,
        scratch_shapes=[pltpu.VMEM((tb + SUBLANES, CONV_CH), F32),
                        pltpu.VMEM((HEADS, HEAD_DIM, HEAD_DIM), F32)],
        compiler_params=pltpu.CompilerParams(dimension_semantics=("arbitrary", "arbitrary"),
                                             vmem_limit_bytes=VMEM_LIMIT),
        name="prompt_layer",
    )(x, mkb, mvb, *weights, *params)


def _sample_kernel(x_ref, ck_ref, cv_ref, s0_ref, cb_ref, wqkv_ref, wrest_ref, wbg_ref,
                   normg_ref, convw_ref, alog_ref, dtb_ref, anormg_ref, lng_ref, lnb_ref, ws_ref, bs_ref,
                   wbra_ref, wbrb_ref, wbrc_ref, bgate_ref, wout_ref, fng_ref,
                   y_ref, sd_ref, sc_ref, vn_ref,
                   hb_scr, pre_scr, rest_scr, bg_scr, ya_scr, yc_scr):
    sb = SAMPLE_BLOCK
    i = pl.program_id(0)
    nsteps = pl.num_programs(0)

    @pl.when(i == 0)
    def _():
        hb = (_rms(x_ref[...]) * normg_ref[...]).astype(BF16)
        hb_scr[...] = hb
        pre_scr[...] = _dot(hb, wqkv_ref[...])
        rest_scr[...] = _dot(hb, wrest_ref[:, :REST_MG])
        bg_scr[...] = _dot(hb, wbg_ref[...])
        bu = rest_scr[:, REST_B:REST_B + WIDTH]
        vn = _layernorm(rest_scr[:, REST_B + WIDTH:REST_B + 2 * WIDTH], lng_ref[...], lnb_ref[...])
        vn_ref[...] = vn
        bgate = rest_scr[:, REST_B + 2 * WIDTH:REST_B + 3 * WIDTH]
        gw = WIDTH // MLP_GROUPS
        s = jnp.concatenate(
            [vn[:, g * gw:(g + 1) * gw] * ws_ref[g, 0:1, 0:1] + bs_ref[g, 0:1, :]
             for g in range(MLP_GROUPS)], axis=1)
        rest_scr[:, REST_B:REST_B + WIDTH] = bu * s * jax.nn.silu(bgate)

    r0 = pl.multiple_of(i * sb, sb)
    rows = pl.ds(r0, sb)
    pre = pre_scr[rows, :]
    beta, g = _beta_and_log_decay(bg_scr[rows, :], alog_ref[...], dtb_ref[...])
    decay = jnp.exp(g)
    agate = rest_scr[rows, REST_AGATE:REST_AGATE + WIDTH]
    cq = rest_scr[rows, REST_C:REST_C + WIDTH]
    cgate = rest_scr[rows, REST_C + WIDTH:REST_C + 2 * WIDTH]

    own_head = ((_iota2((SUBLANES, MEM_LEN * HEADS), 1) % HEADS)
                == (_iota2((SUBLANES, MEM_LEN * HEADS), 0) % HEADS))

    scores = []
    for s in range(sb):
        qh = jnp.concatenate([cq[s:s + 1, hd * HEAD_DIM:(hd + 1) * HEAD_DIM] for hd in range(HEADS)]
                             + [jnp.zeros((SUBLANES - HEADS, HEAD_DIM), F32)], axis=0)
        scores.append(_dot_nt(qh, ck_ref[s]) * (HEAD_DIM ** -0.5))

    carried = [cb_ref[:, j, :] for j in range(CONV_WIDTH - 1)]
    conv = pre * convw_ref[CONV_WIDTH - 1:CONV_WIDTH, :]
    for j in range(CONV_WIDTH - 1):
        conv = conv + carried[j] * convw_ref[j:j + 1, :]
    for j in range(1, CONV_WIDTH - 1):
        sc_ref[:, j - 1, :] = carried[j]
    sc_ref[:, CONV_WIDTH - 2, :] = pre
    qkv = jax.nn.silu(conv)

    row8 = _iota2((sb, HEAD_DIM), 0)
    row16 = _iota2((2 * sb, HEAD_DIM), 0)
    qs, ks, vs, kq_s = [], [], [], []
    for hd in range(HEADS):
        lo = hd * HEAD_DIM
        qs.append(_l2norm(qkv[:, lo:lo + HEAD_DIM]) * (HEAD_DIM ** -0.5))
        ks.append(_l2norm(qkv[:, WIDTH + lo:WIDTH + lo + HEAD_DIM]))
        vs.append(qkv[:, 2 * WIDTH + lo:2 * WIDTH + lo + HEAD_DIM])
        kq = jnp.concatenate([ks[hd], qs[hd]], axis=0).astype(BF16)
        kq_s.append([_dot(kq, s0_ref[s, hd]) for s in range(sb)])
    o_heads = []
    for hd in range(HEADS):
        lo = hd * HEAD_DIM
        sk = jnp.zeros((sb, HEAD_DIM), F32)
        sq = jnp.zeros((sb, HEAD_DIM), F32)
        for s in range(sb):
            sk = jnp.where(row8 == s, kq_s[hd][s][:sb], sk)
            sq = jnp.where(row8 == s, kq_s[hd][s][sb:], sq)
        a = decay[:, HEADS + hd:HEADS + hd + 1]
        v_new = beta[:, hd:hd + 1] * (vs[hd] - a * sk)
        qk = jnp.sum(qs[hd] * ks[hd], axis=-1, keepdims=True)
        o = a * sq + qk * v_new
        o_heads.append(_rms(o) * anormg_ref[...] * jax.nn.silu(agate[:, lo:lo + HEAD_DIM]))
        kt = jnp.concatenate([ks[hd], jnp.zeros((HEAD_DIM - sb, HEAD_DIM), F32)], axis=0).T.astype(BF16)
        v_pad = jnp.concatenate([v_new, jnp.zeros((sb, HEAD_DIM), F32)], axis=0)
        zeros = jnp.zeros((HEAD_DIM - 2 * sb, HEAD_DIM), BF16)
        for s in range(sb):
            only_s = jnp.concatenate([jnp.where(row16 == s, v_pad, 0.0).astype(BF16), zeros], axis=0)
            sd_ref[s, hd] = a[s:s + 1, :] * s0_ref[s, hd] + _dot(kt, only_s)
    ya_scr[rows, :] = jnp.concatenate(o_heads, axis=1)

    probs = []
    for s in range(sb):
        sc = jnp.where(own_head, scores[s], -1e30)
        sc = sc - jnp.max(sc, axis=-1, keepdims=True)
        p = jnp.where(own_head, jnp.exp(sc), 0.0)
        probs.append(p / jnp.sum(p, axis=-1, keepdims=True))
    yc_rows = []
    for s in range(sb):
        oc = _dot(probs[s], cv_ref[s])
        oc = jnp.concatenate([oc[hd:hd + 1, :] for hd in range(HEADS)], axis=1)
        yc_rows.append(oc * jax.nn.silu(cgate[s:s + 1, :]))
    yc_scr[rows, :] = jnp.concatenate(yc_rows, axis=0)

    @pl.when(i == nsteps - 1)
    def _():
        y_ref[...] = _merge_and_project(x_ref[...], hb_scr[...], ya_scr[...],
                                        rest_scr[:, REST_B:REST_B + WIDTH], yc_scr[...], wrest_ref,
                                        bgate_ref, wbra_ref, wbrb_ref, wbrc_ref, wout_ref, fng_ref)


def _sample_layer(x, cache_k, cache_v, s0, cb, weights, params):
    n = x.shape[0]
    sb = SAMPLE_BLOCK
    full = lambda a: pl.BlockSpec(a.shape, lambda i: (0,) * a.ndim, pipeline_mode=pl.Buffered(1))
    in_specs = [
        full(x),
        pl.BlockSpec((sb, MEM_LEN * HEADS, HEAD_DIM), lambda i: (i, 0, 0)),
        pl.BlockSpec((sb, MEM_LEN * HEADS, HEAD_DIM), lambda i: (i, 0, 0)),
        pl.BlockSpec((sb, HEADS, HEAD_DIM, HEAD_DIM), lambda i: (i, 0, 0, 0)),
        pl.BlockSpec((sb, CONV_WIDTH - 1, CONV_CH), lambda i: (i, 0, 0)),
    ] + [full(w) for w in weights] + [full(p) for p in params]
    out_specs = [
        pl.BlockSpec((n, D_MODEL), lambda i: (0, 0)),
        pl.BlockSpec((sb, HEADS, HEAD_DIM, HEAD_DIM), lambda i: (i, 0, 0, 0)),
        pl.BlockSpec((sb, CONV_WIDTH - 1, CONV_CH), lambda i: (i, 0, 0)),
        pl.BlockSpec((n, WIDTH), lambda i: (0, 0)),
    ]
    out_shape = [
        jax.ShapeDtypeStruct((n, D_MODEL), F32),
        jax.ShapeDtypeStruct((n, HEADS, HEAD_DIM, HEAD_DIM), F32),
        jax.ShapeDtypeStruct((n, CONV_WIDTH - 1, CONV_CH), F32),
        jax.ShapeDtypeStruct((n, WIDTH), F32),
    ]
    return pl.pallas_call(
        _sample_kernel,
        grid=(n // sb,),
        in_specs=in_specs,
        out_specs=out_specs,
        out_shape=out_shape,
        scratch_shapes=[pltpu.VMEM((n, D_MODEL), BF16),
                        pltpu.VMEM((n, CONV_CH), F32),
                        pltpu.VMEM((n, REST_MG), F32),
                        pltpu.VMEM((n, LANES), F32),
                        pltpu.VMEM((n, WIDTH), F32),
                        pltpu.VMEM((n, WIDTH), F32)],
        compiler_params=pltpu.CompilerParams(dimension_semantics=("arbitrary",),
                                             vmem_limit_bytes=VMEM_LIMIT),
        name="sample_layer",
    )(x, cache_k, cache_v, s0, cb, *weights, *params)


def _split_input_projection(w):
    qkv_end = CONV_CH
    bg_end = qkv_end + 2 * HEADS
    wbg_pad = jnp.zeros((D_MODEL, LANES), BF16).at[:, :2 * HEADS].set(w[:, qkv_end:bg_end].astype(BF16))
    return w[:, :qkv_end].astype(BF16), w[:, bg_end:].astype(BF16), wbg_pad


def _lanes_4_to_7(vec):
    return jnp.zeros((1, LANES), F32).at[0, HEADS:2 * HEADS].set(vec)


def kernel(x_prompt, x_sample, cache_mem_k, cache_mem_v, state_delta, state_conv, mem_prompt, norm_g, w_in, conv_w, a_log, dt_bias, a_norm_g, ln_v_g, ln_v_b, w_spatial, b_spatial, mem_norm_g, w_mem_kv, w_br_a, w_br_b, w_br_c, b_gate, w_out, final_norm_g):
    depth = norm_g.shape[0]
    assert depth == 1, "single-layer step"
    bsz, seq, _ = x_prompt.shape
    nsmp = x_sample.shape[0]
    assert x_sample.shape[1] == 1 and seq % PROMPT_BLOCK == 0 and nsmp % SAMPLE_BLOCK == 0
    assert w_in.shape[2] == CONV_CH + 2 * HEADS + REST_COLS

    weights = _split_input_projection(w_in[0])
    params = (
        norm_g[0][None, :], conv_w[0], _lanes_4_to_7(a_log[0]), _lanes_4_to_7(dt_bias[0]),
        a_norm_g[0][None, :], ln_v_g[0][None, :], ln_v_b[0][None, :], w_spatial[0],
        jnp.broadcast_to(b_spatial[0][:, :, None], (MLP_GROUPS, MLP_CHUNK, WIDTH // MLP_GROUPS)),
        w_br_a[0].astype(BF16), w_br_b[0].astype(BF16), w_br_c[0].astype(BF16),
        b_gate[0].reshape(1, N_BRANCH * D_MODEL), w_out[0].astype(BF16), final_norm_g[None, :],
    )

    mk, mv, mkb, mvb = _memory_kv(mem_prompt.reshape(bsz * MEM_LEN, D_MODEL), mem_norm_g[0][None, :],
                                  w_mem_kv[0].astype(BF16))
    y_p, sd_p, sc_p = _prompt_layer(x_prompt, mkb.reshape(bsz, MEM_LEN, WIDTH),
                                    mvb.reshape(bsz, MEM_LEN, WIDTH), weights, params)
    y_s, sd_s, sc_s, vn_s = _sample_layer(
        x_sample.reshape(nsmp, D_MODEL), cache_mem_k.reshape(nsmp, MEM_LEN * HEADS, HEAD_DIM),
        cache_mem_v.reshape(nsmp, MEM_LEN * HEADS, HEAD_DIM), state_delta.reshape(state_delta.shape[1:]),
        state_conv.reshape(state_conv.shape[1:]), weights, params)

    kv_shape = (1, bsz, MEM_LEN, HEADS, HEAD_DIM)
    return (y_p, y_s.reshape(nsmp, 1, D_MODEL), sd_p[None], sc_p[None], mk.reshape(kv_shape),
            mv.reshape(kv_shape), sd_s[None], sc_s[None], vn_s.reshape(1, nsmp, 1, WIDTH))
```

```python
import jax
import jax.numpy as jnp
from jax import lax
from jax.experimental import pallas as pl
from jax.experimental.pallas import tpu as pltpu

F32 = jnp.float32
BF16 = jnp.bfloat16

D_MODEL = 1024
HEADS = 4
HEAD_DIM = 128
WIDTH = HEADS * HEAD_DIM
CONV_WIDTH = 4
CONV_CH = 3 * WIDTH
MLP_GROUPS = 4
MLP_CHUNK = 128
MEM_LEN = 256
N_BRANCH = 3
EPS = 1e-6

LANES = 128
SUBLANES = 8

REST_AGATE = 0
REST_B = REST_AGATE + WIDTH
REST_C = REST_B + 3 * WIDTH
REST_MG = REST_C + 2 * WIDTH
REST_COLS = REST_MG + N_BRANCH * D_MODEL
PACKED_COLS = CONV_CH + REST_COLS

PROMPT_BLOCK = 512
DELTA_CHUNK = 64
SAMPLE_BLOCK = 8
VMEM_LIMIT = 56 * 1024 * 1024


def _dot(a, b):
    return jnp.dot(a.astype(BF16), b.astype(BF16), preferred_element_type=F32)


def _dot_nt(a, b):
    return lax.dot_general(a.astype(BF16), b.astype(BF16), (((1,), (1,)), ((), ())),
                           preferred_element_type=F32)


def _dot_exact(a, b):
    return jnp.dot(a, b, preferred_element_type=F32, precision=lax.Precision.HIGHEST)


def _rms(x):
    return x * lax.rsqrt(jnp.mean(x * x, axis=-1, keepdims=True) + EPS)


def _l2norm(x):
    return x * lax.rsqrt(jnp.sum(x * x, axis=-1, keepdims=True) + EPS)


def _softplus(x):
    return jnp.maximum(x, 0.0) + jnp.log1p(jnp.exp(-jnp.abs(x)))


def _iota2(shape, dim):
    return lax.broadcasted_iota(jnp.int32, shape, dim)


def _memkv_kernel(mem_ref, g_ref, w_ref, k_ref, v_ref, kb_ref, vb_ref):
    xn = _rms(mem_ref[...]) * g_ref[...]
    kv = _dot(xn, w_ref[...])
    k = kv[:, :WIDTH]
    v = kv[:, WIDTH:]
    npos = k.shape[0]
    for hd in range(HEADS):
        k_ref[pl.ds(hd, npos, stride=HEADS), :] = k[:, hd * HEAD_DIM:(hd + 1) * HEAD_DIM]
        v_ref[pl.ds(hd, npos, stride=HEADS), :] = v[:, hd * HEAD_DIM:(hd + 1) * HEAD_DIM]
    kb_ref[...] = k.astype(BF16)
    vb_ref[...] = v.astype(BF16)


def _memory_kv(mem2d, mem_norm_g, w_mem_kv_bf):
    rows = mem2d.shape[0]
    blk = 512
    full = lambda shape: pl.BlockSpec(shape, lambda i: (0,) * len(shape))
    row_spec = lambda width: pl.BlockSpec((blk, width), lambda i: (i, 0))
    by_head = pl.BlockSpec((blk * HEADS, HEAD_DIM), lambda i: (i, 0))
    return pl.pallas_call(
        _memkv_kernel,
        grid=(rows // blk,),
        in_specs=[row_spec(D_MODEL), full((1, D_MODEL)), full((D_MODEL, 2 * WIDTH))],
        out_specs=[by_head] * 2 + [row_spec(WIDTH)] * 2,
        out_shape=[jax.ShapeDtypeStruct((rows * HEADS, HEAD_DIM), F32)] * 2
        + [jax.ShapeDtypeStruct((rows, WIDTH), BF16)] * 2,
        compiler_params=pltpu.CompilerParams(dimension_semantics=("arbitrary",),
                                             vmem_limit_bytes=VMEM_LIMIT),
        name="memory_kv",
    )(mem2d, mem_norm_g, w_mem_kv_bf)


def _beta_and_log_decay(bg, alog, dtb):
    beta = jax.nn.sigmoid(bg)
    g = -jnp.exp(alog) * _softplus(bg + dtb)
    return beta, g


def _spatial_weights(ws_ref):
    tril = _iota2((MLP_CHUNK, MLP_CHUNK), 0) >= _iota2((MLP_CHUNK, MLP_CHUNK), 1)
    return [jnp.where(tril, ws_ref[g], 0.0).astype(BF16) for g in range(MLP_GROUPS)]


def _layernorm(v, g, b):
    mu = jnp.mean(v, axis=-1, keepdims=True)
    vc = v - mu
    return vc * lax.rsqrt(jnp.mean(vc * vc, axis=-1, keepdims=True) + EPS) * g + b


def _rest(w_ref, lo, hi):
    return w_ref[:, CONV_CH + lo:CONV_CH + hi]


def _merge_and_project(x, hb, ya, yb, yc, w_ref, bgate_ref, wbra_ref, wbrb_ref, wbrc_ref, wout_ref,
                       fng_ref):
    merged = None
    for i, (yi, wbr) in enumerate(((ya, wbra_ref), (yb, wbrb_ref), (yc, wbrc_ref))):
        lo = i * D_MODEL
        gate = jax.nn.sigmoid(_dot(hb, _rest(w_ref, REST_MG + lo, REST_MG + lo + D_MODEL))
                              + bgate_ref[:, lo:lo + D_MODEL])
        term = gate * _dot(yi, wbr[...])
        merged = term if merged is None else merged + term
    out = x + _dot(merged, wout_ref[...])
    return _rms(out) * fng_ref[...]


def _col_from_lanes(row):
    sel = _iota2((SUBLANES, LANES), 0) == _iota2((SUBLANES, LANES), 1)
    return jnp.sum(jnp.where(sel, row, 0.0), axis=-1, keepdims=True)


def _delta_pair_terms(q, k, v, beta_col, gc_col, gc_row):
    n = q.shape[0]
    ck = n // 2
    ri = _iota2((n, n), 0)
    ci = _iota2((n, n), 1)
    same = (ri < ck) == (ci < ck)
    kt = k.T
    decay = jnp.exp(jnp.where(same & (ri >= ci), gc_col - gc_row, -1e30))
    kb = k * beta_col
    egc = jnp.exp(gc_col)
    low = jnp.where(same & (ri > ci), _dot(kb, kt) * decay, 0.0)
    last_a = gc_row[:, ck - 1:ck]
    last_b = gc_row[:, n - 1:n]
    in_a = _iota2((1, n), 1) < ck
    kdt = kt * jnp.exp(jnp.where(in_a, last_a, last_b) - gc_row)
    return dict(
        neg_low=-(low[:ck] + low[ck:]),
        attn=jnp.where(same & (ri >= ci), _dot(q, kt) * decay, 0.0),
        rhs=jnp.concatenate([v * beta_col, kb * egc], axis=1),
        qd=q * egc,
        kdt_a=kdt[:, :ck],
        kdt_b=jnp.where(in_a, 0.0, kdt),
        gl=(jnp.exp(last_a), jnp.exp(last_b)),
    )


def _block_diag2(packed):
    n = packed.shape[0]
    left = _iota2(packed.shape, 1) < n
    return jnp.concatenate([jnp.where(left, packed, 0.0), jnp.where(left, 0.0, packed)],
                           axis=0).astype(BF16)


def _prompt_kernel(x_ref, mk_ref, mv_ref, w_ref, wbg_ref, normg_ref, convw_ref,
                   alog_ref, dtb_ref, anormg_ref, lng_ref, lnb_ref, ws_ref, bs_ref, wbra_ref, wbrb_ref,
                   wbrc_ref, bgate_ref, wout_ref, fng_ref,
                   y_ref, sd_ref, sc_ref,
                   convbuf, state):
    tb = PROMPT_BLOCK
    ck = DELTA_CHUNK
    nck = tb // ck
    t = pl.program_id(1)

    @pl.when(t == 0)
    def _():
        convbuf[:, 0:SUBLANES, :] = jnp.zeros((CONV_CH // LANES, SUBLANES, LANES), F32)
        state[...] = jnp.zeros_like(state)

    hb = (_rms(x_ref[0]) * normg_ref[...]).astype(BF16)

    pre = _dot(hb, w_ref[:, :CONV_CH])
    bg_col = _dot(hb, wbg_ref[...])
    bg_row = jnp.concatenate([bg_col[r:r + LANES, :].T[:SUBLANES, :] for r in range(0, tb, LANES)],
                             axis=1)
    pb = _dot(hb, _rest(w_ref, REST_B, REST_B + 3 * WIDTH))
    pc = _dot(hb, _rest(w_ref, REST_C, REST_C + 2 * WIDTH))
    agate = _dot(hb, _rest(w_ref, REST_AGATE, REST_AGATE + WIDTH))

    conv_tiles, tails = [], []
    for ct in range(CONV_CH // LANES):
        cols = slice(ct * LANES, (ct + 1) * LANES)
        convbuf[ct, SUBLANES:SUBLANES + tb, :] = pre[:, cols]
        acc = pre[:, cols] * convw_ref[CONV_WIDTH - 1:CONV_WIDTH, cols]
        for j in range(CONV_WIDTH - 1):
            shift = CONV_WIDTH - 1 - j
            acc = acc + convbuf[ct, SUBLANES - shift:SUBLANES - shift + tb, :] * convw_ref[j:j + 1, cols]
        conv_tiles.append(acc)
        tails.append(convbuf[ct, tb:tb + SUBLANES, :])
        convbuf[ct, 0:SUBLANES, :] = tails[ct]
    sc_ref[0] = jnp.concatenate(tails, axis=1)[SUBLANES - (CONV_WIDTH - 1):, :]
    qkv = jax.nn.silu(jnp.concatenate(conv_tiles, axis=1))

    beta_col, g_col = _beta_and_log_decay(bg_col, alog_ref[...], dtb_ref[...])
    g_row = (-jnp.exp(_col_from_lanes(alog_ref[...]))
             * _softplus(bg_row + _col_from_lanes(dtb_ref[...])))
    n2 = 2 * ck
    same2 = (_iota2((n2, n2), 0) < ck) == (_iota2((n2, n2), 1) < ck)
    tril_f = (same2 & (_iota2((n2, n2), 0) >= _iota2((n2, n2), 1))).astype(F32)
    triu_f = (same2 & (_iota2((n2, n2), 0) <= _iota2((n2, n2), 1))).astype(F32)
    assert nck % 2 == 0
    npair = nck // 2
    gc_cols = [_dot_exact(tril_f, g_col[j * n2:(j + 1) * n2, :]) for j in range(npair)]
    gc_rows = [_dot_exact(g_row[:, j * n2:(j + 1) * n2], triu_f) for j in range(npair)]

    groups = [(j, hd) for j in range(npair) for hd in range(HEADS)]
    qn, kn, vs = [], [], []
    for hd in range(HEADS):
        lo = hd * HEAD_DIM
        qn.append(_l2norm(qkv[:, lo:lo + HEAD_DIM]) * (HEAD_DIM ** -0.5))
        kn.append(_l2norm(qkv[:, WIDTH + lo:WIDTH + lo + HEAD_DIM]))
        vs.append(qkv[:, 2 * WIDTH + lo:2 * WIDTH + lo + HEAD_DIM])
    terms = {}
    for j, hd in groups:
        rows = slice(j * n2, (j + 1) * n2)
        terms[j, hd] = _delta_pair_terms(
            qn[hd][rows], kn[hd][rows], vs[hd][rows], beta_col[rows, hd:hd + 1],
            gc_cols[j][:, HEADS + hd:HEADS + hd + 1], gc_rows[j][HEADS + hd:HEADS + hd + 1, :])

    eye2 = (_iota2((ck, n2), 0) == _iota2((ck, n2), 1) % ck).astype(F32)
    powers = [terms[g]["neg_low"] for g in groups]
    invs = [eye2 + p for p in powers]
    powers = [_dot(p, _block_diag2(p)) for p in powers]

    def inverse_level(last=False):
        nonlocal powers, invs
        if last:
            invs = [inv + _dot(inv, _block_diag2(p)) for inv, p in zip(invs, powers)]
            return
        prods = [_dot(jnp.concatenate([inv, p], axis=0), _block_diag2(p)) for inv, p in zip(invs, powers)]
        invs = [inv + r[:ck] for inv, r in zip(invs, prods)]
        powers = [r[ck:] for r in prods]

    levels = 0
    while 2 ** (levels + 1) < ck:
        levels += 1
    assert levels == 5, "the side work below is placed for five squaring levels"

    scores = [_dot_nt(pc[:, hd * HEAD_DIM:(hd + 1) * HEAD_DIM], mk_ref[0, :, hd * HEAD_DIM:(hd + 1) * HEAD_DIM])
              * (HEAD_DIM ** -0.5) for hd in range(HEADS)]
    inverse_level()
    bu = pb[:, :WIDTH]
    vn = _layernorm(pb[:, WIDTH:2 * WIDTH], lng_ref[...], lnb_ref[...])
    ws = _spatial_weights(ws_ref)
    gw = WIDTH // MLP_GROUPS
    s_rows = []
    for n in range(tb // MLP_CHUNK):
        r0 = n * MLP_CHUNK
        s_rows.append(jnp.concatenate(
            [_dot(ws[g], vn[r0:r0 + MLP_CHUNK, g * gw:(g + 1) * gw]) + bs_ref[g]
             for g in range(MLP_GROUPS)], axis=1))
    yb = bu * jnp.concatenate(s_rows, axis=0) * jax.nn.silu(pb[:, 2 * WIDTH:])
    inverse_level()
    inverse_level()
    oc = []
    for hd in range(HEADS):
        p = jnp.exp(scores[hd] - jnp.max(scores[hd], axis=-1, keepdims=True))
        oc.append(_dot(p, mv_ref[0, :, hd * HEAD_DIM:(hd + 1) * HEAD_DIM])
                  / jnp.sum(p, axis=-1, keepdims=True))
    yc = jnp.concatenate(oc, axis=1) * jax.nn.silu(pc[:, WIDTH:])
    inverse_level()
    inverse_level(last=True)
    uws = {g: _dot(_block_diag2(inv), terms[g]["rhs"]) for g, inv in zip(groups, invs)}

    n_slots = 2 * nck
    gate_cols = 2 * D_MODEL // nck
    n_gate = N_BRANCH * D_MODEL // gate_cols
    out_parts = (n_slots - n_gate) // 2
    assert n_gate + 2 * out_parts == n_slots and D_MODEL % gate_cols == 0
    out_cols = D_MODEL // out_parts
    gate_parts = []
    out_b, out_c = [], []

    def gate_piece(j):
        lo = j * gate_cols
        gate_parts.append(jax.nn.sigmoid(_dot(hb, _rest(w_ref, REST_MG + lo, REST_MG + lo + gate_cols))
                                         + bgate_ref[:, lo:lo + gate_cols]))

    side = [lambda j=j: gate_piece(j) for j in range(n_gate)]
    side += [lambda j=j: out_b.append(_dot(yb, wbrb_ref[:, j * out_cols:(j + 1) * out_cols]))
             for j in range(out_parts)]
    side += [lambda j=j: out_c.append(_dot(yc, wbrc_ref[:, j * out_cols:(j + 1) * out_cols]))
             for j in range(out_parts)]

    s_heads = [state[hd] for hd in range(HEADS)]
    o_chunks = [[] for _ in range(HEADS)]
    v_new_a = [None] * HEADS
    for c in range(nck):
        j, second = divmod(c, 2)
        rows = slice(ck, n2) if second else slice(0, ck)
        ws_qs = []
        for hd in range(HEADS):
            wq = jnp.concatenate([uws[j, hd][rows, HEAD_DIM:], terms[j, hd]["qd"][rows]], axis=0)
            ws_qs.append(_dot(wq, s_heads[hd]))
        side[2 * c]()
        for hd in range(HEADS):
            tm = terms[j, hd]
            v_new = uws[j, hd][rows, :HEAD_DIM] - ws_qs[hd][:ck]
            if second:
                v_pair = jnp.concatenate([v_new_a[hd], v_new], axis=0)
                o_chunks[hd].append(ws_qs[hd][ck:] + _dot(tm["attn"][rows], v_pair))
                s_heads[hd] = s_heads[hd] * tm["gl"][1] + _dot(tm["kdt_b"], v_pair)
            else:
                v_new_a[hd] = v_new
                o_chunks[hd].append(ws_qs[hd][ck:] + _dot(tm["attn"][rows, :ck], v_new))
                s_heads[hd] = s_heads[hd] * tm["gl"][0] + _dot(tm["kdt_a"], v_new)
        side[2 * c + 1]()
    for hd in range(HEADS):
        state[hd] = s_heads[hd]
    sd_ref[0] = state[...]

    o_heads = []
    for hd in range(HEADS):
        lo = hd * HEAD_DIM
        o = jnp.concatenate(o_chunks[hd], axis=0)
        o_heads.append(_rms(o) * anormg_ref[...] * jax.nn.silu(agate[:, lo:lo + HEAD_DIM]))
    branch_out = [_dot(jnp.concatenate(o_heads, axis=1), wbra_ref[...]),
                  jnp.concatenate(out_b, axis=1), jnp.concatenate(out_c, axis=1)]
    gates = jnp.concatenate(gate_parts, axis=1)
    merged = None
    for i in range(N_BRANCH):
        term = gates[:, i * D_MODEL:(i + 1) * D_MODEL] * branch_out[i]
        merged = term if merged is None else merged + term
    out = x_ref[0] + _dot(merged, wout_ref[...])
    y_ref[0] = _rms(out) * fng_ref[...]


def _prompt_layer(x, mkb, mvb, weights, params):
    bsz, seq, _ = x.shape
    tb = PROMPT_BLOCK
    nt = seq // tb
    full = lambda a: pl.BlockSpec(a.shape, lambda b, t: (0,) * a.ndim, pipeline_mode=pl.Buffered(1))
    in_specs = [
        pl.BlockSpec((1, tb, D_MODEL), lambda b, t: (b, t, 0)),
        pl.BlockSpec((1, MEM_LEN, WIDTH), lambda b, t: (b, 0, 0)),
        pl.BlockSpec((1, MEM_LEN, WIDTH), lambda b, t: (b, 0, 0)),
    ] + [full(w) for w in weights] + [full(p) for p in params]
    out_specs = [
        pl.BlockSpec((1, tb, D_MODEL), lambda b, t: (b, t, 0)),
        pl.BlockSpec((1, HEADS, HEAD_DIM, HEAD_DIM), lambda b, t: (b, 0, 0, 0)),
        pl.BlockSpec((1, CONV_WIDTH - 1, CONV_CH), lambda b, t: (b, 0, 0)),
    ]
    out_shape = [
        jax.ShapeDtypeStruct((bsz, seq, D_MODEL), F32),
        jax.ShapeDtypeStruct((bsz, HEADS, HEAD_DIM, HEAD_DIM), F32),
        jax.ShapeDtypeStruct((bsz, CONV_WIDTH - 1, CONV_CH), F32),
    ]
    return pl.pallas_call(
        _prompt_kernel,
        grid=(bsz, nt),
        in_specs=in_specs,
        out_specs=out_specs,
        out_shape=out_shape,
        scratch_shapes=[pltpu.VMEM((CONV_CH // LANES, tb + SUBLANES, LANES), F32),
                        pltpu.VMEM((HEADS, HEAD_DIM, HEAD_DIM), F32)],
        compiler_params=pltpu.CompilerParams(dimension_semantics=("arbitrary", "arbitrary"),
                                             vmem_limit_bytes=VMEM_LIMIT),
        name="prompt_layer",
    )(x, mkb, mvb, *weights, *params)


def _sample_kernel(x_ref, ck_ref, cv_ref, s0_ref, cb_ref, w_ref, wbg_ref,
                   normg_ref, convw_ref, alog_ref, dtb_ref, anormg_ref, lng_ref, lnb_ref, ws_ref, bs_ref,
                   wbra_ref, wbrb_ref, wbrc_ref, bgate_ref, wout_ref, fng_ref,
                   y_ref, sd_ref, sc_ref, vn_ref,
                   hb_scr, pre_scr, rest_scr, bg_scr, ya_scr, yc_scr):
    sb = SAMPLE_BLOCK
    i = pl.program_id(0)
    nsteps = pl.num_programs(0)

    @pl.when(i == 0)
    def _():
        hb = (_rms(x_ref[...]) * normg_ref[...]).astype(BF16)
        hb_scr[...] = hb
        pre_scr[...] = _dot(hb, w_ref[:, :CONV_CH])
        rest_scr[...] = _dot(hb, _rest(w_ref, 0, REST_MG))
        bg_scr[...] = _dot(hb, wbg_ref[...])
        bu = rest_scr[:, REST_B:REST_B + WIDTH]
        vn = _layernorm(rest_scr[:, REST_B + WIDTH:REST_B + 2 * WIDTH], lng_ref[...], lnb_ref[...])
        vn_ref[...] = vn
        bgate = rest_scr[:, REST_B + 2 * WIDTH:REST_B + 3 * WIDTH]
        gw = WIDTH // MLP_GROUPS
        s = jnp.concatenate(
            [vn[:, g * gw:(g + 1) * gw] * ws_ref[g, 0:1, 0:1] + bs_ref[g, 0:1, :]
             for g in range(MLP_GROUPS)], axis=1)
        rest_scr[:, REST_B:REST_B + WIDTH] = bu * s * jax.nn.silu(bgate)

    r0 = pl.multiple_of(i * sb, sb)
    rows = pl.ds(r0, sb)
    pre = pre_scr[rows, :]
    beta, g = _beta_and_log_decay(bg_scr[rows, :], alog_ref[...], dtb_ref[...])
    decay = jnp.exp(g)
    agate = rest_scr[rows, REST_AGATE:REST_AGATE + WIDTH]
    cq = rest_scr[rows, REST_C:REST_C + WIDTH]
    cgate = rest_scr[rows, REST_C + WIDTH:REST_C + 2 * WIDTH]

    own_head = ((_iota2((SUBLANES, MEM_LEN * HEADS), 1) % HEADS)
                == (_iota2((SUBLANES, MEM_LEN * HEADS), 0) % HEADS))

    scores = []
    for s in range(sb):
        qh = jnp.concatenate([cq[s:s + 1, hd * HEAD_DIM:(hd + 1) * HEAD_DIM] for hd in range(HEADS)]
                             + [jnp.zeros((SUBLANES - HEADS, HEAD_DIM), F32)], axis=0)
        scores.append(_dot_nt(qh, ck_ref[s]) * (HEAD_DIM ** -0.5))

    carried = [cb_ref[j] for j in range(CONV_WIDTH - 1)]
    conv = pre * convw_ref[CONV_WIDTH - 1:CONV_WIDTH, :]
    for j in range(CONV_WIDTH - 1):
        conv = conv + carried[j] * convw_ref[j:j + 1, :]
    for j in range(1, CONV_WIDTH - 1):
        sc_ref[j - 1] = carried[j]
    sc_ref[CONV_WIDTH - 2] = pre
    qkv = jax.nn.silu(conv)

    row8 = _iota2((sb, HEAD_DIM), 0)
    row16 = _iota2((2 * sb, HEAD_DIM), 0)
    qs, ks, vs, kq_s = [], [], [], []
    for hd in range(HEADS):
        lo = hd * HEAD_DIM
        qs.append(_l2norm(qkv[:, lo:lo + HEAD_DIM]) * (HEAD_DIM ** -0.5))
        ks.append(_l2norm(qkv[:, WIDTH + lo:WIDTH + lo + HEAD_DIM]))
        vs.append(qkv[:, 2 * WIDTH + lo:2 * WIDTH + lo + HEAD_DIM])
        kq = jnp.concatenate([ks[hd], qs[hd]], axis=0).astype(BF16)
        kq_s.append([_dot(kq, s0_ref[s, hd]) for s in range(sb)])
    o_heads = []
    for hd in range(HEADS):
        lo = hd * HEAD_DIM
        sk = jnp.zeros((sb, HEAD_DIM), F32)
        sq = jnp.zeros((sb, HEAD_DIM), F32)
        for s in range(sb):
            sk = jnp.where(row8 == s, kq_s[hd][s][:sb], sk)
            sq = jnp.where(row8 == s, kq_s[hd][s][sb:], sq)
        a = decay[:, HEADS + hd:HEADS + hd + 1]
        v_new = beta[:, hd:hd + 1] * (vs[hd] - a * sk)
        qk = jnp.sum(qs[hd] * ks[hd], axis=-1, keepdims=True)
        o = a * sq + qk * v_new
        o_heads.append(_rms(o) * anormg_ref[...] * jax.nn.silu(agate[:, lo:lo + HEAD_DIM]))
        kt = jnp.concatenate([ks[hd], jnp.zeros((HEAD_DIM - sb, HEAD_DIM), F32)], axis=0).T.astype(BF16)
        v_pad = jnp.concatenate([v_new, jnp.zeros((sb, HEAD_DIM), F32)], axis=0)
        zeros = jnp.zeros((HEAD_DIM - 2 * sb, HEAD_DIM), BF16)
        for s in range(sb):
            only_s = jnp.concatenate([jnp.where(row16 == s, v_pad, 0.0).astype(BF16), zeros], axis=0)
            sd_ref[s, hd] = a[s:s + 1, :] * s0_ref[s, hd] + _dot(kt, only_s)
    ya_scr[rows, :] = jnp.concatenate(o_heads, axis=1)

    probs, sums = [], []
    for s in range(sb):
        sc = jnp.where(own_head, scores[s], -1e30)
        sc = sc - jnp.max(sc, axis=-1, keepdims=True)
        probs.append(jnp.where(own_head, jnp.exp(sc), 0.0))
        sums.append(jnp.sum(probs[s], axis=-1, keepdims=True))
    yc_rows = []
    for s in range(sb):
        oc = _dot(probs[s], cv_ref[s]) / sums[s]
        oc = jnp.concatenate([oc[hd:hd + 1, :] for hd in range(HEADS)], axis=1)
        yc_rows.append(oc * jax.nn.silu(cgate[s:s + 1, :]))
    yc_scr[rows, :] = jnp.concatenate(yc_rows, axis=0)

    @pl.when(i == nsteps - 1)
    def _():
        y_ref[...] = _merge_and_project(x_ref[...], hb_scr[...], ya_scr[...],
                                        rest_scr[:, REST_B:REST_B + WIDTH], yc_scr[...], w_ref,
                                        bgate_ref, wbra_ref, wbrb_ref, wbrc_ref, wout_ref, fng_ref)


def _sample_layer(x, cache_k, cache_v, s0, cb, weights, params):
    n = x.shape[0]
    sb = SAMPLE_BLOCK
    full = lambda a: pl.BlockSpec(a.shape, lambda i: (0,) * a.ndim, pipeline_mode=pl.Buffered(1))
    in_specs = [
        full(x),
        pl.BlockSpec((sb, MEM_LEN * HEADS, HEAD_DIM), lambda i: (i, 0, 0)),
        pl.BlockSpec((sb, MEM_LEN * HEADS, HEAD_DIM), lambda i: (i, 0, 0)),
        pl.BlockSpec((sb, HEADS, HEAD_DIM, HEAD_DIM), lambda i: (i, 0, 0, 0)),
        pl.BlockSpec((CONV_WIDTH - 1, sb, CONV_CH), lambda i: (0, i, 0)),
    ] + [full(w) for w in weights] + [full(p) for p in params]
    out_specs = [
        pl.BlockSpec((n, D_MODEL), lambda i: (0, 0)),
        pl.BlockSpec((sb, HEADS, HEAD_DIM, HEAD_DIM), lambda i: (i, 0, 0, 0)),
        pl.BlockSpec((CONV_WIDTH - 1, sb, CONV_CH), lambda i: (0, i, 0)),
        pl.BlockSpec((n, WIDTH), lambda i: (0, 0)),
    ]
    out_shape = [
        jax.ShapeDtypeStruct((n, D_MODEL), F32),
        jax.ShapeDtypeStruct((n, HEADS, HEAD_DIM, HEAD_DIM), F32),
        jax.ShapeDtypeStruct((CONV_WIDTH - 1, n, CONV_CH), F32),
        jax.ShapeDtypeStruct((n, WIDTH), F32),
    ]
    return pl.pallas_call(
        _sample_kernel,
        grid=(n // sb,),
        in_specs=in_specs,
        out_specs=out_specs,
        out_shape=out_shape,
        scratch_shapes=[pltpu.VMEM((n, D_MODEL), BF16),
                        pltpu.VMEM((n, CONV_CH), F32),
                        pltpu.VMEM((n, REST_MG), F32),
                        pltpu.VMEM((n, LANES), F32),
                        pltpu.VMEM((n, WIDTH), F32),
                        pltpu.VMEM((n, WIDTH), F32)],
        compiler_params=pltpu.CompilerParams(dimension_semantics=("arbitrary",),
                                             vmem_limit_bytes=VMEM_LIMIT),
        name="sample_layer",
    )(x, cache_k, cache_v, s0, cb, *weights, *params)


PACK_BLOCK = 512


def _pack_kernel(a_ref, b_ref, o_ref, bg_ref):
    j = pl.program_id(0)
    first_rest = CONV_CH // PACK_BLOCK
    a = a_ref[...]
    skipped = jnp.concatenate([a[2 * HEADS:], b_ref[...]], axis=0)
    rows = jnp.where(j < first_rest, a, skipped)
    o_ref[...] = rows.T.astype(BF16)

    @pl.when(j == first_rest)
    def _():
        logits = jnp.concatenate([a[:2 * HEADS], jnp.zeros((LANES - 2 * HEADS, D_MODEL), F32)], axis=0)
        bg_ref[...] = logits.T.astype(BF16)


def _pack_input_projection(w_t):
    assert 2 * HEADS == SUBLANES and CONV_CH % PACK_BLOCK == 0 and PACKED_COLS % PACK_BLOCK == 0
    return pl.pallas_call(
        _pack_kernel,
        grid=(PACKED_COLS // PACK_BLOCK,),
        in_specs=[pl.BlockSpec((PACK_BLOCK, D_MODEL), lambda j: (j, 0)),
                  pl.BlockSpec((SUBLANES, D_MODEL), lambda j: ((j + 1) * (PACK_BLOCK // SUBLANES), 0))],
        out_specs=[pl.BlockSpec((D_MODEL, PACK_BLOCK), lambda j: (0, j)),
                   pl.BlockSpec((D_MODEL, LANES), lambda j: (0, 0))],
        out_shape=[jax.ShapeDtypeStruct((D_MODEL, PACKED_COLS), BF16),
                   jax.ShapeDtypeStruct((D_MODEL, LANES), BF16)],
        compiler_params=pltpu.CompilerParams(dimension_semantics=("arbitrary",),
                                             vmem_limit_bytes=VMEM_LIMIT),
        name="pack_input_projection",
    )(w_t, w_t)


def _lanes_4_to_7(vec):
    return jnp.zeros((1, LANES), F32).at[0, HEADS:2 * HEADS].set(vec)


def kernel(x_prompt, x_sample, cache_mem_k, cache_mem_v, state_delta, state_conv, mem_prompt, norm_g, w_in, conv_w, a_log, dt_bias, a_norm_g, ln_v_g, ln_v_b, w_spatial, b_spatial, mem_norm_g, w_mem_kv, w_br_a, w_br_b, w_br_c, b_gate, w_out, final_norm_g):
    depth = norm_g.shape[0]
    assert depth == 1, "single-layer step"
    bsz, seq, _ = x_prompt.shape
    nsmp = x_sample.shape[0]
    assert x_sample.shape[1] == 1 and seq % PROMPT_BLOCK == 0 and nsmp % SAMPLE_BLOCK == 0
    assert w_in.shape[2] == CONV_CH + 2 * HEADS + REST_COLS

    weights = _pack_input_projection(w_in[0].T)
    params = (
        norm_g[0][None, :], conv_w[0], _lanes_4_to_7(a_log[0]), _lanes_4_to_7(dt_bias[0]),
        a_norm_g[0][None, :], ln_v_g[0][None, :], ln_v_b[0][None, :], w_spatial[0],
        jnp.broadcast_to(b_spatial[0][:, :, None], (MLP_GROUPS, MLP_CHUNK, WIDTH // MLP_GROUPS)),
        w_br_a[0].astype(BF16), w_br_b[0].astype(BF16), w_br_c[0].astype(BF16),
        b_gate[0].reshape(1, N_BRANCH * D_MODEL), w_out[0].astype(BF16), final_norm_g[None, :],
    )

    mk, mv, mkb, mvb = _memory_kv(mem_prompt.reshape(bsz * MEM_LEN, D_MODEL), mem_norm_g[0][None, :],
                                  w_mem_kv[0].astype(BF16))
    y_p, sd_p, sc_p = _prompt_layer(x_prompt, mkb.reshape(bsz, MEM_LEN, WIDTH),
                                    mvb.reshape(bsz, MEM_LEN, WIDTH), weights, params)
    y_s, sd_s, sc_s, vn_s = _sample_layer(
        x_sample.reshape(nsmp, D_MODEL), cache_mem_k.reshape(nsmp, MEM_LEN * HEADS, HEAD_DIM),
        cache_mem_v.reshape(nsmp, MEM_LEN * HEADS, HEAD_DIM), state_delta.reshape(state_delta.shape[1:]),
        jnp.transpose(state_conv[0], (1, 0, 2)), weights, params)

    kv_shape = (1, bsz, MEM_LEN, HEADS, HEAD_DIM)
    return (y_p, y_s.reshape(nsmp, 1, D_MODEL), sd_p[None], sc_p[None], mk.reshape(kv_shape),
            mv.reshape(kv_shape), sd_s[None], jnp.transpose(sc_s, (1, 0, 2))[None],
            vn_s.reshape(1, nsmp, 1, WIDTH))
```

```python
import functools

import jax
import jax.numpy as jnp
from jax import lax
from jax.experimental import pallas as pl
from jax.experimental.pallas import tpu as pltpu

F32 = jnp.float32
BF16 = jnp.bfloat16

D_MODEL = 1024
HEADS = 4
HEAD_DIM = 128
WIDTH = HEADS * HEAD_DIM
CONV_WIDTH = 4
CONV_CH = 3 * WIDTH
MLP_GROUPS = 4
MLP_CHUNK = 128
MEM_LEN = 256
N_BRANCH = 3
EPS = 1e-6

LANES = 128
SUBLANES = 8

REST_AGATE = 0
REST_B = REST_AGATE + WIDTH
REST_C = REST_B + 3 * WIDTH
REST_MG = REST_C + 2 * WIDTH
REST_COLS = REST_MG + N_BRANCH * D_MODEL
PACKED_COLS = CONV_CH + REST_COLS

PROMPT_BLOCK = 512
DELTA_CHUNK = 64
SAMPLE_BLOCK = 8
VMEM_LIMIT = 60 * 1024 * 1024


def _dot(a, b):
    return jnp.dot(a.astype(BF16), b.astype(BF16), preferred_element_type=F32)


def _dot_nt(a, b):
    return lax.dot_general(a.astype(BF16), b.astype(BF16), (((1,), (1,)), ((), ())),
                           preferred_element_type=F32)


def _dot_exact(a, b):
    return jnp.dot(a, b, preferred_element_type=F32, precision=lax.Precision.HIGHEST)


def _rms(x):
    return x * lax.rsqrt(jnp.mean(x * x, axis=-1, keepdims=True) + EPS)


def _l2norm(x):
    return x * lax.rsqrt(jnp.sum(x * x, axis=-1, keepdims=True) + EPS)


def _softplus(x):
    return jnp.maximum(x, 0.0) + jnp.log1p(jnp.exp(-jnp.abs(x)))


def _iota2(shape, dim):
    return lax.broadcasted_iota(jnp.int32, shape, dim)


def _memkv_kernel(mem_ref, g_ref, w_ref, k_ref, v_ref, kb_ref, vb_ref):
    xn = _rms(mem_ref[...]) * g_ref[...]
    kv = _dot(xn, w_ref[...])
    k = kv[:, :WIDTH]
    v = kv[:, WIDTH:]
    npos = k.shape[0]
    for hd in range(HEADS):
        k_ref[pl.ds(hd, npos, stride=HEADS), :] = k[:, hd * HEAD_DIM:(hd + 1) * HEAD_DIM]
        v_ref[pl.ds(hd, npos, stride=HEADS), :] = v[:, hd * HEAD_DIM:(hd + 1) * HEAD_DIM]
    kb_ref[...] = k.astype(BF16)
    vb_ref[...] = v.astype(BF16)


def _memory_kv(mem2d, mem_norm_g, w_mem_kv_bf):
    rows = mem2d.shape[0]
    blk = 512
    full = lambda shape: pl.BlockSpec(shape, lambda i: (0,) * len(shape))
    row_spec = lambda width: pl.BlockSpec((blk, width), lambda i: (i, 0))
    by_head = pl.BlockSpec((blk * HEADS, HEAD_DIM), lambda i: (i, 0))
    return pl.pallas_call(
        _memkv_kernel,
        grid=(rows // blk,),
        in_specs=[row_spec(D_MODEL), full((1, D_MODEL)), full((D_MODEL, 2 * WIDTH))],
        out_specs=[by_head] * 2 + [row_spec(WIDTH)] * 2,
        out_shape=[jax.ShapeDtypeStruct((rows * HEADS, HEAD_DIM), F32)] * 2
        + [jax.ShapeDtypeStruct((rows, WIDTH), BF16)] * 2,
        compiler_params=pltpu.CompilerParams(dimension_semantics=("arbitrary",),
                                             vmem_limit_bytes=VMEM_LIMIT),
        name="memory_kv",
    )(mem2d, mem_norm_g, w_mem_kv_bf)


def _beta_and_log_decay(bg, alog, dtb):
    beta = jax.nn.sigmoid(bg)
    g = -jnp.exp(alog) * _softplus(bg + dtb)
    return beta, g


def _spatial_weights(ws_ref):
    tril = _iota2((MLP_CHUNK, MLP_CHUNK), 0) >= _iota2((MLP_CHUNK, MLP_CHUNK), 1)
    return [jnp.where(tril, ws_ref[g], 0.0).astype(BF16) for g in range(MLP_GROUPS)]


def _layernorm(v, g, b):
    mu = jnp.mean(v, axis=-1, keepdims=True)
    vc = v - mu
    return vc * lax.rsqrt(jnp.mean(vc * vc, axis=-1, keepdims=True) + EPS) * g + b


def _rest(w_ref, lo, hi):
    return w_ref[:, CONV_CH + lo:CONV_CH + hi]


def _merge_and_project(x, hb, ya, yb, yc, w_ref, bgate_ref, wbra_ref, wbrb_ref, wbrc_ref, wout_ref,
                       fng_ref):
    merged = None
    for i, (yi, wbr) in enumerate(((ya, wbra_ref), (yb, wbrb_ref), (yc, wbrc_ref))):
        lo = i * D_MODEL
        gate = jax.nn.sigmoid(_dot(hb, _rest(w_ref, REST_MG + lo, REST_MG + lo + D_MODEL))
                              + bgate_ref[:, lo:lo + D_MODEL])
        term = gate * _dot(yi, wbr[...])
        merged = term if merged is None else merged + term
    out = x + _dot(merged, wout_ref[...])
    return _rms(out) * fng_ref[...]


def _col_from_lanes(row):
    sel = _iota2((SUBLANES, LANES), 0) == _iota2((SUBLANES, LANES), 1)
    return jnp.sum(jnp.where(sel, row, 0.0), axis=-1, keepdims=True)


def _delta_pair_terms(q, k, v, beta_col, gc_col, gc_row):
    n = q.shape[0]
    ck = n // 2
    ri = _iota2((n, n), 0)
    ci = _iota2((n, n), 1)
    same = (ri < ck) == (ci < ck)
    kt = k.T
    decay = jnp.exp(jnp.where(same & (ri >= ci), gc_col - gc_row, -1e30))
    kb = k * beta_col
    egc = jnp.exp(gc_col)
    low = jnp.where(same & (ri > ci), _dot(kb, kt) * decay, 0.0)
    last_a = gc_row[:, ck - 1:ck]
    last_b = gc_row[:, n - 1:n]
    in_a = _iota2((1, n), 1) < ck
    kdt = kt * jnp.exp(jnp.where(in_a, last_a, last_b) - gc_row)
    return dict(
        neg_low=-(low[:ck] + low[ck:]),
        attn=jnp.where(same & (ri >= ci), _dot(q, kt) * decay, 0.0),
        rhs=jnp.concatenate([v * beta_col, kb * egc], axis=1),
        qd=q * egc,
        kdt_a=kdt[:, :ck],
        kdt_b=jnp.where(in_a, 0.0, kdt),
        gl=(jnp.exp(last_a), jnp.exp(last_b)),
    )


def _block_diag2(packed):
    n = packed.shape[0]
    left = _iota2(packed.shape, 1) < n
    return jnp.concatenate([jnp.where(left, packed, 0.0), jnp.where(left, 0.0, packed)],
                           axis=0).astype(BF16)


def _prompt_kernel(x_ref, xnext_ref, mk_ref, mv_ref, w_ref, wbg_ref, normg_ref, convw_ref,
                   alog_ref, dtb_ref, anormg_ref, lng_ref, lnb_ref, ws_ref, bs_ref, wbra_ref, wbrb_ref,
                   wbrc_ref, bgate_ref, wout_ref, fng_ref,
                   y_ref, sd_ref, sc_ref,
                   convbuf, state, hb_scr, *, blocks_per_seq):
    tb = PROMPT_BLOCK
    ck = DELTA_CHUNK
    nck = tb // ck
    step = pl.program_id(0)
    t = step % blocks_per_seq
    conv_tiles_n = CONV_CH // LANES

    def normalised(xr):
        return (_rms(xr[0]) * normg_ref[...]).astype(BF16)

    def project_qkv(hb_any, part, nparts):
        width = CONV_CH // nparts
        res = _dot(hb_any, w_ref[:, part * width:(part + 1) * width])
        for k in range(width // LANES):
            convbuf[part * (width // LANES) + k, SUBLANES:SUBLANES + tb, :] = res[:, k * LANES:(k + 1) * LANES]

    @pl.when(step == 0)
    def _():
        hb_scr[...] = normalised(x_ref)
        project_qkv(hb_scr[...], 0, 1)

    @pl.when(t == 0)
    def _():
        convbuf[:, 0:SUBLANES, :] = jnp.zeros((conv_tiles_n, SUBLANES, LANES), F32)
        state[...] = jnp.zeros_like(state)

    hb = hb_scr[...]

    bg_col = _dot(hb, wbg_ref[...])
    bg_row = jnp.concatenate([bg_col[r:r + LANES, :].T[:SUBLANES, :] for r in range(0, tb, LANES)],
                             axis=1)
    pb = _dot(hb, _rest(w_ref, REST_B, REST_B + 3 * WIDTH))
    pc = _dot(hb, _rest(w_ref, REST_C, REST_C + 2 * WIDTH))
    agate = _dot(hb, _rest(w_ref, REST_AGATE, REST_AGATE + WIDTH))

    conv_tiles, tails = [], []
    for ct in range(conv_tiles_n):
        cols = slice(ct * LANES, (ct + 1) * LANES)
        acc = convbuf[ct, SUBLANES:SUBLANES + tb, :] * convw_ref[CONV_WIDTH - 1:CONV_WIDTH, cols]
        for j in range(CONV_WIDTH - 1):
            shift = CONV_WIDTH - 1 - j
            acc = acc + convbuf[ct, SUBLANES - shift:SUBLANES - shift + tb, :] * convw_ref[j:j + 1, cols]
        conv_tiles.append(acc)
        tails.append(convbuf[ct, tb:tb + SUBLANES, :])
        convbuf[ct, 0:SUBLANES, :] = tails[ct]
    sc_ref[0] = jnp.concatenate(tails, axis=1)[SUBLANES - (CONV_WIDTH - 1):, :]
    qkv = jax.nn.silu(jnp.concatenate(conv_tiles, axis=1))

    beta_col, g_col = _beta_and_log_decay(bg_col, alog_ref[...], dtb_ref[...])
    g_row = (-jnp.exp(_col_from_lanes(alog_ref[...]))
             * _softplus(bg_row + _col_from_lanes(dtb_ref[...])))
    n2 = 2 * ck
    same2 = (_iota2((n2, n2), 0) < ck) == (_iota2((n2, n2), 1) < ck)
    tril_f = (same2 & (_iota2((n2, n2), 0) >= _iota2((n2, n2), 1))).astype(F32)
    triu_f = (same2 & (_iota2((n2, n2), 0) <= _iota2((n2, n2), 1))).astype(F32)
    assert nck % 2 == 0
    npair = nck // 2
    gc_cols = [_dot_exact(tril_f, g_col[j * n2:(j + 1) * n2, :]) for j in range(npair)]
    gc_rows = [_dot_exact(g_row[:, j * n2:(j + 1) * n2], triu_f) for j in range(npair)]

    groups = [(j, hd) for j in range(npair) for hd in range(HEADS)]
    qn, kn, vs = [], [], []
    for hd in range(HEADS):
        lo = hd * HEAD_DIM
        qn.append(_l2norm(qkv[:, lo:lo + HEAD_DIM]) * (HEAD_DIM ** -0.5))
        kn.append(_l2norm(qkv[:, WIDTH + lo:WIDTH + lo + HEAD_DIM]))
        vs.append(qkv[:, 2 * WIDTH + lo:2 * WIDTH + lo + HEAD_DIM])
    terms = {}
    for j, hd in groups:
        rows = slice(j * n2, (j + 1) * n2)
        terms[j, hd] = _delta_pair_terms(
            qn[hd][rows], kn[hd][rows], vs[hd][rows], beta_col[rows, hd:hd + 1],
            gc_cols[j][:, HEADS + hd:HEADS + hd + 1], gc_rows[j][HEADS + hd:HEADS + hd + 1, :])

    eye2 = (_iota2((ck, n2), 0) == _iota2((ck, n2), 1) % ck).astype(F32)
    powers = [terms[g]["neg_low"] for g in groups]
    invs = [eye2 + p for p in powers]
    powers = [_dot(p, _block_diag2(p)) for p in powers]

    def inverse_level(last=False):
        nonlocal powers, invs
        if last:
            invs = [inv + _dot(inv, _block_diag2(p)) for inv, p in zip(invs, powers)]
            return
        prods = [_dot(jnp.concatenate([inv, p], axis=0), _block_diag2(p)) for inv, p in zip(invs, powers)]
        invs = [inv + r[:ck] for inv, r in zip(invs, prods)]
        powers = [r[ck:] for r in prods]

    levels = 0
    while 2 ** (levels + 1) < ck:
        levels += 1
    assert levels == 5, "the side work below is placed for five squaring levels"

    scores = [_dot_nt(pc[:, hd * HEAD_DIM:(hd + 1) * HEAD_DIM], mk_ref[0, :, hd * HEAD_DIM:(hd + 1) * HEAD_DIM])
              * (HEAD_DIM ** -0.5) for hd in range(HEADS)]
    inverse_level()
    bu = pb[:, :WIDTH]
    vn = _layernorm(pb[:, WIDTH:2 * WIDTH], lng_ref[...], lnb_ref[...])
    ws = _spatial_weights(ws_ref)
    gw = WIDTH // MLP_GROUPS
    s_rows = []
    for n in range(tb // MLP_CHUNK):
        r0 = n * MLP_CHUNK
        s_rows.append(jnp.concatenate(
            [_dot(ws[g], vn[r0:r0 + MLP_CHUNK, g * gw:(g + 1) * gw]) + bs_ref[g]
             for g in range(MLP_GROUPS)], axis=1))
    yb = bu * jnp.concatenate(s_rows, axis=0) * jax.nn.silu(pb[:, 2 * WIDTH:])
    inverse_level()
    inverse_level()
    oc = []
    for hd in range(HEADS):
        p = jnp.exp(scores[hd] - jnp.max(scores[hd], axis=-1, keepdims=True))
        oc.append(_dot(p, mv_ref[0, :, hd * HEAD_DIM:(hd + 1) * HEAD_DIM])
                  / jnp.sum(p, axis=-1, keepdims=True))
    yc = jnp.concatenate(oc, axis=1) * jax.nn.silu(pc[:, WIDTH:])
    inverse_level()
    inverse_level(last=True)
    uws = {g: _dot(_block_diag2(inv), terms[g]["rhs"]) for g, inv in zip(groups, invs)}

    n_slots = 2 * nck
    gate_cols = 2 * D_MODEL // nck
    n_gate = N_BRANCH * D_MODEL // gate_cols
    out_parts = (n_slots - n_gate) // 2
    assert n_gate + 2 * out_parts == n_slots and D_MODEL % gate_cols == 0
    out_cols = D_MODEL // out_parts
    gate_parts = []
    out_b, out_c = [], []

    def gate_piece(j):
        lo = j * gate_cols
        gate_parts.append(jax.nn.sigmoid(_dot(hb, _rest(w_ref, REST_MG + lo, REST_MG + lo + gate_cols))
                                         + bgate_ref[:, lo:lo + gate_cols]))

    side = [lambda j=j: gate_piece(j) for j in range(n_gate)]
    side += [lambda j=j: out_b.append(_dot(yb, wbrb_ref[:, j * out_cols:(j + 1) * out_cols]))
             for j in range(out_parts)]
    side += [lambda j=j: out_c.append(_dot(yc, wbrc_ref[:, j * out_cols:(j + 1) * out_cols]))
             for j in range(out_parts)]

    s_heads = [state[hd] for hd in range(HEADS)]
    o_chunks = [[] for _ in range(HEADS)]
    v_new_a = [None] * HEADS
    for c in range(nck):
        j, second = divmod(c, 2)
        rows = slice(ck, n2) if second else slice(0, ck)
        ws_qs = []
        for hd in range(HEADS):
            wq = jnp.concatenate([uws[j, hd][rows, HEAD_DIM:], terms[j, hd]["qd"][rows]], axis=0)
            ws_qs.append(_dot(wq, s_heads[hd]))
        side[2 * c]()
        for hd in range(HEADS):
            tm = terms[j, hd]
            v_new = uws[j, hd][rows, :HEAD_DIM] - ws_qs[hd][:ck]
            if second:
                v_pair = jnp.concatenate([v_new_a[hd], v_new], axis=0)
                o_chunks[hd].append(ws_qs[hd][ck:] + _dot(tm["attn"][rows], v_pair))
                s_heads[hd] = s_heads[hd] * tm["gl"][1] + _dot(tm["kdt_b"], v_pair)
            else:
                v_new_a[hd] = v_new
                o_chunks[hd].append(ws_qs[hd][ck:] + _dot(tm["attn"][rows, :ck], v_new))
                s_heads[hd] = s_heads[hd] * tm["gl"][0] + _dot(tm["kdt_a"], v_new)
        side[2 * c + 1]()
    for hd in range(HEADS):
        state[hd] = s_heads[hd]
    sd_ref[0] = state[...]

    hb_next = normalised(xnext_ref)
    hb_scr[...] = hb_next
    project_qkv(hb_next, 0, 3)
    o_heads = []
    for hd in range(HEADS):
        lo = hd * HEAD_DIM
        o = jnp.concatenate(o_chunks[hd], axis=0)
        o_heads.append(_rms(o) * anormg_ref[...] * jax.nn.silu(agate[:, lo:lo + HEAD_DIM]))
    branch_out = [_dot(jnp.concatenate(o_heads, axis=1), wbra_ref[...]),
                  jnp.concatenate(out_b, axis=1), jnp.concatenate(out_c, axis=1)]
    project_qkv(hb_next, 1, 3)
    gates = jnp.concatenate(gate_parts, axis=1)
    merged = None
    for i in range(N_BRANCH):
        term = gates[:, i * D_MODEL:(i + 1) * D_MODEL] * branch_out[i]
        merged = term if merged is None else merged + term
    out = x_ref[0] + _dot(merged, wout_ref[...])
    project_qkv(hb_next, 2, 3)
    y_ref[0] = _rms(out) * fng_ref[...]


def _prompt_layer(x, mkb, mvb, weights, params):
    bsz, seq, _ = x.shape
    tb = PROMPT_BLOCK
    nt = seq // tb
    nsteps = bsz * nt
    full = lambda a: pl.BlockSpec(a.shape, lambda i: (0,) * a.ndim, pipeline_mode=pl.Buffered(1))

    def next_block(i):
        nxt = jnp.minimum(i + 1, nsteps - 1)
        return (nxt // nt, nxt % nt, 0)

    in_specs = [
        pl.BlockSpec((1, tb, D_MODEL), lambda i: (i // nt, i % nt, 0)),
        pl.BlockSpec((1, tb, D_MODEL), next_block),
        pl.BlockSpec((1, MEM_LEN, WIDTH), lambda i: (i // nt, 0, 0)),
        pl.BlockSpec((1, MEM_LEN, WIDTH), lambda i: (i // nt, 0, 0)),
    ] + [full(w) for w in weights] + [full(p) for p in params]
    out_specs = [
        pl.BlockSpec((1, tb, D_MODEL), lambda i: (i // nt, i % nt, 0)),
        pl.BlockSpec((1, HEADS, HEAD_DIM, HEAD_DIM), lambda i: (i // nt, 0, 0, 0)),
        pl.BlockSpec((1, CONV_WIDTH - 1, CONV_CH), lambda i: (i // nt, 0, 0)),
    ]
    out_shape = [
        jax.ShapeDtypeStruct((bsz, seq, D_MODEL), F32),
        jax.ShapeDtypeStruct((bsz, HEADS, HEAD_DIM, HEAD_DIM), F32),
        jax.ShapeDtypeStruct((bsz, CONV_WIDTH - 1, CONV_CH), F32),
    ]
    return pl.pallas_call(
        functools.partial(_prompt_kernel, blocks_per_seq=nt),
        grid=(nsteps,),
        in_specs=in_specs,
        out_specs=out_specs,
        out_shape=out_shape,
        scratch_shapes=[pltpu.VMEM((CONV_CH // LANES, tb + SUBLANES, LANES), F32),
                        pltpu.VMEM((HEADS, HEAD_DIM, HEAD_DIM), F32),
                        pltpu.VMEM((tb, D_MODEL), BF16)],
        compiler_params=pltpu.CompilerParams(dimension_semantics=("arbitrary",),
                                             vmem_limit_bytes=VMEM_LIMIT),
        name="prompt_layer",
    )(x, x, mkb, mvb, *weights, *params)


def _sample_kernel(x_ref, ck_ref, cv_ref, s0_ref, cb_ref, w_ref, wbg_ref,
                   normg_ref, convw_ref, alog_ref, dtb_ref, anormg_ref, lng_ref, lnb_ref, ws_ref, bs_ref,
                   wbra_ref, wbrb_ref, wbrc_ref, bgate_ref, wout_ref, fng_ref,
                   y_ref, sd_ref, sc_ref, vn_ref,
                   hb_scr, pre_scr, rest_scr, bg_scr, ya_scr, yc_scr):
    sb = SAMPLE_BLOCK
    i = pl.program_id(0)
    nsteps = pl.num_programs(0)

    @pl.when(i == 0)
    def _():
        hb = (_rms(x_ref[...]) * normg_ref[...]).astype(BF16)
        hb_scr[...] = hb
        pre_scr[...] = _dot(hb, w_ref[:, :CONV_CH])
        rest_scr[...] = _dot(hb, _rest(w_ref, 0, REST_MG))
        bg_scr[...] = _dot(hb, wbg_ref[...])
        bu = rest_scr[:, REST_B:REST_B + WIDTH]
        vn = _layernorm(rest_scr[:, REST_B + WIDTH:REST_B + 2 * WIDTH], lng_ref[...], lnb_ref[...])
        vn_ref[...] = vn
        bgate = rest_scr[:, REST_B + 2 * WIDTH:REST_B + 3 * WIDTH]
        gw = WIDTH // MLP_GROUPS
        s = jnp.concatenate(
            [vn[:, g * gw:(g + 1) * gw] * ws_ref[g, 0:1, 0:1] + bs_ref[g, 0:1, :]
             for g in range(MLP_GROUPS)], axis=1)
        rest_scr[:, REST_B:REST_B + WIDTH] = bu * s * jax.nn.silu(bgate)

    r0 = pl.multiple_of(i * sb, sb)
    rows = pl.ds(r0, sb)
    pre = pre_scr[rows, :]
    beta, g = _beta_and_log_decay(bg_scr[rows, :], alog_ref[...], dtb_ref[...])
    decay = jnp.exp(g)
    agate = rest_scr[rows, REST_AGATE:REST_AGATE + WIDTH]
    cq = rest_scr[rows, REST_C:REST_C + WIDTH]
    cgate = rest_scr[rows, REST_C + WIDTH:REST_C + 2 * WIDTH]

    own_head = ((_iota2((SUBLANES, MEM_LEN * HEADS), 1) % HEADS)
                == (_iota2((SUBLANES, MEM_LEN * HEADS), 0) % HEADS))

    scores = []
    for s in range(sb):
        qh = jnp.concatenate([cq[s:s + 1, hd * HEAD_DIM:(hd + 1) * HEAD_DIM] for hd in range(HEADS)]
                             + [jnp.zeros((SUBLANES - HEADS, HEAD_DIM), F32)], axis=0)
        scores.append(_dot_nt(qh, ck_ref[s]) * (HEAD_DIM ** -0.5))

    carried = [cb_ref[j] for j in range(CONV_WIDTH - 1)]
    conv = pre * convw_ref[CONV_WIDTH - 1:CONV_WIDTH, :]
    for j in range(CONV_WIDTH - 1):
        conv = conv + carried[j] * convw_ref[j:j + 1, :]
    for j in range(1, CONV_WIDTH - 1):
        sc_ref[j - 1] = carried[j]
    sc_ref[CONV_WIDTH - 2] = pre
    qkv = jax.nn.silu(conv)

    row8 = _iota2((sb, HEAD_DIM), 0)
    row16 = _iota2((2 * sb, HEAD_DIM), 0)
    qs, ks, vs, kq_s = [], [], [], []
    for hd in range(HEADS):
        lo = hd * HEAD_DIM
        qs.append(_l2norm(qkv[:, lo:lo + HEAD_DIM]) * (HEAD_DIM ** -0.5))
        ks.append(_l2norm(qkv[:, WIDTH + lo:WIDTH + lo + HEAD_DIM]))
        vs.append(qkv[:, 2 * WIDTH + lo:2 * WIDTH + lo + HEAD_DIM])
        kq = jnp.concatenate([ks[hd], qs[hd]], axis=0).astype(BF16)
        kq_s.append([_dot(kq, s0_ref[s, hd]) for s in range(sb)])
    o_heads = []
    for hd in range(HEADS):
        lo = hd * HEAD_DIM
        sk = jnp.zeros((sb, HEAD_DIM), F32)
        sq = jnp.zeros((sb, HEAD_DIM), F32)
        for s in range(sb):
            sk = jnp.where(row8 == s, kq_s[hd][s][:sb], sk)
            sq = jnp.where(row8 == s, kq_s[hd][s][sb:], sq)
        a = decay[:, HEADS + hd:HEADS + hd + 1]
        v_new = beta[:, hd:hd + 1] * (vs[hd] - a * sk)
        qk = jnp.sum(qs[hd] * ks[hd], axis=-1, keepdims=True)
        o = a * sq + qk * v_new
        o_heads.append(_rms(o) * anormg_ref[...] * jax.nn.silu(agate[:, lo:lo + HEAD_DIM]))
        kt = jnp.concatenate([ks[hd], jnp.zeros((HEAD_DIM - sb, HEAD_DIM), F32)], axis=0).T.astype(BF16)
        v_pad = jnp.concatenate([v_new, jnp.zeros((sb, HEAD_DIM), F32)], axis=0)
        zeros = jnp.zeros((HEAD_DIM - 2 * sb, HEAD_DIM), BF16)
        for s in range(sb):
            only_s = jnp.concatenate([jnp.where(row16 == s, v_pad, 0.0).astype(BF16), zeros], axis=0)
            sd_ref[s, hd] = a[s:s + 1, :] * s0_ref[s, hd] + _dot(kt, only_s)
    ya_scr[rows, :] = jnp.concatenate(o_heads, axis=1)

    probs, sums = [], []
    for s in range(sb):
        sc = jnp.where(own_head, scores[s], -1e30)
        sc = sc - jnp.max(sc, axis=-1, keepdims=True)
        probs.append(jnp.where(own_head, jnp.exp(sc), 0.0))
        sums.append(jnp.sum(probs[s], axis=-1, keepdims=True))
    yc_rows = []
    for s in range(sb):
        oc = _dot(probs[s], cv_ref[s]) / sums[s]
        oc = jnp.concatenate([oc[hd:hd + 1, :] for hd in range(HEADS)], axis=1)
        yc_rows.append(oc * jax.nn.silu(cgate[s:s + 1, :]))
    yc_scr[rows, :] = jnp.concatenate(yc_rows, axis=0)

    @pl.when(i == nsteps - 1)
    def _():
        y_ref[...] = _merge_and_project(x_ref[...], hb_scr[...], ya_scr[...],
                                        rest_scr[:, REST_B:REST_B + WIDTH], yc_scr[...], w_ref,
                                        bgate_ref, wbra_ref, wbrb_ref, wbrc_ref, wout_ref, fng_ref)


def _sample_layer(x, cache_k, cache_v, s0, cb, weights, params):
    n = x.shape[0]
    sb = SAMPLE_BLOCK
    full = lambda a: pl.BlockSpec(a.shape, lambda i: (0,) * a.ndim, pipeline_mode=pl.Buffered(1))
    in_specs = [
        full(x),
        pl.BlockSpec((sb, MEM_LEN * HEADS, HEAD_DIM), lambda i: (i, 0, 0)),
        pl.BlockSpec((sb, MEM_LEN * HEADS, HEAD_DIM), lambda i: (i, 0, 0)),
        pl.BlockSpec((sb, HEADS, HEAD_DIM, HEAD_DIM), lambda i: (i, 0, 0, 0)),
        pl.BlockSpec((CONV_WIDTH - 1, sb, CONV_CH), lambda i: (0, i, 0)),
    ] + [full(w) for w in weights] + [full(p) for p in params]
    out_specs = [
        pl.BlockSpec((n, D_MODEL), lambda i: (0, 0)),
        pl.BlockSpec((sb, HEADS, HEAD_DIM, HEAD_DIM), lambda i: (i, 0, 0, 0)),
        pl.BlockSpec((CONV_WIDTH - 1, sb, CONV_CH), lambda i: (0, i, 0)),
        pl.BlockSpec((n, WIDTH), lambda i: (0, 0)),
    ]
    out_shape = [
        jax.ShapeDtypeStruct((n, D_MODEL), F32),
        jax.ShapeDtypeStruct((n, HEADS, HEAD_DIM, HEAD_DIM), F32),
        jax.ShapeDtypeStruct((CONV_WIDTH - 1, n, CONV_CH), F32),
        jax.ShapeDtypeStruct((n, WIDTH), F32),
    ]
    return pl.pallas_call(
        _sample_kernel,
        grid=(n // sb,),
        in_specs=in_specs,
        out_specs=out_specs,
        out_shape=out_shape,
        scratch_shapes=[pltpu.VMEM((n, D_MODEL), BF16),
                        pltpu.VMEM((n, CONV_CH), F32),
                        pltpu.VMEM((n, REST_MG), F32),
                        pltpu.VMEM((n, LANES), F32),
                        pltpu.VMEM((n, WIDTH), F32),
                        pltpu.VMEM((n, WIDTH), F32)],
        compiler_params=pltpu.CompilerParams(dimension_semantics=("arbitrary",),
                                             vmem_limit_bytes=VMEM_LIMIT),
        name="sample_layer",
    )(x, cache_k, cache_v, s0, cb, *weights, *params)


PACK_BLOCK = 512


def _pack_kernel(a_ref, b_ref, o_ref, bg_ref):
    j = pl.program_id(0)
    first_rest = CONV_CH // PACK_BLOCK
    a = a_ref[...]
    skipped = jnp.concatenate([a[2 * HEADS:], b_ref[...]], axis=0)
    rows = jnp.where(j < first_rest, a, skipped)
    o_ref[...] = rows.T.astype(BF16)

    @pl.when(j == first_rest)
    def _():
        logits = jnp.concatenate([a[:2 * HEADS], jnp.zeros((LANES - 2 * HEADS, D_MODEL), F32)], axis=0)
        bg_ref[...] = logits.T.astype(BF16)


def _pack_input_projection(w_t):
    assert 2 * HEADS == SUBLANES and CONV_CH % PACK_BLOCK == 0 and PACKED_COLS % PACK_BLOCK == 0
    return pl.pallas_call(
        _pack_kernel,
        grid=(PACKED_COLS // PACK_BLOCK,),
        in_specs=[pl.BlockSpec((PACK_BLOCK, D_MODEL), lambda j: (j, 0)),
                  pl.BlockSpec((SUBLANES, D_MODEL), lambda j: ((j + 1) * (PACK_BLOCK // SUBLANES), 0))],
        out_specs=[pl.BlockSpec((D_MODEL, PACK_BLOCK), lambda j: (0, j)),
                   pl.BlockSpec((D_MODEL, LANES), lambda j: (0, 0))],
        out_shape=[jax.ShapeDtypeStruct((D_MODEL, PACKED_COLS), BF16),
                   jax.ShapeDtypeStruct((D_MODEL, LANES), BF16)],
        compiler_params=pltpu.CompilerParams(dimension_semantics=("arbitrary",),
                                             vmem_limit_bytes=VMEM_LIMIT),
        name="pack_input_projection",
    )(w_t, w_t)


def _lanes_4_to_7(vec):
    return jnp.zeros((1, LANES), F32).at[0, HEADS:2 * HEADS].set(vec)


def kernel(x_prompt, x_sample, cache_mem_k, cache_mem_v, state_delta, state_conv, mem_prompt, norm_g, w_in, conv_w, a_log, dt_bias, a_norm_g, ln_v_g, ln_v_b, w_spatial, b_spatial, mem_norm_g, w_mem_kv, w_br_a, w_br_b, w_br_c, b_gate, w_out, final_norm_g):
    depth = norm_g.shape[0]
    assert depth == 1, "single-layer step"
    bsz, seq, _ = x_prompt.shape
    nsmp = x_sample.shape[0]
    assert x_sample.shape[1] == 1 and seq % PROMPT_BLOCK == 0 and nsmp % SAMPLE_BLOCK == 0
    assert w_in.shape[2] == CONV_CH + 2 * HEADS + REST_COLS

    weights = _pack_input_projection(w_in[0].T)
    params = (
        norm_g[0][None, :], conv_w[0], _lanes_4_to_7(a_log[0]), _lanes_4_to_7(dt_bias[0]),
        a_norm_g[0][None, :], ln_v_g[0][None, :], ln_v_b[0][None, :], w_spatial[0],
        jnp.broadcast_to(b_spatial[0][:, :, None], (MLP_GROUPS, MLP_CHUNK, WIDTH // MLP_GROUPS)),
        w_br_a[0].astype(BF16), w_br_b[0].astype(BF16), w_br_c[0].astype(BF16),
        b_gate[0].reshape(1, N_BRANCH * D_MODEL), w_out[0].astype(BF16), final_norm_g[None, :],
    )

    mk, mv, mkb, mvb = _memory_kv(mem_prompt.reshape(bsz * MEM_LEN, D_MODEL), mem_norm_g[0][None, :],
                                  w_mem_kv[0].astype(BF16))
    y_p, sd_p, sc_p = _prompt_layer(x_prompt, mkb.reshape(bsz, MEM_LEN, WIDTH),
                                    mvb.reshape(bsz, MEM_LEN, WIDTH), weights, params)
    y_s, sd_s, sc_s, vn_s = _sample_layer(
        x_sample.reshape(nsmp, D_MODEL), cache_mem_k.reshape(nsmp, MEM_LEN * HEADS, HEAD_DIM),
        cache_mem_v.reshape(nsmp, MEM_LEN * HEADS, HEAD_DIM), state_delta.reshape(state_delta.shape[1:]),
        jnp.transpose(state_conv[0], (1, 0, 2)), weights, params)

    kv_shape = (1, bsz, MEM_LEN, HEADS, HEAD_DIM)
    return (y_p, y_s.reshape(nsmp, 1, D_MODEL), sd_p[None], sc_p[None], mk.reshape(kv_shape),
            mv.reshape(kv_shape), sd_s[None], jnp.transpose(sc_s, (1, 0, 2))[None],
            vn_s.reshape(1, nsmp, 1, WIDTH))
```

```python
import math

import jax
import jax.numpy as jnp
from jax import lax
from jax.experimental import pallas as pl
from jax.experimental.pallas import tpu as pltpu

F32 = jnp.float32
BF16 = jnp.bfloat16

D_MODEL = 1024
HEADS = 4
HEAD_DIM = 128
WIDTH = HEADS * HEAD_DIM
CONV_WIDTH = 4
CONV_CH = 3 * WIDTH
MLP_GROUPS = 4
MLP_CHUNK = 128
MEM_LEN = 256
N_BRANCH = 3
EPS = 1e-6

LANES = 128
SUBLANES = 8

REST_AGATE = 0
REST_B = REST_AGATE + WIDTH
REST_C = REST_B + 3 * WIDTH
REST_MG = REST_C + 2 * WIDTH
REST_COLS = REST_MG + N_BRANCH * D_MODEL
PACKED_COLS = CONV_CH + REST_COLS

PROMPT_BLOCK = 512
DELTA_CHUNK = 64
SAMPLE_BLOCK = 8
VMEM_LIMIT = 56 * 1024 * 1024


def _dot(a, b):
    return jnp.dot(a.astype(BF16), b.astype(BF16), preferred_element_type=F32)


def _dot_nt(a, b):
    return lax.dot_general(a.astype(BF16), b.astype(BF16), (((1,), (1,)), ((), ())),
                           preferred_element_type=F32)


def _rms(x):
    return x * lax.rsqrt(jnp.mean(x * x, axis=-1, keepdims=True) + EPS)


def _l2norm(x, scale=1.0):
    return x * (lax.rsqrt(jnp.sum(x * x, axis=-1, keepdims=True) + EPS) * scale)


def _softplus(x):
    return jnp.maximum(x, 0.0) + jnp.log1p(jnp.exp(-jnp.abs(x)))


def _iota2(shape, dim):
    return lax.broadcasted_iota(jnp.int32, shape, dim)


def _memkv_kernel(mem_ref, g_ref, w_ref, k_ref, v_ref, kb_ref, vb_ref):
    xn = _rms(mem_ref[...]) * g_ref[...]
    kv = _dot(xn, w_ref[...])
    k = kv[:, :WIDTH]
    v = kv[:, WIDTH:]
    npos = k.shape[0]
    for hd in range(HEADS):
        k_ref[pl.ds(hd, npos, stride=HEADS), :] = k[:, hd * HEAD_DIM:(hd + 1) * HEAD_DIM]
        v_ref[pl.ds(hd, npos, stride=HEADS), :] = v[:, hd * HEAD_DIM:(hd + 1) * HEAD_DIM]
    kb_ref[...] = k.astype(BF16)
    vb_ref[...] = v.astype(BF16)


def _memory_kv(mem2d, mem_norm_g, w_mem_kv_bf):
    rows = mem2d.shape[0]
    blk = 512
    full = lambda shape: pl.BlockSpec(shape, lambda i: (0,) * len(shape))
    row_spec = lambda width: pl.BlockSpec((blk, width), lambda i: (i, 0))
    by_head = pl.BlockSpec((blk * HEADS, HEAD_DIM), lambda i: (i, 0))
    return pl.pallas_call(
        _memkv_kernel,
        grid=(rows // blk,),
        in_specs=[row_spec(D_MODEL), full((1, D_MODEL)), full((D_MODEL, 2 * WIDTH))],
        out_specs=[by_head] * 2 + [row_spec(WIDTH)] * 2,
        out_shape=[jax.ShapeDtypeStruct((rows * HEADS, HEAD_DIM), F32)] * 2
        + [jax.ShapeDtypeStruct((rows, WIDTH), BF16)] * 2,
        compiler_params=pltpu.CompilerParams(dimension_semantics=("arbitrary",),
                                             vmem_limit_bytes=VMEM_LIMIT),
        name="memory_kv",
    )(mem2d, mem_norm_g, w_mem_kv_bf)


def _beta_and_log_decay(bg, alog, dtb):
    beta = jax.nn.sigmoid(bg)
    g = -jnp.exp(alog) * _softplus(bg + dtb)
    return beta, g


def _spatial_weights(ws_ref):
    tril = _iota2((MLP_CHUNK, MLP_CHUNK), 0) >= _iota2((MLP_CHUNK, MLP_CHUNK), 1)
    return [jnp.where(tril, ws_ref[g], 0.0).astype(BF16) for g in range(MLP_GROUPS)]


def _layernorm(v, g, b):
    mu = jnp.mean(v, axis=-1, keepdims=True)
    vc = v - mu
    return vc * lax.rsqrt(jnp.mean(vc * vc, axis=-1, keepdims=True) + EPS) * g + b


def _rest(w_ref, lo, hi):
    return w_ref[:, CONV_CH + lo:CONV_CH + hi]


def _merge_and_project(x, hb, ya, yb, yc, w_ref, bgate_ref, wbra_ref, wbrb_ref, wbrc_ref, wout_ref,
                       fng_ref):
    merged = None
    for i, (yi, wbr) in enumerate(((ya, wbra_ref), (yb, wbrb_ref), (yc, wbrc_ref))):
        lo = i * D_MODEL
        gate = jax.nn.sigmoid(_dot(hb, _rest(w_ref, REST_MG + lo, REST_MG + lo + D_MODEL))
                              + bgate_ref[:, lo:lo + D_MODEL])
        term = gate * _dot(yi, wbr[...])
        merged = term if merged is None else merged + term
    out = x + _dot(merged, wout_ref[...])
    return _rms(out) * fng_ref[...]


def _chunk_cumsum(g, ck, scr):
    n = g.shape[0]
    pad = ck // 2
    pos = _iota2(g.shape, 0) % ck
    scr[0:pad, :] = jnp.zeros((pad, LANES), F32)
    d = 1
    while d < ck:
        scr[pad:pad + n, :] = g
        g = g + jnp.where(pos >= d, scr[pad - d:pad - d + n, :], 0.0)
        d *= 2
    return g


def _delta_pair_terms(q, k, v, beta_col, gc_col, gc_row):
    n = q.shape[0]
    ck = n // 2
    ri = _iota2((n, n), 0)
    ci = _iota2((n, n), 1)
    same = (ri < ck) == (ci < ck)
    kt = k.T
    decay = jnp.exp(jnp.where(same & (ri >= ci), gc_col - gc_row, -1e30))
    kb = k * beta_col
    egc = jnp.exp(gc_col)
    low = jnp.where(same & (ri > ci), _dot(kb, kt) * decay, 0.0)
    last_a = gc_row[:, ck - 1:ck]
    last_b = gc_row[:, n - 1:n]
    in_a = _iota2((1, n), 1) < ck
    kdt = kt * jnp.exp(jnp.where(in_a, last_a, last_b) - gc_row)
    return dict(
        neg_low=-(low[:ck] + low[ck:]),
        attn=jnp.where(same & (ri >= ci), _dot(q, kt) * decay, 0.0),
        rhs=jnp.concatenate([v * beta_col, kb * egc], axis=1),
        qd=q * egc,
        kdt_a=kdt[:, :ck],
        kdt_b=jnp.where(in_a, 0.0, kdt),
        gl=(jnp.exp(last_a), jnp.exp(last_b)),
    )


def _block_diag2(packed):
    n = packed.shape[0]
    left = _iota2(packed.shape, 1) < n
    return jnp.concatenate([jnp.where(left, packed, 0.0), jnp.where(left, 0.0, packed)],
                           axis=0).astype(BF16)


def _prompt_kernel(x_ref, mk_ref, mv_ref, w_ref, wbg_ref, normg_ref, convw_ref,
                   alog_ref, dtb_ref, anormg_ref, lng_ref, lnb_ref, ws_ref, bs_ref, wbra_ref, wbrb_ref,
                   wbrc_ref, bgate_ref, wout_ref, fng_ref,
                   y_ref, sd_ref, sc_ref,
                   convbuf, state, cum_scr):
    tb = PROMPT_BLOCK
    ck = DELTA_CHUNK
    nck = tb // ck
    t = pl.program_id(1)

    @pl.when(t == 0)
    def _():
        convbuf[:, 0:SUBLANES, :] = jnp.zeros((CONV_CH // LANES, SUBLANES, LANES), F32)
        state[...] = jnp.zeros_like(state)

    hb = (_rms(x_ref[0]) * normg_ref[...]).astype(BF16)

    pre = _dot(hb, w_ref[:, :CONV_CH])
    bg_col = _dot(hb, wbg_ref[...])
    pb = _dot(hb, _rest(w_ref, REST_B, REST_B + 3 * WIDTH))
    pc = _dot(hb, _rest(w_ref, REST_C, REST_C + 2 * WIDTH))
    agate = _dot(hb, _rest(w_ref, REST_AGATE, REST_AGATE + WIDTH))

    conv_tiles, tails = [], []
    for ct in range(CONV_CH // LANES):
        cols = slice(ct * LANES, (ct + 1) * LANES)
        convbuf[ct, SUBLANES:SUBLANES + tb, :] = pre[:, cols]
        acc = pre[:, cols] * convw_ref[CONV_WIDTH - 1:CONV_WIDTH, cols]
        for j in range(CONV_WIDTH - 1):
            shift = CONV_WIDTH - 1 - j
            acc = acc + convbuf[ct, SUBLANES - shift:SUBLANES - shift + tb, :] * convw_ref[j:j + 1, cols]
        conv_tiles.append(acc)
        tails.append(convbuf[ct, tb:tb + SUBLANES, :])
        convbuf[ct, 0:SUBLANES, :] = tails[ct]
    sc_ref[0] = jnp.concatenate(tails, axis=1)[SUBLANES - (CONV_WIDTH - 1):, :]
    qkv = jax.nn.silu(jnp.concatenate(conv_tiles, axis=1))

    beta_col, g_col = _beta_and_log_decay(bg_col, alog_ref[...], dtb_ref[...])
    n2 = 2 * ck
    assert nck % 2 == 0 and n2 == LANES
    npair = nck // 2
    gc_col = _chunk_cumsum(g_col, ck, cum_scr)
    gc_cols = [gc_col[j * n2:(j + 1) * n2, :] for j in range(npair)]
    gc_rows = [gc.T[:SUBLANES, :] for gc in gc_cols]

    groups = [(j, hd) for j in range(npair) for hd in range(HEADS)]
    qn, kn, vs = [], [], []
    for hd in range(HEADS):
        lo = hd * HEAD_DIM
        qn.append(_l2norm(qkv[:, lo:lo + HEAD_DIM], HEAD_DIM ** -0.5))
        kn.append(_l2norm(qkv[:, WIDTH + lo:WIDTH + lo + HEAD_DIM]))
        vs.append(qkv[:, 2 * WIDTH + lo:2 * WIDTH + lo + HEAD_DIM])
    terms = {}
    for j, hd in groups:
        rows = slice(j * n2, (j + 1) * n2)
        terms[j, hd] = _delta_pair_terms(
            qn[hd][rows], kn[hd][rows], vs[hd][rows], beta_col[rows, hd:hd + 1],
            gc_cols[j][:, HEADS + hd:HEADS + hd + 1], gc_rows[j][HEADS + hd:HEADS + hd + 1, :])

    eye2 = (_iota2((ck, n2), 0) == _iota2((ck, n2), 1) % ck).astype(F32)
    powers = [terms[g]["neg_low"] for g in groups]
    invs = [eye2 + p for p in powers]
    powers = [_dot(p, _block_diag2(p)) for p in powers]

    def inverse_level(last=False):
        nonlocal powers, invs
        if last:
            invs = [inv + _dot(inv, _block_diag2(p)) for inv, p in zip(invs, powers)]
            return
        prods = [_dot(jnp.concatenate([inv, p], axis=0), _block_diag2(p)) for inv, p in zip(invs, powers)]
        invs = [inv + r[:ck] for inv, r in zip(invs, prods)]
        powers = [r[ck:] for r in prods]

    levels = 0
    while 2 ** (levels + 1) < ck:
        levels += 1
    assert levels == 5, "the side work below is placed for five squaring levels"

    scores = [_dot_nt(pc[:, hd * HEAD_DIM:(hd + 1) * HEAD_DIM], mk_ref[0, :, hd * HEAD_DIM:(hd + 1) * HEAD_DIM])
              for hd in range(HEADS)]
    inverse_level()
    bu = pb[:, :WIDTH]
    vn = _layernorm(pb[:, WIDTH:2 * WIDTH], lng_ref[...], lnb_ref[...])
    ws = _spatial_weights(ws_ref)
    gw = WIDTH // MLP_GROUPS
    s_rows = []
    for n in range(tb // MLP_CHUNK):
        r0 = n * MLP_CHUNK
        s_rows.append(jnp.concatenate(
            [_dot(ws[g], vn[r0:r0 + MLP_CHUNK, g * gw:(g + 1) * gw]) + bs_ref[g]
             for g in range(MLP_GROUPS)], axis=1))
    yb = bu * jnp.concatenate(s_rows, axis=0) * jax.nn.silu(pb[:, 2 * WIDTH:])
    inverse_level()
    inverse_level()
    oc = []
    for hd in range(HEADS):
        p = jnp.exp2((scores[hd] - jnp.max(scores[hd], axis=-1, keepdims=True))
                     * (HEAD_DIM ** -0.5 * math.log2(math.e)))
        oc.append(_dot(p, mv_ref[0, :, hd * HEAD_DIM:(hd + 1) * HEAD_DIM])
                  / jnp.sum(p, axis=-1, keepdims=True))
    yc = jnp.concatenate(oc, axis=1) * jax.nn.silu(pc[:, WIDTH:])
    inverse_level()
    inverse_level(last=True)
    uws = {g: _dot(_block_diag2(inv), terms[g]["rhs"]) for g, inv in zip(groups, invs)}

    n_slots = 2 * nck
    gate_cols = 2 * D_MODEL // nck
    n_gate = N_BRANCH * D_MODEL // gate_cols
    out_parts = (n_slots - n_gate) // 2
    assert n_gate + 2 * out_parts == n_slots and D_MODEL % gate_cols == 0
    out_cols = D_MODEL // out_parts
    gate_parts = []
    out_b, out_c = [], []

    def gate_piece(j):
        lo = j * gate_cols
        gate_parts.append(jax.nn.sigmoid(_dot(hb, _rest(w_ref, REST_MG + lo, REST_MG + lo + gate_cols))
                                         + bgate_ref[:, lo:lo + gate_cols]))

    side = [lambda j=j: gate_piece(j) for j in range(n_gate)]
    side += [lambda j=j: out_b.append(_dot(yb, wbrb_ref[:, j * out_cols:(j + 1) * out_cols]))
             for j in range(out_parts)]
    side += [lambda j=j: out_c.append(_dot(yc, wbrc_ref[:, j * out_cols:(j + 1) * out_cols]))
             for j in range(out_parts)]

    s_heads = [state[hd] for hd in range(HEADS)]
    o_chunks = [[] for _ in range(HEADS)]
    v_new_a = [None] * HEADS
    for c in range(nck):
        j, second = divmod(c, 2)
        rows = slice(ck, n2) if second else slice(0, ck)
        ws_qs = []
        for hd in range(HEADS):
            wq = jnp.concatenate([uws[j, hd][rows, HEAD_DIM:], terms[j, hd]["qd"][rows]], axis=0)
            ws_qs.append(_dot(wq, s_heads[hd]))
        side[2 * c]()
        for hd in range(HEADS):
            tm = terms[j, hd]
            v_new = uws[j, hd][rows, :HEAD_DIM] - ws_qs[hd][:ck]
            if second:
                v_pair = jnp.concatenate([v_new_a[hd], v_new], axis=0)
                o_chunks[hd].append(ws_qs[hd][ck:] + _dot(tm["attn"][rows], v_pair))
                s_heads[hd] = s_heads[hd] * tm["gl"][1] + _dot(tm["kdt_b"], v_pair)
            else:
                v_new_a[hd] = v_new
                o_chunks[hd].append(ws_qs[hd][ck:] + _dot(tm["attn"][rows, :ck], v_new))
                s_heads[hd] = s_heads[hd] * tm["gl"][0] + _dot(tm["kdt_a"], v_new)
        side[2 * c + 1]()
    for hd in range(HEADS):
        state[hd] = s_heads[hd]
    sd_ref[0] = state[...]

    o_heads = []
    for hd in range(HEADS):
        lo = hd * HEAD_DIM
        o = jnp.concatenate(o_chunks[hd], axis=0)
        o_heads.append(_rms(o) * anormg_ref[...] * jax.nn.silu(agate[:, lo:lo + HEAD_DIM]))
    branch_out = [_dot(jnp.concatenate(o_heads, axis=1), wbra_ref[...]),
                  jnp.concatenate(out_b, axis=1), jnp.concatenate(out_c, axis=1)]
    gates = jnp.concatenate(gate_parts, axis=1)
    merged = None
    for i in range(N_BRANCH):
        term = gates[:, i * D_MODEL:(i + 1) * D_MODEL] * branch_out[i]
        merged = term if merged is None else merged + term
    out = x_ref[0] + _dot(merged, wout_ref[...])
    y_ref[0] = _rms(out) * fng_ref[...]


def _prompt_layer(x, mkb, mvb, weights, params):
    bsz, seq, _ = x.shape
    tb = PROMPT_BLOCK
    nt = seq // tb
    full = lambda a: pl.BlockSpec(a.shape, lambda b, t: (0,) * a.ndim, pipeline_mode=pl.Buffered(1))
    in_specs = [
        pl.BlockSpec((1, tb, D_MODEL), lambda b, t: (b, t, 0)),
        pl.BlockSpec((1, MEM_LEN, WIDTH), lambda b, t: (b, 0, 0)),
        pl.BlockSpec((1, MEM_LEN, WIDTH), lambda b, t: (b, 0, 0)),
    ] + [full(w) for w in weights] + [full(p) for p in params]
    out_specs = [
        pl.BlockSpec((1, tb, D_MODEL), lambda b, t: (b, t, 0)),
        pl.BlockSpec((1, HEADS, HEAD_DIM, HEAD_DIM), lambda b, t: (b, 0, 0, 0)),
        pl.BlockSpec((1, CONV_WIDTH - 1, CONV_CH), lambda b, t: (b, 0, 0)),
    ]
    out_shape = [
        jax.ShapeDtypeStruct((bsz, seq, D_MODEL), F32),
        jax.ShapeDtypeStruct((bsz, HEADS, HEAD_DIM, HEAD_DIM), F32),
        jax.ShapeDtypeStruct((bsz, CONV_WIDTH - 1, CONV_CH), F32),
    ]
    return pl.pallas_call(
        _prompt_kernel,
        grid=(bsz, nt),
        in_specs=in_specs,
        out_specs=out_specs,
        out_shape=out_shape,
        scratch_shapes=[pltpu.VMEM((CONV_CH // LANES, tb + SUBLANES, LANES), F32),
                        pltpu.VMEM((HEADS, HEAD_DIM, HEAD_DIM), F32),
                        pltpu.VMEM((DELTA_CHUNK // 2 + tb, LANES), F32)],
        compiler_params=pltpu.CompilerParams(dimension_semantics=("arbitrary", "arbitrary"),
                                             vmem_limit_bytes=VMEM_LIMIT),
        name="prompt_layer",
    )(x, mkb, mvb, *weights, *params)


def _sample_kernel(x_ref, ck_ref, cv_ref, s0_ref, cb_ref, w_ref, wbg_ref,
                   normg_ref, convw_ref, alog_ref, dtb_ref, anormg_ref, lng_ref, lnb_ref, ws_ref, bs_ref,
                   wbra_ref, wbrb_ref, wbrc_ref, bgate_ref, wout_ref, fng_ref,
                   y_ref, sd_ref, sc_ref, vn_ref,
                   hb_scr, pre_scr, rest_scr, bg_scr, ya_scr, yc_scr):
    sb = SAMPLE_BLOCK
    i = pl.program_id(0)
    nsteps = pl.num_programs(0)

    @pl.when(i == 0)
    def _():
        hb = (_rms(x_ref[...]) * normg_ref[...]).astype(BF16)
        hb_scr[...] = hb
        pre_scr[...] = _dot(hb, w_ref[:, :CONV_CH])
        rest_scr[...] = _dot(hb, _rest(w_ref, 0, REST_MG))
        bg_scr[...] = _dot(hb, wbg_ref[...])
        bu = rest_scr[:, REST_B:REST_B + WIDTH]
        vn = _layernorm(rest_scr[:, REST_B + WIDTH:REST_B + 2 * WIDTH], lng_ref[...], lnb_ref[...])
        vn_ref[...] = vn
        bgate = rest_scr[:, REST_B + 2 * WIDTH:REST_B + 3 * WIDTH]
        gw = WIDTH // MLP_GROUPS
        s = jnp.concatenate(
            [vn[:, g * gw:(g + 1) * gw] * ws_ref[g, 0:1, 0:1] + bs_ref[g, 0:1, :]
             for g in range(MLP_GROUPS)], axis=1)
        rest_scr[:, REST_B:REST_B + WIDTH] = bu * s * jax.nn.silu(bgate)

    r0 = pl.multiple_of(i * sb, sb)
    rows = pl.ds(r0, sb)
    pre = pre_scr[rows, :]
    beta, g = _beta_and_log_decay(bg_scr[rows, :], alog_ref[...], dtb_ref[...])
    decay = jnp.exp(g)
    agate = rest_scr[rows, REST_AGATE:REST_AGATE + WIDTH]
    cq = rest_scr[rows, REST_C:REST_C + WIDTH]
    cgate = rest_scr[rows, REST_C + WIDTH:REST_C + 2 * WIDTH]

    own_head = ((_iota2((SUBLANES, MEM_LEN * HEADS), 1) % HEADS)
                == (_iota2((SUBLANES, MEM_LEN * HEADS), 0) % HEADS))

    scores = []
    for s in range(sb):
        qh = jnp.concatenate([cq[s:s + 1, hd * HEAD_DIM:(hd + 1) * HEAD_DIM] for hd in range(HEADS)]
                             + [jnp.zeros((SUBLANES - HEADS, HEAD_DIM), F32)], axis=0)
        scores.append(_dot_nt(qh, ck_ref[s]) * (HEAD_DIM ** -0.5))

    carried = [cb_ref[j] for j in range(CONV_WIDTH - 1)]
    conv = pre * convw_ref[CONV_WIDTH - 1:CONV_WIDTH, :]
    for j in range(CONV_WIDTH - 1):
        conv = conv + carried[j] * convw_ref[j:j + 1, :]
    for j in range(1, CONV_WIDTH - 1):
        sc_ref[j - 1] = carried[j]
    sc_ref[CONV_WIDTH - 2] = pre
    qkv = jax.nn.silu(conv)

    row8 = _iota2((sb, HEAD_DIM), 0)
    row16 = _iota2((2 * sb, HEAD_DIM), 0)
    qs, ks, vs, kq_s = [], [], [], []
    for hd in range(HEADS):
        lo = hd * HEAD_DIM
        qs.append(_l2norm(qkv[:, lo:lo + HEAD_DIM], HEAD_DIM ** -0.5))
        ks.append(_l2norm(qkv[:, WIDTH + lo:WIDTH + lo + HEAD_DIM]))
        vs.append(qkv[:, 2 * WIDTH + lo:2 * WIDTH + lo + HEAD_DIM])
        kq = jnp.concatenate([ks[hd], qs[hd]], axis=0).astype(BF16)
        kq_s.append([_dot(kq, s0_ref[s, hd]) for s in range(sb)])
    o_heads = []
    for hd in range(HEADS):
        lo = hd * HEAD_DIM
        sk = jnp.zeros((sb, HEAD_DIM), F32)
        sq = jnp.zeros((sb, HEAD_DIM), F32)
        for s in range(sb):
            sk = jnp.where(row8 == s, kq_s[hd][s][:sb], sk)
            sq = jnp.where(row8 == s, kq_s[hd][s][sb:], sq)
        a = decay[:, HEADS + hd:HEADS + hd + 1]
        v_new = beta[:, hd:hd + 1] * (vs[hd] - a * sk)
        qk = jnp.sum(qs[hd] * ks[hd], axis=-1, keepdims=True)
        o = a * sq + qk * v_new
        o_heads.append(_rms(o) * anormg_ref[...] * jax.nn.silu(agate[:, lo:lo + HEAD_DIM]))
        kt = jnp.concatenate([ks[hd], jnp.zeros((HEAD_DIM - sb, HEAD_DIM), F32)], axis=0).T.astype(BF16)
        v_pad = jnp.concatenate([v_new, jnp.zeros((sb, HEAD_DIM), F32)], axis=0)
        zeros = jnp.zeros((HEAD_DIM - 2 * sb, HEAD_DIM), BF16)
        for s in range(sb):
            only_s = jnp.concatenate([jnp.where(row16 == s, v_pad, 0.0).astype(BF16), zeros], axis=0)
            sd_ref[s, hd] = a[s:s + 1, :] * s0_ref[s, hd] + _dot(kt, only_s)
    ya_scr[rows, :] = jnp.concatenate(o_heads, axis=1)

    probs, sums = [], []
    for s in range(sb):
        sc = jnp.where(own_head, scores[s], -1e30)
        sc = sc - jnp.max(sc, axis=-1, keepdims=True)
        probs.append(jnp.where(own_head, jnp.exp(sc), 0.0))
        sums.append(jnp.sum(probs[s], axis=-1, keepdims=True))
    yc_rows = []
    for s in range(sb):
        oc = _dot(probs[s], cv_ref[s]) / sums[s]
        oc = jnp.concatenate([oc[hd:hd + 1, :] for hd in range(HEADS)], axis=1)
        yc_rows.append(oc * jax.nn.silu(cgate[s:s + 1, :]))
    yc_scr[rows, :] = jnp.concatenate(yc_rows, axis=0)

    @pl.when(i == nsteps - 1)
    def _():
        y_ref[...] = _merge_and_project(x_ref[...], hb_scr[...], ya_scr[...],
                                        rest_scr[:, REST_B:REST_B + WIDTH], yc_scr[...], w_ref,
                                        bgate_ref, wbra_ref, wbrb_ref, wbrc_ref, wout_ref, fng_ref)


def _sample_layer(x, cache_k, cache_v, s0, cb, weights, params):
    n = x.shape[0]
    sb = SAMPLE_BLOCK
    full = lambda a: pl.BlockSpec(a.shape, lambda i: (0,) * a.ndim, pipeline_mode=pl.Buffered(1))
    in_specs = [
        full(x),
        pl.BlockSpec((sb, MEM_LEN * HEADS, HEAD_DIM), lambda i: (i, 0, 0)),
        pl.BlockSpec((sb, MEM_LEN * HEADS, HEAD_DIM), lambda i: (i, 0, 0)),
        pl.BlockSpec((sb, HEADS, HEAD_DIM, HEAD_DIM), lambda i: (i, 0, 0, 0)),
        pl.BlockSpec((CONV_WIDTH - 1, sb, CONV_CH), lambda i: (0, i, 0)),
    ] + [full(w) for w in weights] + [full(p) for p in params]
    out_specs = [
        pl.BlockSpec((n, D_MODEL), lambda i: (0, 0)),
        pl.BlockSpec((sb, HEADS, HEAD_DIM, HEAD_DIM), lambda i: (i, 0, 0, 0)),
        pl.BlockSpec((CONV_WIDTH - 1, sb, CONV_CH), lambda i: (0, i, 0)),
        pl.BlockSpec((n, WIDTH), lambda i: (0, 0)),
    ]
    out_shape = [
        jax.ShapeDtypeStruct((n, D_MODEL), F32),
        jax.ShapeDtypeStruct((n, HEADS, HEAD_DIM, HEAD_DIM), F32),
        jax.ShapeDtypeStruct((CONV_WIDTH - 1, n, CONV_CH), F32),
        jax.ShapeDtypeStruct((n, WIDTH), F32),
    ]
    return pl.pallas_call(
        _sample_kernel,
        grid=(n // sb,),
        in_specs=in_specs,
        out_specs=out_specs,
        out_shape=out_shape,
        scratch_shapes=[pltpu.VMEM((n, D_MODEL), BF16),
                        pltpu.VMEM((n, CONV_CH), F32),
                        pltpu.VMEM((n, REST_MG), F32),
                        pltpu.VMEM((n, LANES), F32),
                        pltpu.VMEM((n, WIDTH), F32),
                        pltpu.VMEM((n, WIDTH), F32)],
        compiler_params=pltpu.CompilerParams(dimension_semantics=("arbitrary",),
                                             vmem_limit_bytes=VMEM_LIMIT),
        name="sample_layer",
    )(x, cache_k, cache_v, s0, cb, *weights, *params)


PACK_BLOCK = 512


def _pack_kernel(a_ref, b_ref, o_ref, bg_ref):
    j = pl.program_id(0)
    first_rest = CONV_CH // PACK_BLOCK
    a = a_ref[...]
    skipped = jnp.concatenate([a[2 * HEADS:], b_ref[...]], axis=0)
    rows = jnp.where(j < first_rest, a, skipped)
    o_ref[...] = rows.T.astype(BF16)

    @pl.when(j == first_rest)
    def _():
        logits = jnp.concatenate([a[:2 * HEADS], jnp.zeros((LANES - 2 * HEADS, D_MODEL), F32)], axis=0)
        bg_ref[...] = logits.T.astype(BF16)


def _pack_input_projection(w_t):
    assert 2 * HEADS == SUBLANES and CONV_CH % PACK_BLOCK == 0 and PACKED_COLS % PACK_BLOCK == 0
    return pl.pallas_call(
        _pack_kernel,
        grid=(PACKED_COLS // PACK_BLOCK,),
        in_specs=[pl.BlockSpec((PACK_BLOCK, D_MODEL), lambda j: (j, 0)),
                  pl.BlockSpec((SUBLANES, D_MODEL), lambda j: ((j + 1) * (PACK_BLOCK // SUBLANES), 0))],
        out_specs=[pl.BlockSpec((D_MODEL, PACK_BLOCK), lambda j: (0, j)),
                   pl.BlockSpec((D_MODEL, LANES), lambda j: (0, 0))],
        out_shape=[jax.ShapeDtypeStruct((D_MODEL, PACKED_COLS), BF16),
                   jax.ShapeDtypeStruct((D_MODEL, LANES), BF16)],
        compiler_params=pltpu.CompilerParams(dimension_semantics=("arbitrary",),
                                             vmem_limit_bytes=VMEM_LIMIT),
        name="pack_input_projection",
    )(w_t, w_t)


def _lanes_4_to_7(vec):
    return jnp.zeros((1, LANES), F32).at[0, HEADS:2 * HEADS].set(vec)


def kernel(x_prompt, x_sample, cache_mem_k, cache_mem_v, state_delta, state_conv, mem_prompt, norm_g, w_in, conv_w, a_log, dt_bias, a_norm_g, ln_v_g, ln_v_b, w_spatial, b_spatial, mem_norm_g, w_mem_kv, w_br_a, w_br_b, w_br_c, b_gate, w_out, final_norm_g):
    depth = norm_g.shape[0]
    assert depth == 1, "single-layer step"
    bsz, seq, _ = x_prompt.shape
    nsmp = x_sample.shape[0]
    assert x_sample.shape[1] == 1 and seq % PROMPT_BLOCK == 0 and nsmp % SAMPLE_BLOCK == 0
    assert w_in.shape[2] == CONV_CH + 2 * HEADS + REST_COLS

    weights = _pack_input_projection(w_in[0].T)
    params = (
        norm_g[0][None, :], conv_w[0], _lanes_4_to_7(a_log[0]), _lanes_4_to_7(dt_bias[0]),
        a_norm_g[0][None, :], ln_v_g[0][None, :], ln_v_b[0][None, :], w_spatial[0],
        jnp.broadcast_to(b_spatial[0][:, :, None], (MLP_GROUPS, MLP_CHUNK, WIDTH // MLP_GROUPS)),
        w_br_a[0].astype(BF16), w_br_b[0].astype(BF16), w_br_c[0].astype(BF16),
        b_gate[0].reshape(1, N_BRANCH * D_MODEL), w_out[0].astype(BF16), final_norm_g[None, :],
    )

    mk, mv, mkb, mvb = _memory_kv(mem_prompt.reshape(bsz * MEM_LEN, D_MODEL), mem_norm_g[0][None, :],
                                  w_mem_kv[0].astype(BF16))
    y_p, sd_p, sc_p = _prompt_layer(x_prompt, mkb.reshape(bsz, MEM_LEN, WIDTH),
                                    mvb.reshape(bsz, MEM_LEN, WIDTH), weights, params)
    y_s, sd_s, sc_s, vn_s = _sample_layer(
        x_sample.reshape(nsmp, D_MODEL), cache_mem_k.reshape(nsmp, MEM_LEN * HEADS, HEAD_DIM),
        cache_mem_v.reshape(nsmp, MEM_LEN * HEADS, HEAD_DIM), state_delta.reshape(state_delta.shape[1:]),
        jnp.transpose(state_conv[0], (1, 0, 2)), weights, params)

    kv_shape = (1, bsz, MEM_LEN, HEADS, HEAD_DIM)
    return (y_p, y_s.reshape(nsmp, 1, D_MODEL), sd_p[None], sc_p[None], mk.reshape(kv_shape),
            mv.reshape(kv_shape), sd_s[None], jnp.transpose(sc_s, (1, 0, 2))[None],
            vn_s.reshape(1, nsmp, 1, WIDTH))
```

```python
import math

import jax
import jax.numpy as jnp
from jax import lax
from jax.experimental import pallas as pl
from jax.experimental.pallas import tpu as pltpu

F32 = jnp.float32
BF16 = jnp.bfloat16

D_MODEL = 1024
HEADS = 4
HEAD_DIM = 128
WIDTH = HEADS * HEAD_DIM
CONV_WIDTH = 4
CONV_CH = 3 * WIDTH
MLP_GROUPS = 4
MLP_CHUNK = 128
MEM_LEN = 256
N_BRANCH = 3
EPS = 1e-6

LANES = 128
SUBLANES = 8

REST_AGATE = 0
REST_B = REST_AGATE + WIDTH
REST_C = REST_B + 3 * WIDTH
REST_MG = REST_C + 2 * WIDTH
REST_COLS = REST_MG + N_BRANCH * D_MODEL
PACKED_COLS = CONV_CH + REST_COLS

PROMPT_BLOCK = 512
DELTA_CHUNK = 64
SAMPLE_BLOCK = 8
VMEM_LIMIT = 56 * 1024 * 1024


def _dot(a, b):
    return jnp.dot(a.astype(BF16), b.astype(BF16), preferred_element_type=F32)


def _dot_nt(a, b):
    return lax.dot_general(a.astype(BF16), b.astype(BF16), (((1,), (1,)), ((), ())),
                           preferred_element_type=F32)


def _rms(x):
    return x * lax.rsqrt(jnp.mean(x * x, axis=-1, keepdims=True) + EPS)


def _l2norm(x, scale=1.0):
    return x * (lax.rsqrt(jnp.sum(x * x, axis=-1, keepdims=True) + EPS) * scale)


def _softplus(x):
    return jnp.maximum(x, 0.0) + jnp.log1p(jnp.exp(-jnp.abs(x)))


def _iota2(shape, dim):
    return lax.broadcasted_iota(jnp.int32, shape, dim)


def _memkv_kernel(mem_ref, g_ref, w_ref, k_ref, v_ref, kb_ref, vb_ref):
    xn = _rms(mem_ref[...]) * g_ref[...]
    kv = _dot(xn, w_ref[...])
    k = kv[:, :WIDTH]
    v = kv[:, WIDTH:]
    npos = k.shape[0]
    for hd in range(HEADS):
        k_ref[pl.ds(hd, npos, stride=HEADS), :] = k[:, hd * HEAD_DIM:(hd + 1) * HEAD_DIM]
        v_ref[pl.ds(hd, npos, stride=HEADS), :] = v[:, hd * HEAD_DIM:(hd + 1) * HEAD_DIM]
    kb_ref[...] = k.astype(BF16)
    vb_ref[...] = v.astype(BF16)


def _memory_kv(mem2d, mem_norm_g, w_mem_kv_bf):
    rows = mem2d.shape[0]
    blk = 512
    full = lambda shape: pl.BlockSpec(shape, lambda i: (0,) * len(shape))
    row_spec = lambda width: pl.BlockSpec((blk, width), lambda i: (i, 0))
    by_head = pl.BlockSpec((blk * HEADS, HEAD_DIM), lambda i: (i, 0))
    return pl.pallas_call(
        _memkv_kernel,
        grid=(rows // blk,),
        in_specs=[row_spec(D_MODEL), full((1, D_MODEL)), full((D_MODEL, 2 * WIDTH))],
        out_specs=[by_head] * 2 + [row_spec(WIDTH)] * 2,
        out_shape=[jax.ShapeDtypeStruct((rows * HEADS, HEAD_DIM), F32)] * 2
        + [jax.ShapeDtypeStruct((rows, WIDTH), BF16)] * 2,
        compiler_params=pltpu.CompilerParams(dimension_semantics=("arbitrary",),
                                             vmem_limit_bytes=VMEM_LIMIT),
        name="memory_kv",
    )(mem2d, mem_norm_g, w_mem_kv_bf)


def _beta_and_log_decay(bg, alog, dtb):
    beta = jax.nn.sigmoid(bg)
    g = -jnp.exp(alog) * _softplus(bg + dtb)
    return beta, g


def _spatial_weights(ws_ref):
    tril = _iota2((MLP_CHUNK, MLP_CHUNK), 0) >= _iota2((MLP_CHUNK, MLP_CHUNK), 1)
    return [jnp.where(tril, ws_ref[g], 0.0).astype(BF16) for g in range(MLP_GROUPS)]


def _layernorm(v, g, b):
    mu = jnp.mean(v, axis=-1, keepdims=True)
    vc = v - mu
    return vc * lax.rsqrt(jnp.mean(vc * vc, axis=-1, keepdims=True) + EPS) * g + b


def _rest(w_ref, lo, hi):
    return w_ref[:, CONV_CH + lo:CONV_CH + hi]


def _merge_and_project(x, hb, ya, yb, yc, w_ref, bgate_ref, wbra_ref, wbrb_ref, wbrc_ref, wout_ref,
                       fng_ref):
    merged = None
    for i, (yi, wbr) in enumerate(((ya, wbra_ref), (yb, wbrb_ref), (yc, wbrc_ref))):
        lo = i * D_MODEL
        gate = jax.nn.sigmoid(_dot(hb, _rest(w_ref, REST_MG + lo, REST_MG + lo + D_MODEL))
                              + bgate_ref[:, lo:lo + D_MODEL])
        term = gate * _dot(yi, wbr[...])
        merged = term if merged is None else merged + term
    out = x + _dot(merged, wout_ref[...])
    return _rms(out) * fng_ref[...]


def _chunk_cumsum(g, ck, scr):
    n = g.shape[0]
    pad = ck // 2
    pos = _iota2(g.shape, 0) % ck
    scr[0:pad, :] = jnp.zeros((pad, LANES), F32)
    d = 1
    while d < ck:
        scr[pad:pad + n, :] = g
        g = g + jnp.where(pos >= d, scr[pad - d:pad - d + n, :], 0.0)
        d *= 2
    return g


def _delta_pair_terms(q, k, v, beta_col, gc_col, gc_row):
    n = q.shape[0]
    ck = n // 2
    ri = _iota2((n, n), 0)
    ci = _iota2((n, n), 1)
    same = (ri < ck) == (ci < ck)
    kt = k.T
    decay = jnp.exp(jnp.where(same & (ri >= ci), gc_col - gc_row, -1e30))
    kb = k * beta_col
    egc = jnp.exp(gc_col)
    low = jnp.where(same & (ri > ci), _dot(kb, kt) * decay, 0.0)
    last_a = gc_row[:, ck - 1:ck]
    last_b = gc_row[:, n - 1:n]
    in_a = _iota2((1, n), 1) < ck
    kdt = kt * jnp.exp(jnp.where(in_a, last_a, last_b) - gc_row)
    return dict(
        neg_low=-(low[:ck] + low[ck:]),
        attn=jnp.where(same & (ri >= ci), _dot(q, kt) * decay, 0.0),
        rhs=jnp.concatenate([v * beta_col, kb * egc], axis=1),
        qd=q * egc,
        kdt_a=kdt[:, :ck],
        kdt_b=jnp.where(in_a, 0.0, kdt),
        gl=(jnp.exp(last_a), jnp.exp(last_b)),
    )


def _block_diag2(packed):
    n = packed.shape[0]
    left = _iota2(packed.shape, 1) < n
    return jnp.concatenate([jnp.where(left, packed, 0.0), jnp.where(left, 0.0, packed)],
                           axis=0).astype(BF16)


def _prompt_kernel(x_ref, mk_ref, mv_ref, w_ref, wbg_ref, normg_ref, convw_ref,
                   alog_ref, dtb_ref, anormg_ref, lng_ref, lnb_ref, ws_ref, bs_ref, wbra_ref, wbrb_ref,
                   wbrc_ref, bgate_ref, wout_ref, fng_ref,
                   y_ref, sd_ref, sc_ref,
                   convbuf, state, cum_scr):
    tb = PROMPT_BLOCK
    ck = DELTA_CHUNK
    nck = tb // ck
    t = pl.program_id(1)

    @pl.when(t == 0)
    def _():
        convbuf[:, 0:SUBLANES, :] = jnp.zeros((CONV_CH // LANES, SUBLANES, LANES), F32)
        state[...] = jnp.zeros_like(state)

    hb = (_rms(x_ref[0]) * normg_ref[...]).astype(BF16)

    pre = _dot(hb, w_ref[:, :CONV_CH])
    bg_col = _dot(hb, wbg_ref[...])
    pb = _dot(hb, _rest(w_ref, REST_B, REST_B + 3 * WIDTH))
    pc = _dot(hb, _rest(w_ref, REST_C, REST_C + 2 * WIDTH))
    agate = _dot(hb, _rest(w_ref, REST_AGATE, REST_AGATE + WIDTH))

    conv_tiles, tails = [], []
    for ct in range(CONV_CH // LANES):
        cols = slice(ct * LANES, (ct + 1) * LANES)
        convbuf[ct, SUBLANES:SUBLANES + tb, :] = pre[:, cols]
        acc = pre[:, cols] * convw_ref[CONV_WIDTH - 1:CONV_WIDTH, cols]
        for j in range(CONV_WIDTH - 1):
            shift = CONV_WIDTH - 1 - j
            acc = acc + convbuf[ct, SUBLANES - shift:SUBLANES - shift + tb, :] * convw_ref[j:j + 1, cols]
        conv_tiles.append(acc)
        tails.append(convbuf[ct, tb:tb + SUBLANES, :])
        convbuf[ct, 0:SUBLANES, :] = tails[ct]
    sc_ref[0] = jnp.concatenate(tails, axis=1)[SUBLANES - (CONV_WIDTH - 1):, :]
    qkv = jax.nn.silu(jnp.concatenate(conv_tiles, axis=1))

    beta_col, g_col = _beta_and_log_decay(bg_col, alog_ref[...], dtb_ref[...])
    n2 = 2 * ck
    assert nck % 2 == 0 and n2 == LANES
    npair = nck // 2
    gc_col = _chunk_cumsum(g_col, ck, cum_scr)
    gc_cols = [gc_col[j * n2:(j + 1) * n2, :] for j in range(npair)]
    gc_rows = [gc.T[:SUBLANES, :] for gc in gc_cols]

    groups = [(j, hd) for j in range(npair) for hd in range(HEADS)]
    qn, kn, vs = [], [], []
    for hd in range(HEADS):
        lo = hd * HEAD_DIM
        qn.append(_l2norm(qkv[:, lo:lo + HEAD_DIM], HEAD_DIM ** -0.5))
        kn.append(_l2norm(qkv[:, WIDTH + lo:WIDTH + lo + HEAD_DIM]))
        vs.append(qkv[:, 2 * WIDTH + lo:2 * WIDTH + lo + HEAD_DIM])
    terms = {}
    for j, hd in groups:
        rows = slice(j * n2, (j + 1) * n2)
        terms[j, hd] = _delta_pair_terms(
            qn[hd][rows], kn[hd][rows], vs[hd][rows], beta_col[rows, hd:hd + 1],
            gc_cols[j][:, HEADS + hd:HEADS + hd + 1], gc_rows[j][HEADS + hd:HEADS + hd + 1, :])

    eye2 = (_iota2((ck, n2), 0) == _iota2((ck, n2), 1) % ck).astype(F32)
    powers = [terms[g]["neg_low"] for g in groups]
    invs = [eye2 + p for p in powers]
    powers = [_dot(p, _block_diag2(p)) for p in powers]

    def inverse_level(last=False):
        nonlocal powers, invs
        if last:
            invs = [inv + _dot(inv, _block_diag2(p)) for inv, p in zip(invs, powers)]
            return
        prods = [_dot(jnp.concatenate([inv, p], axis=0), _block_diag2(p)) for inv, p in zip(invs, powers)]
        invs = [inv + r[:ck] for inv, r in zip(invs, prods)]
        powers = [r[ck:] for r in prods]

    levels = 0
    while 2 ** (levels + 1) < ck:
        levels += 1
    assert levels == 5, "the side work below is placed for five squaring levels"

    n_slots = 2 * nck
    gate_cols = 2 * D_MODEL // nck
    n_gate = N_BRANCH * D_MODEL // gate_cols
    out_parts = 4
    out_cols = D_MODEL // out_parts
    n_early = n_gate + 2 * out_parts - n_slots
    assert 0 <= n_early <= levels - 1 and D_MODEL % gate_cols == 0
    gate_parts = []
    out_b, out_c = [], []

    def gate_piece(j):
        lo = j * gate_cols
        gate_parts.append(jax.nn.sigmoid(_dot(hb, _rest(w_ref, REST_MG + lo, REST_MG + lo + gate_cols))
                                         + bgate_ref[:, lo:lo + gate_cols]))

    side = [lambda j=j: gate_piece(j) for j in range(n_gate)]
    side += [lambda j=j: out_b.append(_dot(yb, wbrb_ref[:, j * out_cols:(j + 1) * out_cols]))
             for j in range(out_parts)]
    side += [lambda j=j: out_c.append(_dot(yc, wbrc_ref[:, j * out_cols:(j + 1) * out_cols]))
             for j in range(out_parts)]
    early, side = side[:n_early], side[n_early:]

    def inverse_level_and_filler(i):
        inverse_level()
        if i < n_early:
            early[i]()

    scores = [_dot_nt(pc[:, hd * HEAD_DIM:(hd + 1) * HEAD_DIM], mk_ref[0, :, hd * HEAD_DIM:(hd + 1) * HEAD_DIM])
              for hd in range(HEADS)]
    inverse_level_and_filler(0)
    bu = pb[:, :WIDTH]
    vn = _layernorm(pb[:, WIDTH:2 * WIDTH], lng_ref[...], lnb_ref[...])
    ws = _spatial_weights(ws_ref)
    gw = WIDTH // MLP_GROUPS
    s_rows = []
    for n in range(tb // MLP_CHUNK):
        r0 = n * MLP_CHUNK
        s_rows.append(jnp.concatenate(
            [_dot(ws[g], vn[r0:r0 + MLP_CHUNK, g * gw:(g + 1) * gw]) + bs_ref[g]
             for g in range(MLP_GROUPS)], axis=1))
    yb = bu * jnp.concatenate(s_rows, axis=0) * jax.nn.silu(pb[:, 2 * WIDTH:])
    inverse_level_and_filler(1)
    inverse_level_and_filler(2)
    oc = []
    for hd in range(HEADS):
        p = jnp.exp2((scores[hd] - jnp.max(scores[hd], axis=-1, keepdims=True))
                     * (HEAD_DIM ** -0.5 * math.log2(math.e)))
        oc.append(_dot(p, mv_ref[0, :, hd * HEAD_DIM:(hd + 1) * HEAD_DIM])
                  / jnp.sum(p, axis=-1, keepdims=True))
    yc = jnp.concatenate(oc, axis=1) * jax.nn.silu(pc[:, WIDTH:])
    inverse_level_and_filler(3)
    inverse_level(last=True)
    uws = {g: _dot(_block_diag2(inv), terms[g]["rhs"]) for g, inv in zip(groups, invs)}

    s_heads = [state[hd] for hd in range(HEADS)]
    o_chunks = [[] for _ in range(HEADS)]
    v_new_a = [None] * HEADS
    for c in range(nck):
        j, second = divmod(c, 2)
        rows = slice(ck, n2) if second else slice(0, ck)
        ws_qs = []
        for hd in range(HEADS):
            wq = jnp.concatenate([uws[j, hd][rows, HEAD_DIM:], terms[j, hd]["qd"][rows]], axis=0)
            ws_qs.append(_dot(wq, s_heads[hd]))
        side[2 * c]()
        for hd in range(HEADS):
            tm = terms[j, hd]
            v_new = uws[j, hd][rows, :HEAD_DIM] - ws_qs[hd][:ck]
            if second:
                v_pair = jnp.concatenate([v_new_a[hd], v_new], axis=0)
                o_chunks[hd].append(ws_qs[hd][ck:] + _dot(tm["attn"][rows], v_pair))
                s_heads[hd] = s_heads[hd] * tm["gl"][1] + _dot(tm["kdt_b"], v_pair)
            else:
                v_new_a[hd] = v_new
                o_chunks[hd].append(ws_qs[hd][ck:] + _dot(tm["attn"][rows, :ck], v_new))
                s_heads[hd] = s_heads[hd] * tm["gl"][0] + _dot(tm["kdt_a"], v_new)
        side[2 * c + 1]()
    for hd in range(HEADS):
        state[hd] = s_heads[hd]
    sd_ref[0] = state[...]

    o_heads = []
    for hd in range(HEADS):
        lo = hd * HEAD_DIM
        o = jnp.concatenate(o_chunks[hd], axis=0)
        o_heads.append(_rms(o) * anormg_ref[...] * jax.nn.silu(agate[:, lo:lo + HEAD_DIM]))
    branch_out = [_dot(jnp.concatenate(o_heads, axis=1), wbra_ref[...]),
                  jnp.concatenate(out_b, axis=1), jnp.concatenate(out_c, axis=1)]
    gates = jnp.concatenate(gate_parts, axis=1)
    merged = None
    for i in range(N_BRANCH):
        term = gates[:, i * D_MODEL:(i + 1) * D_MODEL] * branch_out[i]
        merged = term if merged is None else merged + term
    out = x_ref[0] + _dot(merged, wout_ref[...])
    y_ref[0] = _rms(out) * fng_ref[...]


def _prompt_layer(x, mkb, mvb, weights, params):
    bsz, seq, _ = x.shape
    tb = PROMPT_BLOCK
    nt = seq // tb
    full = lambda a: pl.BlockSpec(a.shape, lambda b, t: (0,) * a.ndim, pipeline_mode=pl.Buffered(1))
    in_specs = [
        pl.BlockSpec((1, tb, D_MODEL), lambda b, t: (b, t, 0)),
        pl.BlockSpec((1, MEM_LEN, WIDTH), lambda b, t: (b, 0, 0)),
        pl.BlockSpec((1, MEM_LEN, WIDTH), lambda b, t: (b, 0, 0)),
    ] + [full(w) for w in weights] + [full(p) for p in params]
    out_specs = [
        pl.BlockSpec((1, tb, D_MODEL), lambda b, t: (b, t, 0)),
        pl.BlockSpec((1, HEADS, HEAD_DIM, HEAD_DIM), lambda b, t: (b, 0, 0, 0)),
        pl.BlockSpec((1, CONV_WIDTH - 1, CONV_CH), lambda b, t: (b, 0, 0)),
    ]
    out_shape = [
        jax.ShapeDtypeStruct((bsz, seq, D_MODEL), F32),
        jax.ShapeDtypeStruct((bsz, HEADS, HEAD_DIM, HEAD_DIM), F32),
        jax.ShapeDtypeStruct((bsz, CONV_WIDTH - 1, CONV_CH), F32),
    ]
    return pl.pallas_call(
        _prompt_kernel,
        grid=(bsz, nt),
        in_specs=in_specs,
        out_specs=out_specs,
        out_shape=out_shape,
        scratch_shapes=[pltpu.VMEM((CONV_CH // LANES, tb + SUBLANES, LANES), F32),
                        pltpu.VMEM((HEADS, HEAD_DIM, HEAD_DIM), F32),
                        pltpu.VMEM((DELTA_CHUNK // 2 + tb, LANES), F32)],
        compiler_params=pltpu.CompilerParams(dimension_semantics=("arbitrary", "arbitrary"),
                                             vmem_limit_bytes=VMEM_LIMIT),
        name="prompt_layer",
    )(x, mkb, mvb, *weights, *params)


def _sample_kernel(x_ref, ck_ref, cv_ref, s0_ref, cb_ref, w_ref, wbg_ref,
                   normg_ref, convw_ref, alog_ref, dtb_ref, anormg_ref, lng_ref, lnb_ref, ws_ref, bs_ref,
                   wbra_ref, wbrb_ref, wbrc_ref, bgate_ref, wout_ref, fng_ref,
                   y_ref, sd_ref, sc_ref, vn_ref,
                   hb_scr, pre_scr, rest_scr, bg_scr, ya_scr, yc_scr):
    sb = SAMPLE_BLOCK
    i = pl.program_id(0)
    nsteps = pl.num_programs(0)

    @pl.when(i == 0)
    def _():
        hb = (_rms(x_ref[...]) * normg_ref[...]).astype(BF16)
        hb_scr[...] = hb
        pre_scr[...] = _dot(hb, w_ref[:, :CONV_CH])
        rest_scr[...] = _dot(hb, _rest(w_ref, 0, REST_MG))
        bg_scr[...] = _dot(hb, wbg_ref[...])
        bu = rest_scr[:, REST_B:REST_B + WIDTH]
        vn = _layernorm(rest_scr[:, REST_B + WIDTH:REST_B + 2 * WIDTH], lng_ref[...], lnb_ref[...])
        vn_ref[...] = vn
        bgate = rest_scr[:, REST_B + 2 * WIDTH:REST_B + 3 * WIDTH]
        gw = WIDTH // MLP_GROUPS
        s = jnp.concatenate(
            [vn[:, g * gw:(g + 1) * gw] * ws_ref[g, 0:1, 0:1] + bs_ref[g, 0:1, :]
             for g in range(MLP_GROUPS)], axis=1)
        rest_scr[:, REST_B:REST_B + WIDTH] = bu * s * jax.nn.silu(bgate)

    r0 = pl.multiple_of(i * sb, sb)
    rows = pl.ds(r0, sb)
    pre = pre_scr[rows, :]
    beta, g = _beta_and_log_decay(bg_scr[rows, :], alog_ref[...], dtb_ref[...])
    decay = jnp.exp(g)
    agate = rest_scr[rows, REST_AGATE:REST_AGATE + WIDTH]
    cq = rest_scr[rows, REST_C:REST_C + WIDTH]
    cgate = rest_scr[rows, REST_C + WIDTH:REST_C + 2 * WIDTH]

    own_head = ((_iota2((SUBLANES, MEM_LEN * HEADS), 1) % HEADS)
                == (_iota2((SUBLANES, MEM_LEN * HEADS), 0) % HEADS))

    scores = []
    for s in range(sb):
        qh = jnp.concatenate([cq[s:s + 1, hd * HEAD_DIM:(hd + 1) * HEAD_DIM] for hd in range(HEADS)]
                             + [jnp.zeros((SUBLANES - HEADS, HEAD_DIM), F32)], axis=0)
        scores.append(_dot_nt(qh, ck_ref[s]) * (HEAD_DIM ** -0.5))

    carried = [cb_ref[j] for j in range(CONV_WIDTH - 1)]
    conv = pre * convw_ref[CONV_WIDTH - 1:CONV_WIDTH, :]
    for j in range(CONV_WIDTH - 1):
        conv = conv + carried[j] * convw_ref[j:j + 1, :]
    for j in range(1, CONV_WIDTH - 1):
        sc_ref[j - 1] = carried[j]
    sc_ref[CONV_WIDTH - 2] = pre
    qkv = jax.nn.silu(conv)

    row8 = _iota2((sb, HEAD_DIM), 0)
    row16 = _iota2((2 * sb, HEAD_DIM), 0)
    qs, ks, vs, kq_s = [], [], [], []
    for hd in range(HEADS):
        lo = hd * HEAD_DIM
        qs.append(_l2norm(qkv[:, lo:lo + HEAD_DIM], HEAD_DIM ** -0.5))
        ks.append(_l2norm(qkv[:, WIDTH + lo:WIDTH + lo + HEAD_DIM]))
        vs.append(qkv[:, 2 * WIDTH + lo:2 * WIDTH + lo + HEAD_DIM])
        kq = jnp.concatenate([ks[hd], qs[hd]], axis=0).astype(BF16)
        kq_s.append([_dot(kq, s0_ref[s, hd]) for s in range(sb)])
    o_heads = []
    for hd in range(HEADS):
        lo = hd * HEAD_DIM
        sk = jnp.zeros((sb, HEAD_DIM), F32)
        sq = jnp.zeros((sb, HEAD_DIM), F32)
        for s in range(sb):
            sk = jnp.where(row8 == s, kq_s[hd][s][:sb], sk)
            sq = jnp.where(row8 == s, kq_s[hd][s][sb:], sq)
        a = decay[:, HEADS + hd:HEADS + hd + 1]
        v_new = beta[:, hd:hd + 1] * (vs[hd] - a * sk)
        qk = jnp.sum(qs[hd] * ks[hd], axis=-1, keepdims=True)
        o = a * sq + qk * v_new
        o_heads.append(_rms(o) * anormg_ref[...] * jax.nn.silu(agate[:, lo:lo + HEAD_DIM]))
        kt = jnp.concatenate([ks[hd], jnp.zeros((HEAD_DIM - sb, HEAD_DIM), F32)], axis=0).T.astype(BF16)
        v_pad = jnp.concatenate([v_new, jnp.zeros((sb, HEAD_DIM), F32)], axis=0)
        zeros = jnp.zeros((HEAD_DIM - 2 * sb, HEAD_DIM), BF16)
        for s in range(sb):
            only_s = jnp.concatenate([jnp.where(row16 == s, v_pad, 0.0).astype(BF16), zeros], axis=0)
            sd_ref[s, hd] = a[s:s + 1, :] * s0_ref[s, hd] + _dot(kt, only_s)
    ya_scr[rows, :] = jnp.concatenate(o_heads, axis=1)

    probs, sums = [], []
    for s in range(sb):
        sc = jnp.where(own_head, scores[s], -1e30)
        sc = sc - jnp.max(sc, axis=-1, keepdims=True)
        probs.append(jnp.where(own_head, jnp.exp(sc), 0.0))
        sums.append(jnp.sum(probs[s], axis=-1, keepdims=True))
    yc_rows = []
    for s in range(sb):
        oc = _dot(probs[s], cv_ref[s]) / sums[s]
        oc = jnp.concatenate([oc[hd:hd + 1, :] for hd in range(HEADS)], axis=1)
        yc_rows.append(oc * jax.nn.silu(cgate[s:s + 1, :]))
    yc_scr[rows, :] = jnp.concatenate(yc_rows, axis=0)

    @pl.when(i == nsteps - 1)
    def _():
        y_ref[...] = _merge_and_project(x_ref[...], hb_scr[...], ya_scr[...],
                                        rest_scr[:, REST_B:REST_B + WIDTH], yc_scr[...], w_ref,
                                        bgate_ref, wbra_ref, wbrb_ref, wbrc_ref, wout_ref, fng_ref)


def _sample_layer(x, cache_k, cache_v, s0, cb, weights, params):
    n = x.shape[0]
    sb = SAMPLE_BLOCK
    full = lambda a: pl.BlockSpec(a.shape, lambda i: (0,) * a.ndim, pipeline_mode=pl.Buffered(1))
    in_specs = [
        full(x),
        pl.BlockSpec((sb, MEM_LEN * HEADS, HEAD_DIM), lambda i: (i, 0, 0)),
        pl.BlockSpec((sb, MEM_LEN * HEADS, HEAD_DIM), lambda i: (i, 0, 0)),
        pl.BlockSpec((sb, HEADS, HEAD_DIM, HEAD_DIM), lambda i: (i, 0, 0, 0)),
        pl.BlockSpec((CONV_WIDTH - 1, sb, CONV_CH), lambda i: (0, i, 0)),
    ] + [full(w) for w in weights] + [full(p) for p in params]
    out_specs = [
        pl.BlockSpec((n, D_MODEL), lambda i: (0, 0)),
        pl.BlockSpec((sb, HEADS, HEAD_DIM, HEAD_DIM), lambda i: (i, 0, 0, 0)),
        pl.BlockSpec((CONV_WIDTH - 1, sb, CONV_CH), lambda i: (0, i, 0)),
        pl.BlockSpec((n, WIDTH), lambda i: (0, 0)),
    ]
    out_shape = [
        jax.ShapeDtypeStruct((n, D_MODEL), F32),
        jax.ShapeDtypeStruct((n, HEADS, HEAD_DIM, HEAD_DIM), F32),
        jax.ShapeDtypeStruct((CONV_WIDTH - 1, n, CONV_CH), F32),
        jax.ShapeDtypeStruct((n, WIDTH), F32),
    ]
    return pl.pallas_call(
        _sample_kernel,
        grid=(n // sb,),
        in_specs=in_specs,
        out_specs=out_specs,
        out_shape=out_shape,
        scratch_shapes=[pltpu.VMEM((n, D_MODEL), BF16),
                        pltpu.VMEM((n, CONV_CH), F32),
                        pltpu.VMEM((n, REST_MG), F32),
                        pltpu.VMEM((n, LANES), F32),
                        pltpu.VMEM((n, WIDTH), F32),
                        pltpu.VMEM((n, WIDTH), F32)],
        compiler_params=pltpu.CompilerParams(dimension_semantics=("arbitrary",),
                                             vmem_limit_bytes=VMEM_LIMIT),
        name="sample_layer",
    )(x, cache_k, cache_v, s0, cb, *weights, *params)


PACK_BLOCK = 768


def _pack_kernel(a_ref, b_ref, o_ref, bg_ref):
    j = pl.program_id(0)
    first_rest = CONV_CH // PACK_BLOCK
    a = a_ref[...]
    skipped = jnp.concatenate([a[2 * HEADS:], b_ref[...]], axis=0)
    rows = jnp.where(j < first_rest, a, skipped)
    o_ref[...] = rows.T.astype(BF16)

    @pl.when(j == first_rest)
    def _():
        logits = jnp.concatenate([a[:2 * HEADS], jnp.zeros((LANES - 2 * HEADS, D_MODEL), F32)], axis=0)
        bg_ref[...] = logits.T.astype(BF16)


def _pack_input_projection(w_t):
    assert 2 * HEADS == SUBLANES and CONV_CH % PACK_BLOCK == 0 and PACKED_COLS % PACK_BLOCK == 0
    return pl.pallas_call(
        _pack_kernel,
        grid=(PACKED_COLS // PACK_BLOCK,),
        in_specs=[pl.BlockSpec((PACK_BLOCK, D_MODEL), lambda j: (j, 0)),
                  pl.BlockSpec((SUBLANES, D_MODEL), lambda j: ((j + 1) * (PACK_BLOCK // SUBLANES), 0))],
        out_specs=[pl.BlockSpec((D_MODEL, PACK_BLOCK), lambda j: (0, j)),
                   pl.BlockSpec((D_MODEL, LANES), lambda j: (0, 0))],
        out_shape=[jax.ShapeDtypeStruct((D_MODEL, PACKED_COLS), BF16),
                   jax.ShapeDtypeStruct((D_MODEL, LANES), BF16)],
        compiler_params=pltpu.CompilerParams(dimension_semantics=("arbitrary",),
                                             vmem_limit_bytes=VMEM_LIMIT),
        name="pack_input_projection",
    )(w_t, w_t)


def _lanes_4_to_7(vec):
    return jnp.zeros((1, LANES), F32).at[0, HEADS:2 * HEADS].set(vec)


def kernel(x_prompt, x_sample, cache_mem_k, cache_mem_v, state_delta, state_conv, mem_prompt, norm_g, w_in, conv_w, a_log, dt_bias, a_norm_g, ln_v_g, ln_v_b, w_spatial, b_spatial, mem_norm_g, w_mem_kv, w_br_a, w_br_b, w_br_c, b_gate, w_out, final_norm_g):
    depth = norm_g.shape[0]
    assert depth == 1, "single-layer step"
    bsz, seq, _ = x_prompt.shape
    nsmp = x_sample.shape[0]
    assert x_sample.shape[1] == 1 and seq % PROMPT_BLOCK == 0 and nsmp % SAMPLE_BLOCK == 0
    assert w_in.shape[2] == CONV_CH + 2 * HEADS + REST_COLS

    weights = _pack_input_projection(w_in[0].T)
    params = (
        norm_g[0][None, :], conv_w[0], _lanes_4_to_7(a_log[0]), _lanes_4_to_7(dt_bias[0]),
        a_norm_g[0][None, :], ln_v_g[0][None, :], ln_v_b[0][None, :], w_spatial[0],
        jnp.broadcast_to(b_spatial[0][:, :, None], (MLP_GROUPS, MLP_CHUNK, WIDTH // MLP_GROUPS)),
        w_br_a[0].astype(BF16), w_br_b[0].astype(BF16), w_br_c[0].astype(BF16),
        b_gate[0].reshape(1, N_BRANCH * D_MODEL), w_out[0].astype(BF16), final_norm_g[None, :],
    )

    mk, mv, mkb, mvb = _memory_kv(mem_prompt.reshape(bsz * MEM_LEN, D_MODEL), mem_norm_g[0][None, :],
                                  w_mem_kv[0].astype(BF16))
    y_p, sd_p, sc_p = _prompt_layer(x_prompt, mkb.reshape(bsz, MEM_LEN, WIDTH),
                                    mvb.reshape(bsz, MEM_LEN, WIDTH), weights, params)
    y_s, sd_s, sc_s, vn_s = _sample_layer(
        x_sample.reshape(nsmp, D_MODEL), cache_mem_k.reshape(nsmp, MEM_LEN * HEADS, HEAD_DIM),
        cache_mem_v.reshape(nsmp, MEM_LEN * HEADS, HEAD_DIM), state_delta.reshape(state_delta.shape[1:]),
        jnp.transpose(state_conv[0], (1, 0, 2)), weights, params)

    kv_shape = (1, bsz, MEM_LEN, HEADS, HEAD_DIM)
    return (y_p, y_s.reshape(nsmp, 1, D_MODEL), sd_p[None], sc_p[None], mk.reshape(kv_shape),
            mv.reshape(kv_shape), sd_s[None], jnp.transpose(sc_s, (1, 0, 2))[None],
            vn_s.reshape(1, nsmp, 1, WIDTH))
```

```python
import math

import jax
import jax.numpy as jnp
from jax import lax
from jax.experimental import pallas as pl
from jax.experimental.pallas import tpu as pltpu

F32 = jnp.float32
BF16 = jnp.bfloat16

D_MODEL = 1024
HEADS = 4
HEAD_DIM = 128
WIDTH = HEADS * HEAD_DIM
CONV_WIDTH = 4
CONV_CH = 3 * WIDTH
MLP_GROUPS = 4
MLP_CHUNK = 128
MEM_LEN = 256
N_BRANCH = 3
EPS = 1e-6

LANES = 128
SUBLANES = 8

REST_AGATE = 0
REST_B = REST_AGATE + WIDTH
REST_C = REST_B + 3 * WIDTH
REST_MG = REST_C + 2 * WIDTH
REST_COLS = REST_MG + N_BRANCH * D_MODEL
PACKED_COLS = CONV_CH + REST_COLS

PROMPT_BLOCK = 512
DELTA_CHUNK = 64
SAMPLE_BLOCK = 8
VMEM_LIMIT = 56 * 1024 * 1024


def _dot(a, b):
    return jnp.dot(a.astype(BF16), b.astype(BF16), preferred_element_type=F32)


def _dot_nt(a, b):
    return lax.dot_general(a.astype(BF16), b.astype(BF16), (((1,), (1,)), ((), ())),
                           preferred_element_type=F32)


def _rms(x):
    return x * lax.rsqrt(jnp.mean(x * x, axis=-1, keepdims=True) + EPS)


def _l2norm(x, scale=1.0):
    return x * (lax.rsqrt(jnp.sum(x * x, axis=-1, keepdims=True) + EPS) * scale)


def _softplus(x):
    return jnp.maximum(x, 0.0) + jnp.log1p(jnp.exp(-jnp.abs(x)))


def _iota2(shape, dim):
    return lax.broadcasted_iota(jnp.int32, shape, dim)


def _memkv_kernel(mem_ref, g_ref, w_ref, k_ref, v_ref, kb_ref, vb_ref):
    xn = _rms(mem_ref[...]) * g_ref[...]
    kv = _dot(xn, w_ref[...])
    k = kv[:, :WIDTH]
    v = kv[:, WIDTH:]
    npos = k.shape[0]
    for hd in range(HEADS):
        k_ref[pl.ds(hd, npos, stride=HEADS), :] = k[:, hd * HEAD_DIM:(hd + 1) * HEAD_DIM]
        v_ref[pl.ds(hd, npos, stride=HEADS), :] = v[:, hd * HEAD_DIM:(hd + 1) * HEAD_DIM]
    kb_ref[...] = k.astype(BF16)
    vb_ref[...] = v.astype(BF16)


def _memory_kv(mem2d, mem_norm_g, w_mem_kv_bf):
    rows = mem2d.shape[0]
    blk = 512
    full = lambda shape: pl.BlockSpec(shape, lambda i: (0,) * len(shape))
    row_spec = lambda width: pl.BlockSpec((blk, width), lambda i: (i, 0))
    by_head = pl.BlockSpec((blk * HEADS, HEAD_DIM), lambda i: (i, 0))
    return pl.pallas_call(
        _memkv_kernel,
        grid=(rows // blk,),
        in_specs=[row_spec(D_MODEL), full((1, D_MODEL)), full((D_MODEL, 2 * WIDTH))],
        out_specs=[by_head] * 2 + [row_spec(WIDTH)] * 2,
        out_shape=[jax.ShapeDtypeStruct((rows * HEADS, HEAD_DIM), F32)] * 2
        + [jax.ShapeDtypeStruct((rows, WIDTH), BF16)] * 2,
        compiler_params=pltpu.CompilerParams(dimension_semantics=("arbitrary",),
                                             vmem_limit_bytes=VMEM_LIMIT),
        name="memory_kv",
    )(mem2d, mem_norm_g, w_mem_kv_bf)


def _beta_and_log_decay(bg, alog, dtb):
    beta = jax.nn.sigmoid(bg)
    g = -jnp.exp(alog) * _softplus(bg + dtb)
    return beta, g


def _spatial_weights(ws_ref):
    tril = _iota2((MLP_CHUNK, MLP_CHUNK), 0) >= _iota2((MLP_CHUNK, MLP_CHUNK), 1)
    return [jnp.where(tril, ws_ref[g], 0.0).astype(BF16) for g in range(MLP_GROUPS)]


def _layernorm(v, g, b):
    mu = jnp.mean(v, axis=-1, keepdims=True)
    vc = v - mu
    return vc * lax.rsqrt(jnp.mean(vc * vc, axis=-1, keepdims=True) + EPS) * g + b


def _rest(w_ref, lo, hi):
    return w_ref[:, CONV_CH + lo:CONV_CH + hi]


def _merge_and_project(x, hb, ya, yb, yc, w_ref, bgate_ref, wbra_ref, wbrb_ref, wbrc_ref, wout_ref,
                       fng_ref):
    merged = None
    for i, (yi, wbr) in enumerate(((ya, wbra_ref), (yb, wbrb_ref), (yc, wbrc_ref))):
        lo = i * D_MODEL
        gate = jax.nn.sigmoid(_dot(hb, _rest(w_ref, REST_MG + lo, REST_MG + lo + D_MODEL))
                              + bgate_ref[:, lo:lo + D_MODEL])
        term = gate * _dot(yi, wbr[...])
        merged = term if merged is None else merged + term
    out = x + _dot(merged, wout_ref[...])
    return _rms(out) * fng_ref[...]


def _chunk_cumsum(g, ck, scr):
    n = g.shape[0]
    pad = ck // 2
    pos = _iota2(g.shape, 0) % ck
    scr[0:pad, :] = jnp.zeros((pad, LANES), F32)
    d = 1
    while d < ck:
        scr[pad:pad + n, :] = g
        g = g + jnp.where(pos >= d, scr[pad - d:pad - d + n, :], 0.0)
        d *= 2
    return g


def _delta_chunk_terms(q, k, v, beta_col, gc_col, gc_row):
    n = q.shape[0]
    half = n // 2
    ri = _iota2((n, n), 0)
    ci = _iota2((n, n), 1)
    same = (ri < half) == (ci < half)
    kt = k.T
    decay = jnp.exp(jnp.where(ri >= ci, gc_col - gc_row, -1e30))
    kb = k * beta_col
    egc = jnp.exp(gc_col)
    low = jnp.where(ri > ci, _dot(kb, kt) * decay, 0.0)
    diag = jnp.where(same, low, 0.0)
    last = gc_row[:, n - 1:n]
    return dict(
        neg_low=-(diag[:half] + diag[half:]),
        off=jnp.where(same, 0.0, low),
        attn=jnp.where(ri >= ci, _dot(q, kt) * decay, 0.0),
        rhs=jnp.concatenate([v * beta_col, kb * egc], axis=1),
        qd=q * egc,
        kdt=kt * jnp.exp(last - gc_row),
        gl=jnp.exp(last),
    )


def _block_diag2(packed, dtype=BF16):
    n = packed.shape[0]
    left = _iota2(packed.shape, 1) < n
    return jnp.concatenate([jnp.where(left, packed, 0.0), jnp.where(left, 0.0, packed)],
                           axis=0).astype(dtype)


def _prompt_kernel(x_ref, mk_ref, mv_ref, w_ref, wbg_ref, normg_ref, convw_ref,
                   alog_ref, dtb_ref, anormg_ref, lng_ref, lnb_ref, ws_ref, bs_ref, wbra_ref, wbrb_ref,
                   wbrc_ref, bgate_ref, wout_ref, fng_ref,
                   y_ref, sd_ref, sc_ref,
                   convbuf, state, cum_scr):
    tb = PROMPT_BLOCK
    ck = DELTA_CHUNK
    nck = tb // ck
    t = pl.program_id(1)

    @pl.when(t == 0)
    def _():
        convbuf[:, 0:SUBLANES, :] = jnp.zeros((CONV_CH // LANES, SUBLANES, LANES), F32)
        state[...] = jnp.zeros_like(state)

    hb = (_rms(x_ref[0]) * normg_ref[...]).astype(BF16)

    pre = _dot(hb, w_ref[:, :CONV_CH])
    bg_col = _dot(hb, wbg_ref[...])
    pb = _dot(hb, _rest(w_ref, REST_B, REST_B + 3 * WIDTH))
    pc = _dot(hb, _rest(w_ref, REST_C, REST_C + 2 * WIDTH))
    agate = _dot(hb, _rest(w_ref, REST_AGATE, REST_AGATE + WIDTH))

    conv_tiles, tails = [], []
    for ct in range(CONV_CH // LANES):
        cols = slice(ct * LANES, (ct + 1) * LANES)
        convbuf[ct, SUBLANES:SUBLANES + tb, :] = pre[:, cols]
        acc = pre[:, cols] * convw_ref[CONV_WIDTH - 1:CONV_WIDTH, cols]
        for j in range(CONV_WIDTH - 1):
            shift = CONV_WIDTH - 1 - j
            acc = acc + convbuf[ct, SUBLANES - shift:SUBLANES - shift + tb, :] * convw_ref[j:j + 1, cols]
        conv_tiles.append(acc)
        tails.append(convbuf[ct, tb:tb + SUBLANES, :])
        convbuf[ct, 0:SUBLANES, :] = tails[ct]
    sc_ref[0] = jnp.concatenate(tails, axis=1)[SUBLANES - (CONV_WIDTH - 1):, :]
    qkv = jax.nn.silu(jnp.concatenate(conv_tiles, axis=1))

    beta_col, g_col = _beta_and_log_decay(bg_col, alog_ref[...], dtb_ref[...])
    n2 = 2 * ck
    assert nck % 2 == 0 and n2 == LANES
    npair = nck // 2
    gc_col = _chunk_cumsum(g_col, n2, cum_scr)
    gc_cols = [gc_col[j * n2:(j + 1) * n2, :] for j in range(npair)]
    gc_rows = [gc.T[:SUBLANES, :] for gc in gc_cols]

    groups = [(j, hd) for j in range(npair) for hd in range(HEADS)]
    qn, kn, vs = [], [], []
    for hd in range(HEADS):
        lo = hd * HEAD_DIM
        qn.append(_l2norm(qkv[:, lo:lo + HEAD_DIM], HEAD_DIM ** -0.5))
        kn.append(_l2norm(qkv[:, WIDTH + lo:WIDTH + lo + HEAD_DIM]))
        vs.append(qkv[:, 2 * WIDTH + lo:2 * WIDTH + lo + HEAD_DIM])
    terms = {}
    for j, hd in groups:
        rows = slice(j * n2, (j + 1) * n2)
        terms[j, hd] = _delta_chunk_terms(
            qn[hd][rows], kn[hd][rows], vs[hd][rows], beta_col[rows, hd:hd + 1],
            gc_cols[j][:, HEADS + hd:HEADS + hd + 1], gc_rows[j][HEADS + hd:HEADS + hd + 1, :])

    eye2 = (_iota2((ck, n2), 0) == _iota2((ck, n2), 1) % ck).astype(F32)
    powers = [terms[g]["neg_low"] for g in groups]
    invs = [eye2 + p for p in powers]
    powers = [_dot(p, _block_diag2(p)) for p in powers]

    def inverse_level(last=False):
        nonlocal powers, invs
        if last:
            invs = [inv + _dot(inv, _block_diag2(p)) for inv, p in zip(invs, powers)]
            return
        prods = [_dot(jnp.concatenate([inv, p], axis=0), _block_diag2(p)) for inv, p in zip(invs, powers)]
        invs = [inv + r[:ck] for inv, r in zip(invs, prods)]
        powers = [r[ck:] for r in prods]

    levels = 0
    while 2 ** (levels + 1) < ck:
        levels += 1
    assert levels == 5, "the side work below is placed for five squaring levels"

    n_slots = 2 * nck
    gate_cols = 2 * D_MODEL // nck
    n_gate = N_BRANCH * D_MODEL // gate_cols
    out_parts = 4
    out_cols = D_MODEL // out_parts
    n_early = n_gate + 2 * out_parts - n_slots
    assert 0 <= n_early <= levels and D_MODEL % gate_cols == 0
    gate_parts = []
    out_b, out_c = [], []

    def gate_piece(j):
        lo = j * gate_cols
        gate_parts.append(jax.nn.sigmoid(_dot(hb, _rest(w_ref, REST_MG + lo, REST_MG + lo + gate_cols))
                                         + bgate_ref[:, lo:lo + gate_cols]))

    side = [lambda j=j: gate_piece(j) for j in range(n_gate)]
    side += [lambda j=j: out_b.append(_dot(yb, wbrb_ref[:, j * out_cols:(j + 1) * out_cols]))
             for j in range(out_parts)]
    side += [lambda j=j: out_c.append(_dot(yc, wbrc_ref[:, j * out_cols:(j + 1) * out_cols]))
             for j in range(out_parts)]
    early, side = side[:n_early], side[n_early:]

    def inverse_level_and_filler(last=False):
        inverse_level(last)
        if early:
            early.pop(0)()

    scores = [_dot_nt(pc[:, hd * HEAD_DIM:(hd + 1) * HEAD_DIM], mk_ref[0, :, hd * HEAD_DIM:(hd + 1) * HEAD_DIM])
              for hd in range(HEADS)]
    inverse_level_and_filler()
    bu = pb[:, :WIDTH]
    vn = _layernorm(pb[:, WIDTH:2 * WIDTH], lng_ref[...], lnb_ref[...])
    ws = _spatial_weights(ws_ref)
    gw = WIDTH // MLP_GROUPS
    s_rows = []
    for n in range(tb // MLP_CHUNK):
        r0 = n * MLP_CHUNK
        s_rows.append(jnp.concatenate(
            [_dot(ws[g], vn[r0:r0 + MLP_CHUNK, g * gw:(g + 1) * gw]) + bs_ref[g]
             for g in range(MLP_GROUPS)], axis=1))
    yb = bu * jnp.concatenate(s_rows, axis=0) * jax.nn.silu(pb[:, 2 * WIDTH:])
    inverse_level_and_filler()
    inverse_level_and_filler()
    oc = []
    for hd in range(HEADS):
        p = jnp.exp2((scores[hd] - jnp.max(scores[hd], axis=-1, keepdims=True))
                     * (HEAD_DIM ** -0.5 * math.log2(math.e)))
        oc.append(_dot(p, mv_ref[0, :, hd * HEAD_DIM:(hd + 1) * HEAD_DIM])
                  / jnp.sum(p, axis=-1, keepdims=True))
    yc = jnp.concatenate(oc, axis=1) * jax.nn.silu(pc[:, WIDTH:])
    inverse_level_and_filler()
    inverse_level_and_filler(last=True)
    bds = [_block_diag2(inv, F32) for inv in invs]
    crosses = [_dot(bd, terms[g]["off"]) for g, bd in zip(groups, bds)]
    fulls = [bd - _dot(cr, bd) for bd, cr in zip(bds, crosses)]
    uws = {g: _dot(full, terms[g]["rhs"]) for g, full in zip(groups, fulls)}

    s_heads = [state[hd] for hd in range(HEADS)]
    o_chunks = [[] for _ in range(HEADS)]
    per_stage = len(side) // (2 * npair)
    assert per_stage * 2 * npair == len(side)
    for j in range(npair):
        ws_qs = []
        for hd in range(HEADS):
            wq = jnp.concatenate([uws[j, hd][:, HEAD_DIM:], terms[j, hd]["qd"]], axis=0)
            ws_qs.append(_dot(wq, s_heads[hd]))
        for _ in range(per_stage):
            side.pop(0)()
        for hd in range(HEADS):
            tm = terms[j, hd]
            v_new = uws[j, hd][:, :HEAD_DIM] - ws_qs[hd][:n2]
            o_chunks[hd].append(ws_qs[hd][n2:] + _dot(tm["attn"], v_new))
            s_heads[hd] = s_heads[hd] * tm["gl"] + _dot(tm["kdt"], v_new)
        for _ in range(per_stage):
            side.pop(0)()
    for hd in range(HEADS):
        state[hd] = s_heads[hd]
    sd_ref[0] = state[...]

    o_heads = []
    for hd in range(HEADS):
        lo = hd * HEAD_DIM
        o = jnp.concatenate(o_chunks[hd], axis=0)
        o_heads.append(_rms(o) * anormg_ref[...] * jax.nn.silu(agate[:, lo:lo + HEAD_DIM]))
    branch_out = [_dot(jnp.concatenate(o_heads, axis=1), wbra_ref[...]),
                  jnp.concatenate(out_b, axis=1), jnp.concatenate(out_c, axis=1)]
    gates = jnp.concatenate(gate_parts, axis=1)
    merged = None
    for i in range(N_BRANCH):
        term = gates[:, i * D_MODEL:(i + 1) * D_MODEL] * branch_out[i]
        merged = term if merged is None else merged + term
    out = x_ref[0] + _dot(merged, wout_ref[...])
    y_ref[0] = _rms(out) * fng_ref[...]


def _prompt_layer(x, mkb, mvb, weights, params):
    bsz, seq, _ = x.shape
    tb = PROMPT_BLOCK
    nt = seq // tb
    full = lambda a: pl.BlockSpec(a.shape, lambda b, t: (0,) * a.ndim, pipeline_mode=pl.Buffered(1))
    in_specs = [
        pl.BlockSpec((1, tb, D_MODEL), lambda b, t: (b, t, 0)),
        pl.BlockSpec((1, MEM_LEN, WIDTH), lambda b, t: (b, 0, 0)),
        pl.BlockSpec((1, MEM_LEN, WIDTH), lambda b, t: (b, 0, 0)),
    ] + [full(w) for w in weights] + [full(p) for p in params]
    out_specs = [
        pl.BlockSpec((1, tb, D_MODEL), lambda b, t: (b, t, 0)),
        pl.BlockSpec((1, HEADS, HEAD_DIM, HEAD_DIM), lambda b, t: (b, 0, 0, 0)),
        pl.BlockSpec((1, CONV_WIDTH - 1, CONV_CH), lambda b, t: (b, 0, 0)),
    ]
    out_shape = [
        jax.ShapeDtypeStruct((bsz, seq, D_MODEL), F32),
        jax.ShapeDtypeStruct((bsz, HEADS, HEAD_DIM, HEAD_DIM), F32),
        jax.ShapeDtypeStruct((bsz, CONV_WIDTH - 1, CONV_CH), F32),
    ]
    return pl.pallas_call(
        _prompt_kernel,
        grid=(bsz, nt),
        in_specs=in_specs,
        out_specs=out_specs,
        out_shape=out_shape,
        scratch_shapes=[pltpu.VMEM((CONV_CH // LANES, tb + SUBLANES, LANES), F32),
                        pltpu.VMEM((HEADS, HEAD_DIM, HEAD_DIM), F32),
                        pltpu.VMEM((DELTA_CHUNK + tb, LANES), F32)],
        compiler_params=pltpu.CompilerParams(dimension_semantics=("arbitrary", "arbitrary"),
                                             vmem_limit_bytes=VMEM_LIMIT),
        name="prompt_layer",
    )(x, mkb, mvb, *weights, *params)


def _sample_kernel(x_ref, ck_ref, cv_ref, s0_ref, cb_ref, w_ref, wbg_ref,
                   normg_ref, convw_ref, alog_ref, dtb_ref, anormg_ref, lng_ref, lnb_ref, ws_ref, bs_ref,
                   wbra_ref, wbrb_ref, wbrc_ref, bgate_ref, wout_ref, fng_ref,
                   y_ref, sd_ref, sc_ref, vn_ref,
                   hb_scr, pre_scr, rest_scr, bg_scr, ya_scr, yc_scr):
    sb = SAMPLE_BLOCK
    i = pl.program_id(0)
    nsteps = pl.num_programs(0)

    @pl.when(i == 0)
    def _():
        hb = (_rms(x_ref[...]) * normg_ref[...]).astype(BF16)
        hb_scr[...] = hb
        pre_scr[...] = _dot(hb, w_ref[:, :CONV_CH])
        rest_scr[...] = _dot(hb, _rest(w_ref, 0, REST_MG))
        bg_scr[...] = _dot(hb, wbg_ref[...])
        bu = rest_scr[:, REST_B:REST_B + WIDTH]
        vn = _layernorm(rest_scr[:, REST_B + WIDTH:REST_B + 2 * WIDTH], lng_ref[...], lnb_ref[...])
        vn_ref[...] = vn
        bgate = rest_scr[:, REST_B + 2 * WIDTH:REST_B + 3 * WIDTH]
        gw = WIDTH // MLP_GROUPS
        s = jnp.concatenate(
            [vn[:, g * gw:(g + 1) * gw] * ws_ref[g, 0:1, 0:1] + bs_ref[g, 0:1, :]
             for g in range(MLP_GROUPS)], axis=1)
        rest_scr[:, REST_B:REST_B + WIDTH] = bu * s * jax.nn.silu(bgate)

    r0 = pl.multiple_of(i * sb, sb)
    rows = pl.ds(r0, sb)
    pre = pre_scr[rows, :]
    beta, g = _beta_and_log_decay(bg_scr[rows, :], alog_ref[...], dtb_ref[...])
    decay = jnp.exp(g)
    agate = rest_scr[rows, REST_AGATE:REST_AGATE + WIDTH]
    cq = rest_scr[rows, REST_C:REST_C + WIDTH]
    cgate = rest_scr[rows, REST_C + WIDTH:REST_C + 2 * WIDTH]

    own_head = ((_iota2((SUBLANES, MEM_LEN * HEADS), 1) % HEADS)
                == (_iota2((SUBLANES, MEM_LEN * HEADS), 0) % HEADS))

    scores = []
    for s in range(sb):
        qh = jnp.concatenate([cq[s:s + 1, hd * HEAD_DIM:(hd + 1) * HEAD_DIM] for hd in range(HEADS)]
                             + [jnp.zeros((SUBLANES - HEADS, HEAD_DIM), F32)], axis=0)
        scores.append(_dot_nt(qh, ck_ref[s]) * (HEAD_DIM ** -0.5))

    carried = [cb_ref[j] for j in range(CONV_WIDTH - 1)]
    conv = pre * convw_ref[CONV_WIDTH - 1:CONV_WIDTH, :]
    for j in range(CONV_WIDTH - 1):
        conv = conv + carried[j] * convw_ref[j:j + 1, :]
    for j in range(1, CONV_WIDTH - 1):
        sc_ref[j - 1] = carried[j]
    sc_ref[CONV_WIDTH - 2] = pre
    qkv = jax.nn.silu(conv)

    row8 = _iota2((sb, HEAD_DIM), 0)
    row16 = _iota2((2 * sb, HEAD_DIM), 0)
    qs, ks, vs, kq_s = [], [], [], []
    for hd in range(HEADS):
        lo = hd * HEAD_DIM
        qs.append(_l2norm(qkv[:, lo:lo + HEAD_DIM], HEAD_DIM ** -0.5))
        ks.append(_l2norm(qkv[:, WIDTH + lo:WIDTH + lo + HEAD_DIM]))
        vs.append(qkv[:, 2 * WIDTH + lo:2 * WIDTH + lo + HEAD_DIM])
        kq = jnp.concatenate([ks[hd], qs[hd]], axis=0).astype(BF16)
        kq_s.append([_dot(kq, s0_ref[s, hd]) for s in range(sb)])
    o_heads = []
    for hd in range(HEADS):
        lo = hd * HEAD_DIM
        sk = jnp.zeros((sb, HEAD_DIM), F32)
        sq = jnp.zeros((sb, HEAD_DIM), F32)
        for s in range(sb):
            sk = jnp.where(row8 == s, kq_s[hd][s][:sb], sk)
            sq = jnp.where(row8 == s, kq_s[hd][s][sb:], sq)
        a = decay[:, HEADS + hd:HEADS + hd + 1]
        v_new = beta[:, hd:hd + 1] * (vs[hd] - a * sk)
        qk = jnp.sum(qs[hd] * ks[hd], axis=-1, keepdims=True)
        o = a * sq + qk * v_new
        o_heads.append(_rms(o) * anormg_ref[...] * jax.nn.silu(agate[:, lo:lo + HEAD_DIM]))
        kt = jnp.concatenate([ks[hd], jnp.zeros((HEAD_DIM - sb, HEAD_DIM), F32)], axis=0).T.astype(BF16)
        v_pad = jnp.concatenate([v_new, jnp.zeros((sb, HEAD_DIM), F32)], axis=0)
        zeros = jnp.zeros((HEAD_DIM - 2 * sb, HEAD_DIM), BF16)
        for s in range(sb):
            only_s = jnp.concatenate([jnp.where(row16 == s, v_pad, 0.0).astype(BF16), zeros], axis=0)
            sd_ref[s, hd] = a[s:s + 1, :] * s0_ref[s, hd] + _dot(kt, only_s)
    ya_scr[rows, :] = jnp.concatenate(o_heads, axis=1)

    probs, sums = [], []
    for s in range(sb):
        sc = jnp.where(own_head, scores[s], -1e30)
        sc = sc - jnp.max(sc, axis=-1, keepdims=True)
        probs.append(jnp.where(own_head, jnp.exp(sc), 0.0))
        sums.append(jnp.sum(probs[s], axis=-1, keepdims=True))
    yc_rows = []
    for s in range(sb):
        oc = _dot(probs[s], cv_ref[s]) / sums[s]
        oc = jnp.concatenate([oc[hd:hd + 1, :] for hd in range(HEADS)], axis=1)
        yc_rows.append(oc * jax.nn.silu(cgate[s:s + 1, :]))
    yc_scr[rows, :] = jnp.concatenate(yc_rows, axis=0)

    @pl.when(i == nsteps - 1)
    def _():
        y_ref[...] = _merge_and_project(x_ref[...], hb_scr[...], ya_scr[...],
                                        rest_scr[:, REST_B:REST_B + WIDTH], yc_scr[...], w_ref,
                                        bgate_ref, wbra_ref, wbrb_ref, wbrc_ref, wout_ref, fng_ref)


def _sample_layer(x, cache_k, cache_v, s0, cb, weights, params):
    n = x.shape[0]
    sb = SAMPLE_BLOCK
    full = lambda a: pl.BlockSpec(a.shape, lambda i: (0,) * a.ndim, pipeline_mode=pl.Buffered(1))
    in_specs = [
        full(x),
        pl.BlockSpec((sb, MEM_LEN * HEADS, HEAD_DIM), lambda i: (i, 0, 0)),
        pl.BlockSpec((sb, MEM_LEN * HEADS, HEAD_DIM), lambda i: (i, 0, 0)),
        pl.BlockSpec((sb, HEADS, HEAD_DIM, HEAD_DIM), lambda i: (i, 0, 0, 0)),
        pl.BlockSpec((CONV_WIDTH - 1, sb, CONV_CH), lambda i: (0, i, 0)),
    ] + [full(w) for w in weights] + [full(p) for p in params]
    out_specs = [
        pl.BlockSpec((n, D_MODEL), lambda i: (0, 0)),
        pl.BlockSpec((sb, HEADS, HEAD_DIM, HEAD_DIM), lambda i: (i, 0, 0, 0)),
        pl.BlockSpec((CONV_WIDTH - 1, sb, CONV_CH), lambda i: (0, i, 0)),
        pl.BlockSpec((n, WIDTH), lambda i: (0, 0)),
    ]
    out_shape = [
        jax.ShapeDtypeStruct((n, D_MODEL), F32),
        jax.ShapeDtypeStruct((n, HEADS, HEAD_DIM, HEAD_DIM), F32),
        jax.ShapeDtypeStruct((CONV_WIDTH - 1, n, CONV_CH), F32),
        jax.ShapeDtypeStruct((n, WIDTH), F32),
    ]
    return pl.pallas_call(
        _sample_kernel,
        grid=(n // sb,),
        in_specs=in_specs,
        out_specs=out_specs,
        out_shape=out_shape,
        scratch_shapes=[pltpu.VMEM((n, D_MODEL), BF16),
                        pltpu.VMEM((n, CONV_CH), F32),
                        pltpu.VMEM((n, REST_MG), F32),
                        pltpu.VMEM((n, LANES), F32),
                        pltpu.VMEM((n, WIDTH), F32),
                        pltpu.VMEM((n, WIDTH), F32)],
        compiler_params=pltpu.CompilerParams(dimension_semantics=("arbitrary",),
                                             vmem_limit_bytes=VMEM_LIMIT),
        name="sample_layer",
    )(x, cache_k, cache_v, s0, cb, *weights, *params)


PACK_BLOCK = 768


def _pack_kernel(a_ref, b_ref, o_ref, bg_ref):
    j = pl.program_id(0)
    first_rest = CONV_CH // PACK_BLOCK
    a = a_ref[...]
    skipped = jnp.concatenate([a[2 * HEADS:], b_ref[...]], axis=0)
    rows = jnp.where(j < first_rest, a, skipped)
    o_ref[...] = rows.T.astype(BF16)

    @pl.when(j == first_rest)
    def _():
        logits = jnp.concatenate([a[:2 * HEADS], jnp.zeros((LANES - 2 * HEADS, D_MODEL), F32)], axis=0)
        bg_ref[...] = logits.T.astype(BF16)


def _pack_input_projection(w_t):
    assert 2 * HEADS == SUBLANES and CONV_CH % PACK_BLOCK == 0 and PACKED_COLS % PACK_BLOCK == 0
    return pl.pallas_call(
        _pack_kernel,
        grid=(PACKED_COLS // PACK_BLOCK,),
        in_specs=[pl.BlockSpec((PACK_BLOCK, D_MODEL), lambda j: (j, 0)),
                  pl.BlockSpec((SUBLANES, D_MODEL), lambda j: ((j + 1) * (PACK_BLOCK // SUBLANES), 0))],
        out_specs=[pl.BlockSpec((D_MODEL, PACK_BLOCK), lambda j: (0, j)),
                   pl.BlockSpec((D_MODEL, LANES), lambda j: (0, 0))],
        out_shape=[jax.ShapeDtypeStruct((D_MODEL, PACKED_COLS), BF16),
                   jax.ShapeDtypeStruct((D_MODEL, LANES), BF16)],
        compiler_params=pltpu.CompilerParams(dimension_semantics=("arbitrary",),
                                             vmem_limit_bytes=VMEM_LIMIT),
        name="pack_input_projection",
    )(w_t, w_t)


def _lanes_4_to_7(vec):
    return jnp.zeros((1, LANES), F32).at[0, HEADS:2 * HEADS].set(vec)


def kernel(x_prompt, x_sample, cache_mem_k, cache_mem_v, state_delta, state_conv, mem_prompt, norm_g, w_in, conv_w, a_log, dt_bias, a_norm_g, ln_v_g, ln_v_b, w_spatial, b_spatial, mem_norm_g, w_mem_kv, w_br_a, w_br_b, w_br_c, b_gate, w_out, final_norm_g):
    depth = norm_g.shape[0]
    assert depth == 1, "single-layer step"
    bsz, seq, _ = x_prompt.shape
    nsmp = x_sample.shape[0]
    assert x_sample.shape[1] == 1 and seq % PROMPT_BLOCK == 0 and nsmp % SAMPLE_BLOCK == 0
    assert w_in.shape[2] == CONV_CH + 2 * HEADS + REST_COLS

    weights = _pack_input_projection(w_in[0].T)
    params = (
        norm_g[0][None, :], conv_w[0], _lanes_4_to_7(a_log[0]), _lanes_4_to_7(dt_bias[0]),
        a_norm_g[0][None, :], ln_v_g[0][None, :], ln_v_b[0][None, :], w_spatial[0],
        jnp.broadcast_to(b_spatial[0][:, :, None], (MLP_GROUPS, MLP_CHUNK, WIDTH // MLP_GROUPS)),
        w_br_a[0].astype(BF16), w_br_b[0].astype(BF16), w_br_c[0].astype(BF16),
        b_gate[0].reshape(1, N_BRANCH * D_MODEL), w_out[0].astype(BF16), final_norm_g[None, :],
    )

    mk, mv, mkb, mvb = _memory_kv(mem_prompt.reshape(bsz * MEM_LEN, D_MODEL), mem_norm_g[0][None, :],
                                  w_mem_kv[0].astype(BF16))
    y_p, sd_p, sc_p = _prompt_layer(x_prompt, mkb.reshape(bsz, MEM_LEN, WIDTH),
                                    mvb.reshape(bsz, MEM_LEN, WIDTH), weights, params)
    y_s, sd_s, sc_s, vn_s = _sample_layer(
        x_sample.reshape(nsmp, D_MODEL), cache_mem_k.reshape(nsmp, MEM_LEN * HEADS, HEAD_DIM),
        cache_mem_v.reshape(nsmp, MEM_LEN * HEADS, HEAD_DIM), state_delta.reshape(state_delta.shape[1:]),
        jnp.transpose(state_conv[0], (1, 0, 2)), weights, params)

    kv_shape = (1, bsz, MEM_LEN, HEADS, HEAD_DIM)
    return (y_p, y_s.reshape(nsmp, 1, D_MODEL), sd_p[None], sc_p[None], mk.reshape(kv_shape),
            mv.reshape(kv_shape), sd_s[None], jnp.transpose(sc_s, (1, 0, 2))[None],
            vn_s.reshape(1, nsmp, 1, WIDTH))
```

```python
import math

import jax
import jax.numpy as jnp
from jax import lax
from jax.experimental import pallas as pl
from jax.experimental.pallas import tpu as pltpu

F32 = jnp.float32
BF16 = jnp.bfloat16

D_MODEL = 1024
HEADS = 4
HEAD_DIM = 128
WIDTH = HEADS * HEAD_DIM
CONV_WIDTH = 4
CONV_CH = 3 * WIDTH
MLP_GROUPS = 4
MLP_CHUNK = 128
MEM_LEN = 256
N_BRANCH = 3
EPS = 1e-6

LANES = 128
SUBLANES = 8

REST_AGATE = 0
REST_B = REST_AGATE + WIDTH
REST_C = REST_B + 3 * WIDTH
REST_MG = REST_C + 2 * WIDTH
REST_COLS = REST_MG + N_BRANCH * D_MODEL
PACKED_COLS = CONV_CH + REST_COLS

PROMPT_BLOCK = 512
DELTA_CHUNK = 64
SAMPLE_BLOCK = 8
VMEM_LIMIT = 56 * 1024 * 1024


def _dot(a, b):
    return jnp.dot(a.astype(BF16), b.astype(BF16), preferred_element_type=F32)


def _dot_nt(a, b):
    return lax.dot_general(a.astype(BF16), b.astype(BF16), (((1,), (1,)), ((), ())),
                           preferred_element_type=F32)


def _rms(x):
    return x * lax.rsqrt(jnp.mean(x * x, axis=-1, keepdims=True) + EPS)


def _l2norm(x, scale=1.0):
    return x * (lax.rsqrt(jnp.sum(x * x, axis=-1, keepdims=True) + EPS) * scale)


def _softplus(x):
    return jnp.maximum(x, 0.0) + jnp.log1p(jnp.exp(-jnp.abs(x)))


def _iota2(shape, dim):
    return lax.broadcasted_iota(jnp.int32, shape, dim)


def _memkv_kernel(mem_ref, g_ref, w_ref, k_ref, v_ref, kb_ref, vb_ref):
    xn = _rms(mem_ref[...]) * g_ref[...]
    kv = _dot(xn, w_ref[...])
    k = kv[:, :WIDTH]
    v = kv[:, WIDTH:]
    npos = k.shape[0]
    for hd in range(HEADS):
        k_ref[pl.ds(hd, npos, stride=HEADS), :] = k[:, hd * HEAD_DIM:(hd + 1) * HEAD_DIM]
        v_ref[pl.ds(hd, npos, stride=HEADS), :] = v[:, hd * HEAD_DIM:(hd + 1) * HEAD_DIM]
    kb_ref[...] = k.astype(BF16)
    vb_ref[...] = v.astype(BF16)


def _memory_kv(mem2d, mem_norm_g, w_mem_kv_bf):
    rows = mem2d.shape[0]
    blk = 512
    full = lambda shape: pl.BlockSpec(shape, lambda i: (0,) * len(shape))
    row_spec = lambda width: pl.BlockSpec((blk, width), lambda i: (i, 0))
    by_head = pl.BlockSpec((blk * HEADS, HEAD_DIM), lambda i: (i, 0))
    return pl.pallas_call(
        _memkv_kernel,
        grid=(rows // blk,),
        in_specs=[row_spec(D_MODEL), full((1, D_MODEL)), full((D_MODEL, 2 * WIDTH))],
        out_specs=[by_head] * 2 + [row_spec(WIDTH)] * 2,
        out_shape=[jax.ShapeDtypeStruct((rows * HEADS, HEAD_DIM), F32)] * 2
        + [jax.ShapeDtypeStruct((rows, WIDTH), BF16)] * 2,
        compiler_params=pltpu.CompilerParams(dimension_semantics=("arbitrary",),
                                             vmem_limit_bytes=VMEM_LIMIT),
        name="memory_kv",
    )(mem2d, mem_norm_g, w_mem_kv_bf)


def _beta_and_log_decay(bg, alog, dtb):
    beta = jax.nn.sigmoid(bg)
    g = -jnp.exp(alog) * _softplus(bg + dtb)
    return beta, g


def _spatial_weights(ws_ref):
    tril = _iota2((MLP_CHUNK, MLP_CHUNK), 0) >= _iota2((MLP_CHUNK, MLP_CHUNK), 1)
    return [jnp.where(tril, ws_ref[g], 0.0).astype(BF16) for g in range(MLP_GROUPS)]


def _layernorm(v, g, b):
    mu = jnp.mean(v, axis=-1, keepdims=True)
    vc = v - mu
    return vc * lax.rsqrt(jnp.mean(vc * vc, axis=-1, keepdims=True) + EPS) * g + b


def _rest(w_ref, lo, hi):
    return w_ref[:, CONV_CH + lo:CONV_CH + hi]


def _merge_and_project(x, hb, ya, yb, yc, w_ref, bgate_ref, wbra_ref, wbrb_ref, wbrc_ref, wout_ref,
                       fng_ref):
    merged = None
    for i, (yi, wbr) in enumerate(((ya, wbra_ref), (yb, wbrb_ref), (yc, wbrc_ref))):
        lo = i * D_MODEL
        gate = jax.nn.sigmoid(_dot(hb, _rest(w_ref, REST_MG + lo, REST_MG + lo + D_MODEL))
                              + bgate_ref[:, lo:lo + D_MODEL])
        term = gate * _dot(yi, wbr[...])
        merged = term if merged is None else merged + term
    out = x + _dot(merged, wout_ref[...])
    return _rms(out) * fng_ref[...]


def _chunk_cumsum(g, ck, scr):
    n = g.shape[0]
    pad = ck // 2
    pos = _iota2(g.shape, 0) % ck
    scr[0:pad, :] = jnp.zeros((pad, LANES), F32)
    d = 1
    while d < ck:
        scr[pad:pad + n, :] = g
        g = g + jnp.where(pos >= d, scr[pad - d:pad - d + n, :], 0.0)
        d *= 2
    return g


def _delta_chunk_terms(q, k, v, beta_col, gc_col, gc_row):
    n = q.shape[0]
    half = n // 2
    ri = _iota2((n, n), 0)
    ci = _iota2((n, n), 1)
    same = (ri < half) == (ci < half)
    kt = k.T
    decay = jnp.exp(jnp.where(ri >= ci, gc_col - gc_row, -1e30))
    kb = k * beta_col
    egc = jnp.exp(gc_col)
    low = jnp.where(ri > ci, _dot(kb, kt) * decay, 0.0)
    diag = jnp.where(same, low, 0.0)
    last = gc_row[:, n - 1:n]
    return dict(
        neg_low=-(diag[:half] + diag[half:]),
        off=jnp.where(same, 0.0, low),
        attn=jnp.where(ri >= ci, _dot(q, kt) * decay, 0.0),
        rhs=jnp.concatenate([v * beta_col, kb * egc], axis=1),
        qd=q * egc,
        kdt=kt * jnp.exp(last - gc_row),
        gl=jnp.exp(last),
    )


def _block_diag2(packed, dtype=BF16):
    n = packed.shape[0]
    left = _iota2(packed.shape, 1) < n
    return jnp.concatenate([jnp.where(left, packed, 0.0), jnp.where(left, 0.0, packed)],
                           axis=0).astype(dtype)


def _prompt_kernel(x_ref, mk_ref, mv_ref, w_ref, wbg_ref, normg_ref, convw_ref,
                   alog_ref, dtb_ref, anormg_ref, lng_ref, lnb_ref, ws_ref, bs_ref, wbra_ref, wbrb_ref,
                   wbrc_ref, bgate_ref, wout_ref, fng_ref,
                   y_ref, sd_ref, sc_ref,
                   convbuf, state, cum_scr):
    tb = PROMPT_BLOCK
    ck = DELTA_CHUNK
    nck = tb // ck
    t = pl.program_id(1)

    @pl.when(t == 0)
    def _():
        convbuf[:, 0:SUBLANES, :] = jnp.zeros((CONV_CH // LANES, SUBLANES, LANES), F32)
        state[...] = jnp.zeros_like(state)

    hb = (_rms(x_ref[0]) * normg_ref[...]).astype(BF16)

    pre = _dot(hb, w_ref[:, :CONV_CH])
    bg_col = _dot(hb, wbg_ref[...])
    pb = _dot(hb, _rest(w_ref, REST_B, REST_B + 3 * WIDTH))
    pc = _dot(hb, _rest(w_ref, REST_C, REST_C + 2 * WIDTH))
    agate = _dot(hb, _rest(w_ref, REST_AGATE, REST_AGATE + WIDTH))

    conv_tiles, tails = [], []
    for ct in range(CONV_CH // LANES):
        cols = slice(ct * LANES, (ct + 1) * LANES)
        convbuf[ct, SUBLANES:SUBLANES + tb, :] = pre[:, cols]
        acc = pre[:, cols] * convw_ref[CONV_WIDTH - 1:CONV_WIDTH, cols]
        for j in range(CONV_WIDTH - 1):
            shift = CONV_WIDTH - 1 - j
            acc = acc + convbuf[ct, SUBLANES - shift:SUBLANES - shift + tb, :] * convw_ref[j:j + 1, cols]
        conv_tiles.append(acc)
        tails.append(convbuf[ct, tb:tb + SUBLANES, :])
        convbuf[ct, 0:SUBLANES, :] = tails[ct]
    sc_ref[0] = jnp.concatenate(tails, axis=1)[SUBLANES - (CONV_WIDTH - 1):, :]
    qkv = jax.nn.silu(jnp.concatenate(conv_tiles, axis=1))

    beta_col, g_col = _beta_and_log_decay(bg_col, alog_ref[...], dtb_ref[...])
    n2 = 2 * ck
    assert nck % 2 == 0 and n2 == LANES
    npair = nck // 2
    gc_col = _chunk_cumsum(g_col, n2, cum_scr)
    gc_cols = [gc_col[j * n2:(j + 1) * n2, :] for j in range(npair)]
    gc_rows = [gc.T[:SUBLANES, :] for gc in gc_cols]

    groups = [(j, hd) for j in range(npair) for hd in range(HEADS)]
    qn, kn, vs = [], [], []
    for hd in range(HEADS):
        lo = hd * HEAD_DIM
        qn.append(_l2norm(qkv[:, lo:lo + HEAD_DIM], HEAD_DIM ** -0.5))
        kn.append(_l2norm(qkv[:, WIDTH + lo:WIDTH + lo + HEAD_DIM]))
        vs.append(qkv[:, 2 * WIDTH + lo:2 * WIDTH + lo + HEAD_DIM])
    terms = {}
    for j, hd in groups:
        rows = slice(j * n2, (j + 1) * n2)
        terms[j, hd] = _delta_chunk_terms(
            qn[hd][rows], kn[hd][rows], vs[hd][rows], beta_col[rows, hd:hd + 1],
            gc_cols[j][:, HEADS + hd:HEADS + hd + 1], gc_rows[j][HEADS + hd:HEADS + hd + 1, :])

    eye2 = (_iota2((ck, n2), 0) == _iota2((ck, n2), 1) % ck).astype(F32)
    powers = [terms[g]["neg_low"] for g in groups]
    invs = [eye2 + p for p in powers]
    powers = [_dot(p, _block_diag2(p)) for p in powers]

    def inverse_level(last=False):
        nonlocal powers, invs
        if last:
            invs = [inv + _dot(inv, _block_diag2(p)) for inv, p in zip(invs, powers)]
            return
        prods = [_dot(jnp.concatenate([inv, p], axis=0), _block_diag2(p)) for inv, p in zip(invs, powers)]
        invs = [inv + r[:ck] for inv, r in zip(invs, prods)]
        powers = [r[ck:] for r in prods]

    levels = 0
    while 2 ** (levels + 1) < ck:
        levels += 1
    assert levels == 5, "the side work below is placed for five squaring levels"

    n_slots = 2 * nck
    gate_cols = 2 * D_MODEL // nck
    n_gate = N_BRANCH * D_MODEL // gate_cols
    out_parts = 4
    out_cols = D_MODEL // out_parts
    n_early = n_gate + 2 * out_parts - n_slots
    assert 0 <= n_early <= levels and D_MODEL % gate_cols == 0
    gate_parts = []
    out_b, out_c = [], []

    def gate_piece(j):
        lo = j * gate_cols
        gate_parts.append(jax.nn.sigmoid(_dot(hb, _rest(w_ref, REST_MG + lo, REST_MG + lo + gate_cols))
                                         + bgate_ref[:, lo:lo + gate_cols]))

    side = [lambda j=j: gate_piece(j) for j in range(n_gate)]
    side += [lambda j=j: out_b.append(_dot(yb, wbrb_ref[:, j * out_cols:(j + 1) * out_cols]))
             for j in range(out_parts)]
    side += [lambda j=j: out_c.append(_dot(yc, wbrc_ref[:, j * out_cols:(j + 1) * out_cols]))
             for j in range(out_parts)]
    early, side = side[:n_early], side[n_early:]

    def inverse_level_and_filler(last=False):
        inverse_level(last)
        if early:
            early.pop(0)()

    scores = [_dot_nt(pc[:, hd * HEAD_DIM:(hd + 1) * HEAD_DIM], mk_ref[0, :, hd * HEAD_DIM:(hd + 1) * HEAD_DIM])
              for hd in range(HEADS)]
    inverse_level_and_filler()
    bu = pb[:, :WIDTH]
    vn = _layernorm(pb[:, WIDTH:2 * WIDTH], lng_ref[...], lnb_ref[...])
    ws = _spatial_weights(ws_ref)
    gw = WIDTH // MLP_GROUPS
    s_rows = []
    for n in range(tb // MLP_CHUNK):
        r0 = n * MLP_CHUNK
        s_rows.append(jnp.concatenate(
            [_dot(ws[g], vn[r0:r0 + MLP_CHUNK, g * gw:(g + 1) * gw]) + bs_ref[g]
             for g in range(MLP_GROUPS)], axis=1))
    yb = bu * jnp.concatenate(s_rows, axis=0) * jax.nn.silu(pb[:, 2 * WIDTH:])
    inverse_level_and_filler()
    inverse_level_and_filler()
    oc = []
    for hd in range(HEADS):
        p = jnp.exp2((scores[hd] - jnp.max(scores[hd], axis=-1, keepdims=True))
                     * (HEAD_DIM ** -0.5 * math.log2(math.e)))
        oc.append(_dot(p, mv_ref[0, :, hd * HEAD_DIM:(hd + 1) * HEAD_DIM])
                  / jnp.sum(p, axis=-1, keepdims=True))
    yc = jnp.concatenate(oc, axis=1) * jax.nn.silu(pc[:, WIDTH:])
    inverse_level_and_filler()
    inverse_level_and_filler(last=True)
    bds = [_block_diag2(inv, F32) for inv in invs]
    crosses = [_dot(bd, terms[g]["off"]) for g, bd in zip(groups, bds)]
    fulls = [bd - _dot(cr, bd) for bd, cr in zip(bds, crosses)]
    uws = {g: _dot(full, terms[g]["rhs"]) for g, full in zip(groups, fulls)}

    s_heads = [state[hd] for hd in range(HEADS)]
    o_chunks = [[] for _ in range(HEADS)]
    per_stage = len(side) // (2 * npair)
    assert per_stage * 2 * npair == len(side)
    for j in range(npair):
        ws_qs = []
        for hd in range(HEADS):
            wq = jnp.concatenate([uws[j, hd][:, HEAD_DIM:], terms[j, hd]["qd"]], axis=0)
            ws_qs.append(_dot(wq, s_heads[hd]))
        for _ in range(per_stage):
            side.pop(0)()
        for hd in range(HEADS):
            tm = terms[j, hd]
            v_new = uws[j, hd][:, :HEAD_DIM] - ws_qs[hd][:n2]
            o_chunks[hd].append(ws_qs[hd][n2:] + _dot(tm["attn"], v_new))
            s_heads[hd] = s_heads[hd] * tm["gl"] + _dot(tm["kdt"], v_new)
        for _ in range(per_stage):
            side.pop(0)()
    for hd in range(HEADS):
        state[hd] = s_heads[hd]
    sd_ref[0] = state[...]

    o_heads = []
    for hd in range(HEADS):
        lo = hd * HEAD_DIM
        o = jnp.concatenate(o_chunks[hd], axis=0)
        o_heads.append(_rms(o) * anormg_ref[...] * jax.nn.silu(agate[:, lo:lo + HEAD_DIM]))
    branch_out = [_dot(jnp.concatenate(o_heads, axis=1), wbra_ref[...]),
                  jnp.concatenate(out_b, axis=1), jnp.concatenate(out_c, axis=1)]
    gates = jnp.concatenate(gate_parts, axis=1)
    merged = None
    for i in range(N_BRANCH):
        term = gates[:, i * D_MODEL:(i + 1) * D_MODEL] * branch_out[i]
        merged = term if merged is None else merged + term
    out = x_ref[0] + _dot(merged, wout_ref[...])
    y_ref[0] = _rms(out) * fng_ref[...]


def _prompt_layer(x, mkb, mvb, weights, params):
    bsz, seq, _ = x.shape
    tb = PROMPT_BLOCK
    nt = seq // tb
    full = lambda a: pl.BlockSpec(a.shape, lambda b, t: (0,) * a.ndim, pipeline_mode=pl.Buffered(1))
    in_specs = [
        pl.BlockSpec((1, tb, D_MODEL), lambda b, t: (b, t, 0)),
        pl.BlockSpec((1, MEM_LEN, WIDTH), lambda b, t: (b, 0, 0)),
        pl.BlockSpec((1, MEM_LEN, WIDTH), lambda b, t: (b, 0, 0)),
    ] + [full(w) for w in weights] + [full(p) for p in params]
    out_specs = [
        pl.BlockSpec((1, tb, D_MODEL), lambda b, t: (b, t, 0)),
        pl.BlockSpec((1, HEADS, HEAD_DIM, HEAD_DIM), lambda b, t: (b, 0, 0, 0)),
        pl.BlockSpec((1, CONV_WIDTH - 1, CONV_CH), lambda b, t: (b, 0, 0)),
    ]
    out_shape = [
        jax.ShapeDtypeStruct((bsz, seq, D_MODEL), F32),
        jax.ShapeDtypeStruct((bsz, HEADS, HEAD_DIM, HEAD_DIM), F32),
        jax.ShapeDtypeStruct((bsz, CONV_WIDTH - 1, CONV_CH), F32),
    ]
    return pl.pallas_call(
        _prompt_kernel,
        grid=(bsz, nt),
        in_specs=in_specs,
        out_specs=out_specs,
        out_shape=out_shape,
        scratch_shapes=[pltpu.VMEM((CONV_CH // LANES, tb + SUBLANES, LANES), F32),
                        pltpu.VMEM((HEADS, HEAD_DIM, HEAD_DIM), F32),
                        pltpu.VMEM((DELTA_CHUNK + tb, LANES), F32)],
        compiler_params=pltpu.CompilerParams(dimension_semantics=("arbitrary", "arbitrary"),
                                             vmem_limit_bytes=VMEM_LIMIT),
        name="prompt_layer",
    )(x, mkb, mvb, *weights, *params)


def _sample_kernel(x_ref, ck_ref, cv_ref, s0_ref, cb_ref, w_ref, wbg_ref,
                   normg_ref, convw_ref, alog_ref, dtb_ref, anormg_ref, lng_ref, lnb_ref, ws_ref, bs_ref,
                   wbra_ref, wbrb_ref, wbrc_ref, bgate_ref, wout_ref, fng_ref,
                   y_ref, sd_ref, sc_ref, vn_ref,
                   hb_scr, pre_scr, rest_scr, bg_scr, ya_scr, yc_scr):
    sb = SAMPLE_BLOCK
    i = pl.program_id(0)
    nsteps = pl.num_programs(0)

    @pl.when(i == 0)
    def _():
        hb = (_rms(x_ref[:, 0, :]) * normg_ref[...]).astype(BF16)
        hb_scr[...] = hb
        pre_scr[...] = _dot(hb, w_ref[:, :CONV_CH])
        rest_scr[...] = _dot(hb, _rest(w_ref, 0, REST_MG))
        bg_scr[...] = _dot(hb, wbg_ref[...])
        bu = rest_scr[:, REST_B:REST_B + WIDTH]
        vn = _layernorm(rest_scr[:, REST_B + WIDTH:REST_B + 2 * WIDTH], lng_ref[...], lnb_ref[...])
        vn_ref[:, 0, :] = vn
        bgate = rest_scr[:, REST_B + 2 * WIDTH:REST_B + 3 * WIDTH]
        gw = WIDTH // MLP_GROUPS
        s = jnp.concatenate(
            [vn[:, g * gw:(g + 1) * gw] * ws_ref[g, 0:1, 0:1] + bs_ref[g, 0:1, :]
             for g in range(MLP_GROUPS)], axis=1)
        rest_scr[:, REST_B:REST_B + WIDTH] = bu * s * jax.nn.silu(bgate)

    r0 = pl.multiple_of(i * sb, sb)
    rows = pl.ds(r0, sb)
    pre = pre_scr[rows, :]
    beta, g = _beta_and_log_decay(bg_scr[rows, :], alog_ref[...], dtb_ref[...])
    decay = jnp.exp(g)
    agate = rest_scr[rows, REST_AGATE:REST_AGATE + WIDTH]
    cq = rest_scr[rows, REST_C:REST_C + WIDTH]
    cgate = rest_scr[rows, REST_C + WIDTH:REST_C + 2 * WIDTH]

    own_head = ((_iota2((SUBLANES, MEM_LEN * HEADS), 1) % HEADS)
                == (_iota2((SUBLANES, MEM_LEN * HEADS), 0) % HEADS))

    scores = []
    for s in range(sb):
        qh = jnp.concatenate([cq[s:s + 1, hd * HEAD_DIM:(hd + 1) * HEAD_DIM] for hd in range(HEADS)]
                             + [jnp.zeros((SUBLANES - HEADS, HEAD_DIM), F32)], axis=0)
        scores.append(_dot_nt(qh, ck_ref[s]) * (HEAD_DIM ** -0.5))

    carried = [cb_ref[j] for j in range(CONV_WIDTH - 1)]
    conv = pre * convw_ref[CONV_WIDTH - 1:CONV_WIDTH, :]
    for j in range(CONV_WIDTH - 1):
        conv = conv + carried[j] * convw_ref[j:j + 1, :]
    for j in range(1, CONV_WIDTH - 1):
        sc_ref[j - 1] = carried[j]
    sc_ref[CONV_WIDTH - 2] = pre
    qkv = jax.nn.silu(conv)

    row8 = _iota2((sb, HEAD_DIM), 0)
    row16 = _iota2((2 * sb, HEAD_DIM), 0)
    qs, ks, vs, kq_s = [], [], [], []
    for hd in range(HEADS):
        lo = hd * HEAD_DIM
        qs.append(_l2norm(qkv[:, lo:lo + HEAD_DIM], HEAD_DIM ** -0.5))
        ks.append(_l2norm(qkv[:, WIDTH + lo:WIDTH + lo + HEAD_DIM]))
        vs.append(qkv[:, 2 * WIDTH + lo:2 * WIDTH + lo + HEAD_DIM])
        kq = jnp.concatenate([ks[hd], qs[hd]], axis=0).astype(BF16)
        kq_s.append([_dot(kq, s0_ref[s, hd]) for s in range(sb)])
    o_heads = []
    for hd in range(HEADS):
        lo = hd * HEAD_DIM
        sk = jnp.zeros((sb, HEAD_DIM), F32)
        sq = jnp.zeros((sb, HEAD_DIM), F32)
        for s in range(sb):
            sk = jnp.where(row8 == s, kq_s[hd][s][:sb], sk)
            sq = jnp.where(row8 == s, kq_s[hd][s][sb:], sq)
        a = decay[:, HEADS + hd:HEADS + hd + 1]
        v_new = beta[:, hd:hd + 1] * (vs[hd] - a * sk)
        qk = jnp.sum(qs[hd] * ks[hd], axis=-1, keepdims=True)
        o = a * sq + qk * v_new
        o_heads.append(_rms(o) * anormg_ref[...] * jax.nn.silu(agate[:, lo:lo + HEAD_DIM]))
        kt = jnp.concatenate([ks[hd], jnp.zeros((HEAD_DIM - sb, HEAD_DIM), F32)], axis=0).T.astype(BF16)
        v_pad = jnp.concatenate([v_new, jnp.zeros((sb, HEAD_DIM), F32)], axis=0)
        zeros = jnp.zeros((HEAD_DIM - 2 * sb, HEAD_DIM), BF16)
        for s in range(sb):
            only_s = jnp.concatenate([jnp.where(row16 == s, v_pad, 0.0).astype(BF16), zeros], axis=0)
            sd_ref[s, hd] = a[s:s + 1, :] * s0_ref[s, hd] + _dot(kt, only_s)
    ya_scr[rows, :] = jnp.concatenate(o_heads, axis=1)

    probs, sums = [], []
    for s in range(sb):
        sc = jnp.where(own_head, scores[s], -1e30)
        sc = sc - jnp.max(sc, axis=-1, keepdims=True)
        probs.append(jnp.where(own_head, jnp.exp(sc), 0.0))
        sums.append(jnp.sum(probs[s], axis=-1, keepdims=True))
    yc_rows = []
    for s in range(sb):
        oc = _dot(probs[s], cv_ref[s]) / sums[s]
        oc = jnp.concatenate([oc[hd:hd + 1, :] for hd in range(HEADS)], axis=1)
        yc_rows.append(oc * jax.nn.silu(cgate[s:s + 1, :]))
    yc_scr[rows, :] = jnp.concatenate(yc_rows, axis=0)

    @pl.when(i == nsteps - 1)
    def _():
        y_ref[:, 0, :] = _merge_and_project(x_ref[:, 0, :], hb_scr[...], ya_scr[...],
                                        rest_scr[:, REST_B:REST_B + WIDTH], yc_scr[...], w_ref,
                                        bgate_ref, wbra_ref, wbrb_ref, wbrc_ref, wout_ref, fng_ref)


def _sample_layer(x, cache_k, cache_v, s0, cb, weights, params):
    n = x.shape[0]
    sb = SAMPLE_BLOCK
    full = lambda a: pl.BlockSpec(a.shape, lambda i: (0,) * a.ndim, pipeline_mode=pl.Buffered(1))
    in_specs = [
        full(x),
        pl.BlockSpec((sb, MEM_LEN * HEADS, HEAD_DIM), lambda i: (i, 0, 0)),
        pl.BlockSpec((sb, MEM_LEN * HEADS, HEAD_DIM), lambda i: (i, 0, 0)),
        pl.BlockSpec((sb, HEADS, HEAD_DIM, HEAD_DIM), lambda i: (i, 0, 0, 0)),
        pl.BlockSpec((CONV_WIDTH - 1, sb, CONV_CH), lambda i: (0, i, 0)),
    ] + [full(w) for w in weights] + [full(p) for p in params]
    out_specs = [
        pl.BlockSpec((n, 1, D_MODEL), lambda i: (0, 0, 0)),
        pl.BlockSpec((sb, HEADS, HEAD_DIM, HEAD_DIM), lambda i: (i, 0, 0, 0)),
        pl.BlockSpec((CONV_WIDTH - 1, sb, CONV_CH), lambda i: (0, i, 0)),
        pl.BlockSpec((n, 1, WIDTH), lambda i: (0, 0, 0)),
    ]
    out_shape = [
        jax.ShapeDtypeStruct((n, 1, D_MODEL), F32),
        jax.ShapeDtypeStruct((n, HEADS, HEAD_DIM, HEAD_DIM), F32),
        jax.ShapeDtypeStruct((CONV_WIDTH - 1, n, CONV_CH), F32),
        jax.ShapeDtypeStruct((n, 1, WIDTH), F32),
    ]
    return pl.pallas_call(
        _sample_kernel,
        grid=(n // sb,),
        in_specs=in_specs,
        out_specs=out_specs,
        out_shape=out_shape,
        scratch_shapes=[pltpu.VMEM((n, D_MODEL), BF16),
                        pltpu.VMEM((n, CONV_CH), F32),
                        pltpu.VMEM((n, REST_MG), F32),
                        pltpu.VMEM((n, LANES), F32),
                        pltpu.VMEM((n, WIDTH), F32),
                        pltpu.VMEM((n, WIDTH), F32)],
        compiler_params=pltpu.CompilerParams(dimension_semantics=("arbitrary",),
                                             vmem_limit_bytes=VMEM_LIMIT),
        name="sample_layer",
    )(x, cache_k, cache_v, s0, cb, *weights, *params)


PACK_BLOCK = 768


def _pack_kernel(a_ref, b_ref, o_ref, bg_ref):
    j = pl.program_id(0)
    first_rest = CONV_CH // PACK_BLOCK
    a = a_ref[...]
    skipped = jnp.concatenate([a[2 * HEADS:], b_ref[...]], axis=0)
    rows = jnp.where(j < first_rest, a, skipped)
    o_ref[...] = rows.T.astype(BF16)

    @pl.when(j == first_rest)
    def _():
        logits = jnp.concatenate([a[:2 * HEADS], jnp.zeros((LANES - 2 * HEADS, D_MODEL), F32)], axis=0)
        bg_ref[...] = logits.T.astype(BF16)


def _pack_input_projection(w_t):
    assert 2 * HEADS == SUBLANES and CONV_CH % PACK_BLOCK == 0 and PACKED_COLS % PACK_BLOCK == 0
    return pl.pallas_call(
        _pack_kernel,
        grid=(PACKED_COLS // PACK_BLOCK,),
        in_specs=[pl.BlockSpec((PACK_BLOCK, D_MODEL), lambda j: (j, 0)),
                  pl.BlockSpec((SUBLANES, D_MODEL), lambda j: ((j + 1) * (PACK_BLOCK // SUBLANES), 0))],
        out_specs=[pl.BlockSpec((D_MODEL, PACK_BLOCK), lambda j: (0, j)),
                   pl.BlockSpec((D_MODEL, LANES), lambda j: (0, 0))],
        out_shape=[jax.ShapeDtypeStruct((D_MODEL, PACKED_COLS), BF16),
                   jax.ShapeDtypeStruct((D_MODEL, LANES), BF16)],
        compiler_params=pltpu.CompilerParams(dimension_semantics=("arbitrary",),
                                             vmem_limit_bytes=VMEM_LIMIT),
        name="pack_input_projection",
    )(w_t, w_t)


def _lanes_4_to_7(vec):
    return jnp.zeros((1, LANES), F32).at[0, HEADS:2 * HEADS].set(vec)


def kernel(x_prompt, x_sample, cache_mem_k, cache_mem_v, state_delta, state_conv, mem_prompt, norm_g, w_in, conv_w, a_log, dt_bias, a_norm_g, ln_v_g, ln_v_b, w_spatial, b_spatial, mem_norm_g, w_mem_kv, w_br_a, w_br_b, w_br_c, b_gate, w_out, final_norm_g):
    depth = norm_g.shape[0]
    assert depth == 1, "single-layer step"
    bsz, seq, _ = x_prompt.shape
    nsmp = x_sample.shape[0]
    assert x_sample.shape[1] == 1 and seq % PROMPT_BLOCK == 0 and nsmp % SAMPLE_BLOCK == 0
    assert w_in.shape[2] == CONV_CH + 2 * HEADS + REST_COLS

    weights = _pack_input_projection(w_in[0].T)
    params = (
        norm_g[0][None, :], conv_w[0], _lanes_4_to_7(a_log[0]), _lanes_4_to_7(dt_bias[0]),
        a_norm_g[0][None, :], ln_v_g[0][None, :], ln_v_b[0][None, :], w_spatial[0],
        jnp.broadcast_to(b_spatial[0][:, :, None], (MLP_GROUPS, MLP_CHUNK, WIDTH // MLP_GROUPS)),
        w_br_a[0].astype(BF16), w_br_b[0].astype(BF16), w_br_c[0].astype(BF16),
        b_gate[0].reshape(1, N_BRANCH * D_MODEL), w_out[0].astype(BF16), final_norm_g[None, :],
    )

    mk, mv, mkb, mvb = _memory_kv(mem_prompt.reshape(bsz * MEM_LEN, D_MODEL), mem_norm_g[0][None, :],
                                  w_mem_kv[0].astype(BF16))
    y_p, sd_p, sc_p = _prompt_layer(x_prompt, mkb.reshape(bsz, MEM_LEN, WIDTH),
                                    mvb.reshape(bsz, MEM_LEN, WIDTH), weights, params)
    y_s, sd_s, sc_s, vn_s = _sample_layer(
        x_sample, cache_mem_k.reshape(nsmp, MEM_LEN * HEADS, HEAD_DIM),
        cache_mem_v.reshape(nsmp, MEM_LEN * HEADS, HEAD_DIM), state_delta.reshape(state_delta.shape[1:]),
        jnp.transpose(state_conv[0], (1, 0, 2)), weights, params)

    kv_shape = (1, bsz, MEM_LEN, HEADS, HEAD_DIM)
    return (y_p, y_s, sd_p[None], sc_p[None], mk.reshape(kv_shape),
            mv.reshape(kv_shape), sd_s[None], jnp.transpose(sc_s, (1, 0, 2))[None], vn_s[None])
```

```python
import math

import jax
import jax.numpy as jnp
from jax import lax
from jax.experimental import pallas as pl
from jax.experimental.pallas import tpu as pltpu

F32 = jnp.float32
BF16 = jnp.bfloat16

D_MODEL = 1024
HEADS = 4
HEAD_DIM = 128
WIDTH = HEADS * HEAD_DIM
CONV_WIDTH = 4
CONV_CH = 3 * WIDTH
MLP_GROUPS = 4
MLP_CHUNK = 128
MEM_LEN = 256
N_BRANCH = 3
EPS = 1e-6

LANES = 128
SUBLANES = 8

REST_AGATE = 0
REST_B = REST_AGATE + WIDTH
REST_C = REST_B + 3 * WIDTH
REST_MG = REST_C + 2 * WIDTH
REST_COLS = REST_MG + N_BRANCH * D_MODEL
PACKED_COLS = CONV_CH + REST_COLS

PROMPT_BLOCK = 512
DELTA_CHUNK = 64
SAMPLE_BLOCK = 8
VMEM_LIMIT = 56 * 1024 * 1024


def _dot(a, b):
    return jnp.dot(a.astype(BF16), b.astype(BF16), preferred_element_type=F32)


def _dot_nt(a, b):
    return lax.dot_general(a.astype(BF16), b.astype(BF16), (((1,), (1,)), ((), ())),
                           preferred_element_type=F32)


def _rms(x):
    return x * lax.rsqrt(jnp.mean(x * x, axis=-1, keepdims=True) + EPS)


def _l2norm(x, scale=1.0):
    return x * (lax.rsqrt(jnp.sum(x * x, axis=-1, keepdims=True) + EPS) * scale)


def _softplus(x):
    return jnp.maximum(x, 0.0) + jnp.log1p(jnp.exp(-jnp.abs(x)))


def _iota2(shape, dim):
    return lax.broadcasted_iota(jnp.int32, shape, dim)


def _memkv_kernel(mem_ref, g_ref, w_ref, k_ref, v_ref, kb_ref, vb_ref):
    xn = _rms(mem_ref[...]) * g_ref[...]
    kv = _dot(xn, w_ref[...])
    k = kv[:, :WIDTH]
    v = kv[:, WIDTH:]
    npos = k.shape[0]
    for hd in range(HEADS):
        k_ref[pl.ds(hd, npos, stride=HEADS), :] = k[:, hd * HEAD_DIM:(hd + 1) * HEAD_DIM]
        v_ref[pl.ds(hd, npos, stride=HEADS), :] = v[:, hd * HEAD_DIM:(hd + 1) * HEAD_DIM]
    kb_ref[...] = k.astype(BF16)
    vb_ref[...] = v.astype(BF16)


def _memory_kv(mem2d, mem_norm_g, w_mem_kv):
    rows = mem2d.shape[0]
    blk = 512
    full = lambda shape: pl.BlockSpec(shape, lambda i: (0,) * len(shape))
    row_spec = lambda width: pl.BlockSpec((blk, width), lambda i: (i, 0))
    by_head = pl.BlockSpec((blk * HEADS, HEAD_DIM), lambda i: (i, 0))
    return pl.pallas_call(
        _memkv_kernel,
        grid=(rows // blk,),
        in_specs=[row_spec(D_MODEL), full((1, D_MODEL)), full((D_MODEL, 2 * WIDTH))],
        out_specs=[by_head] * 2 + [row_spec(WIDTH)] * 2,
        out_shape=[jax.ShapeDtypeStruct((rows * HEADS, HEAD_DIM), F32)] * 2
        + [jax.ShapeDtypeStruct((rows, WIDTH), BF16)] * 2,
        compiler_params=pltpu.CompilerParams(dimension_semantics=("arbitrary",),
                                             vmem_limit_bytes=VMEM_LIMIT),
        name="memory_kv",
    )(mem2d, mem_norm_g, w_mem_kv)


def _beta_and_log_decay(bg, alog, dtb):
    beta = jax.nn.sigmoid(bg)
    g = -jnp.exp(alog) * _softplus(bg + dtb)
    return beta, g


def _spatial_weights(ws_ref):
    tril = _iota2((MLP_CHUNK, MLP_CHUNK), 0) >= _iota2((MLP_CHUNK, MLP_CHUNK), 1)
    return [jnp.where(tril, ws_ref[g], 0.0).astype(BF16) for g in range(MLP_GROUPS)]


def _layernorm(v, g, b):
    mu = jnp.mean(v, axis=-1, keepdims=True)
    vc = v - mu
    return vc * lax.rsqrt(jnp.mean(vc * vc, axis=-1, keepdims=True) + EPS) * g + b


def _rest(w_ref, lo, hi):
    return w_ref[:, CONV_CH + lo:CONV_CH + hi]


def _merge_and_project(x, hb, ya, yb, yc, w_ref, bgate_ref, wbra_ref, wbrb_ref, wbrc_ref, wout_ref,
                       fng_ref):
    merged = None
    for i, (yi, wbr) in enumerate(((ya, wbra_ref), (yb, wbrb_ref), (yc, wbrc_ref))):
        lo = i * D_MODEL
        gate = jax.nn.sigmoid(_dot(hb, _rest(w_ref, REST_MG + lo, REST_MG + lo + D_MODEL))
                              + bgate_ref[:, lo:lo + D_MODEL])
        term = gate * _dot(yi, wbr[...])
        merged = term if merged is None else merged + term
    out = x + _dot(merged, wout_ref[...])
    return _rms(out) * fng_ref[...]


def _chunk_cumsum(g, ck, scr):
    n = g.shape[0]
    pad = ck // 2
    pos = _iota2(g.shape, 0) % ck
    scr[0:pad, :] = jnp.zeros((pad, LANES), F32)
    d = 1
    while d < ck:
        scr[pad:pad + n, :] = g
        g = g + jnp.where(pos >= d, scr[pad - d:pad - d + n, :], 0.0)
        d *= 2
    return g


def _delta_chunk_terms(q, k, v, beta_col, gc_col, gc_row):
    n = q.shape[0]
    half = n // 2
    ri = _iota2((n, n), 0)
    ci = _iota2((n, n), 1)
    same = (ri < half) == (ci < half)
    kt = k.T
    decay = jnp.exp(jnp.where(ri >= ci, gc_col - gc_row, -1e30))
    kb = k * beta_col
    egc = jnp.exp(gc_col)
    low = jnp.where(ri > ci, _dot(kb, kt) * decay, 0.0)
    diag = jnp.where(same, low, 0.0)
    last = gc_row[:, n - 1:n]
    return dict(
        neg_low=-(diag[:half] + diag[half:]),
        off=jnp.where(same, 0.0, low),
        attn=jnp.where(ri >= ci, _dot(q, kt) * decay, 0.0),
        rhs=jnp.concatenate([v * beta_col, kb * egc], axis=1),
        qd=q * egc,
        kdt=kt * jnp.exp(last - gc_row),
        gl=jnp.exp(last),
    )


def _block_diag2(packed, dtype=BF16):
    n = packed.shape[0]
    left = _iota2(packed.shape, 1) < n
    return jnp.concatenate([jnp.where(left, packed, 0.0), jnp.where(left, 0.0, packed)],
                           axis=0).astype(dtype)


def _prompt_kernel(x_ref, mk_ref, mv_ref, w_ref, wbg_ref, normg_ref, convw_ref,
                   alog_ref, dtb_ref, anormg_ref, lng_ref, lnb_ref, ws_ref, bs_ref, wbra_ref, wbrb_ref,
                   wbrc_ref, bgate_ref, wout_ref, fng_ref,
                   y_ref, sd_ref, sc_ref,
                   convbuf, state, cum_scr):
    tb = PROMPT_BLOCK
    ck = DELTA_CHUNK
    nck = tb // ck
    t = pl.program_id(1)

    @pl.when(t == 0)
    def _():
        convbuf[:, 0:SUBLANES, :] = jnp.zeros((CONV_CH // LANES, SUBLANES, LANES), F32)
        state[...] = jnp.zeros_like(state)

    hb = (_rms(x_ref[0]) * normg_ref[...]).astype(BF16)

    pre = _dot(hb, w_ref[:, :CONV_CH])
    bg_col = _dot(hb, wbg_ref[...])
    pb = _dot(hb, _rest(w_ref, REST_B, REST_B + 3 * WIDTH))
    pc = _dot(hb, _rest(w_ref, REST_C, REST_C + 2 * WIDTH))
    agate = _dot(hb, _rest(w_ref, REST_AGATE, REST_AGATE + WIDTH))

    conv_tiles, tails = [], []
    for ct in range(CONV_CH // LANES):
        cols = slice(ct * LANES, (ct + 1) * LANES)
        convbuf[ct, SUBLANES:SUBLANES + tb, :] = pre[:, cols]
        acc = pre[:, cols] * convw_ref[CONV_WIDTH - 1:CONV_WIDTH, cols]
        for j in range(CONV_WIDTH - 1):
            shift = CONV_WIDTH - 1 - j
            acc = acc + convbuf[ct, SUBLANES - shift:SUBLANES - shift + tb, :] * convw_ref[j:j + 1, cols]
        conv_tiles.append(acc)
        tails.append(convbuf[ct, tb:tb + SUBLANES, :])
        convbuf[ct, 0:SUBLANES, :] = tails[ct]
    sc_ref[0] = jnp.concatenate(tails, axis=1)[SUBLANES - (CONV_WIDTH - 1):, :]
    qkv = jax.nn.silu(jnp.concatenate(conv_tiles, axis=1))

    beta_col, g_col = _beta_and_log_decay(bg_col, alog_ref[...], dtb_ref[...])
    n2 = 2 * ck
    assert nck % 2 == 0 and n2 == LANES
    npair = nck // 2
    gc_col = _chunk_cumsum(g_col, n2, cum_scr)
    gc_cols = [gc_col[j * n2:(j + 1) * n2, :] for j in range(npair)]
    gc_rows = [gc.T[:SUBLANES, :] for gc in gc_cols]

    groups = [(j, hd) for j in range(npair) for hd in range(HEADS)]
    qn, kn, vs = [], [], []
    for hd in range(HEADS):
        lo = hd * HEAD_DIM
        qn.append(_l2norm(qkv[:, lo:lo + HEAD_DIM], HEAD_DIM ** -0.5))
        kn.append(_l2norm(qkv[:, WIDTH + lo:WIDTH + lo + HEAD_DIM]))
        vs.append(qkv[:, 2 * WIDTH + lo:2 * WIDTH + lo + HEAD_DIM])
    terms = {}
    for j, hd in groups:
        rows = slice(j * n2, (j + 1) * n2)
        terms[j, hd] = _delta_chunk_terms(
            qn[hd][rows], kn[hd][rows], vs[hd][rows], beta_col[rows, hd:hd + 1],
            gc_cols[j][:, HEADS + hd:HEADS + hd + 1], gc_rows[j][HEADS + hd:HEADS + hd + 1, :])

    eye2 = (_iota2((ck, n2), 0) == _iota2((ck, n2), 1) % ck).astype(F32)
    powers = [terms[g]["neg_low"] for g in groups]
    invs = [eye2 + p for p in powers]
    powers = [_dot(p, _block_diag2(p)) for p in powers]

    def inverse_level(last=False):
        nonlocal powers, invs
        if last:
            invs = [inv + _dot(inv, _block_diag2(p)) for inv, p in zip(invs, powers)]
            return
        prods = [_dot(jnp.concatenate([inv, p], axis=0), _block_diag2(p)) for inv, p in zip(invs, powers)]
        invs = [inv + r[:ck] for inv, r in zip(invs, prods)]
        powers = [r[ck:] for r in prods]

    levels = 0
    while 2 ** (levels + 1) < ck:
        levels += 1
    assert levels == 5, "the side work below is placed for five squaring levels"

    n_slots = 2 * nck
    gate_cols = 2 * D_MODEL // nck
    n_gate = N_BRANCH * D_MODEL // gate_cols
    out_parts = 4
    out_cols = D_MODEL // out_parts
    n_early = n_gate + 2 * out_parts - n_slots
    assert 0 <= n_early <= levels and D_MODEL % gate_cols == 0
    gate_parts = []
    out_b, out_c = [], []

    def gate_piece(j):
        lo = j * gate_cols
        gate_parts.append(jax.nn.sigmoid(_dot(hb, _rest(w_ref, REST_MG + lo, REST_MG + lo + gate_cols))
                                         + bgate_ref[:, lo:lo + gate_cols]))

    side = [lambda j=j: gate_piece(j) for j in range(n_gate)]
    side += [lambda j=j: out_b.append(_dot(yb, wbrb_ref[:, j * out_cols:(j + 1) * out_cols]))
             for j in range(out_parts)]
    side += [lambda j=j: out_c.append(_dot(yc, wbrc_ref[:, j * out_cols:(j + 1) * out_cols]))
             for j in range(out_parts)]
    early, side = side[:n_early], side[n_early:]

    def inverse_level_and_filler(last=False):
        inverse_level(last)
        if early:
            early.pop(0)()

    scores = [_dot_nt(pc[:, hd * HEAD_DIM:(hd + 1) * HEAD_DIM], mk_ref[0, :, hd * HEAD_DIM:(hd + 1) * HEAD_DIM])
              for hd in range(HEADS)]
    inverse_level_and_filler()
    bu = pb[:, :WIDTH]
    vn = _layernorm(pb[:, WIDTH:2 * WIDTH], lng_ref[...], lnb_ref[...])
    ws = _spatial_weights(ws_ref)
    gw = WIDTH // MLP_GROUPS
    s_rows = []
    for n in range(tb // MLP_CHUNK):
        r0 = n * MLP_CHUNK
        s_rows.append(jnp.concatenate(
            [_dot(ws[g], vn[r0:r0 + MLP_CHUNK, g * gw:(g + 1) * gw]) + bs_ref[g]
             for g in range(MLP_GROUPS)], axis=1))
    yb = bu * jnp.concatenate(s_rows, axis=0) * jax.nn.silu(pb[:, 2 * WIDTH:])
    inverse_level_and_filler()
    inverse_level_and_filler()
    oc = []
    for hd in range(HEADS):
        p = jnp.exp2((scores[hd] - jnp.max(scores[hd], axis=-1, keepdims=True))
                     * (HEAD_DIM ** -0.5 * math.log2(math.e)))
        oc.append(_dot(p, mv_ref[0, :, hd * HEAD_DIM:(hd + 1) * HEAD_DIM])
                  / jnp.sum(p, axis=-1, keepdims=True))
    yc = jnp.concatenate(oc, axis=1) * jax.nn.silu(pc[:, WIDTH:])
    inverse_level_and_filler()
    inverse_level_and_filler(last=True)
    bds = [_block_diag2(inv, F32) for inv in invs]
    crosses = [_dot(bd, terms[g]["off"]) for g, bd in zip(groups, bds)]
    fulls = [bd - _dot(cr, bd) for bd, cr in zip(bds, crosses)]
    uws = {g: _dot(full, terms[g]["rhs"]) for g, full in zip(groups, fulls)}

    s_heads = [state[hd] for hd in range(HEADS)]
    o_chunks = [[] for _ in range(HEADS)]
    per_stage = len(side) // (2 * npair)
    assert per_stage * 2 * npair == len(side)
    for j in range(npair):
        ws_qs = []
        for hd in range(HEADS):
            wq = jnp.concatenate([uws[j, hd][:, HEAD_DIM:], terms[j, hd]["qd"]], axis=0)
            ws_qs.append(_dot(wq, s_heads[hd]))
        for _ in range(per_stage):
            side.pop(0)()
        for hd in range(HEADS):
            tm = terms[j, hd]
            v_new = uws[j, hd][:, :HEAD_DIM] - ws_qs[hd][:n2]
            o_chunks[hd].append(ws_qs[hd][n2:] + _dot(tm["attn"], v_new))
            s_heads[hd] = s_heads[hd] * tm["gl"] + _dot(tm["kdt"], v_new)
        for _ in range(per_stage):
            side.pop(0)()
    for hd in range(HEADS):
        state[hd] = s_heads[hd]
    sd_ref[0] = state[...]

    o_heads = []
    for hd in range(HEADS):
        lo = hd * HEAD_DIM
        o = jnp.concatenate(o_chunks[hd], axis=0)
        o_heads.append(_rms(o) * anormg_ref[...] * jax.nn.silu(agate[:, lo:lo + HEAD_DIM]))
    branch_out = [_dot(jnp.concatenate(o_heads, axis=1), wbra_ref[...]),
                  jnp.concatenate(out_b, axis=1), jnp.concatenate(out_c, axis=1)]
    gates = jnp.concatenate(gate_parts, axis=1)
    merged = None
    for i in range(N_BRANCH):
        term = gates[:, i * D_MODEL:(i + 1) * D_MODEL] * branch_out[i]
        merged = term if merged is None else merged + term
    out = x_ref[0] + _dot(merged, wout_ref[...])
    y_ref[0] = _rms(out) * fng_ref[...]


def _prompt_layer(x, mkb, mvb, weights, params):
    bsz, seq, _ = x.shape
    tb = PROMPT_BLOCK
    nt = seq // tb
    full = lambda a: pl.BlockSpec(a.shape, lambda b, t: (0,) * a.ndim, pipeline_mode=pl.Buffered(1))
    in_specs = [
        pl.BlockSpec((1, tb, D_MODEL), lambda b, t: (b, t, 0)),
        pl.BlockSpec((1, MEM_LEN, WIDTH), lambda b, t: (b, 0, 0)),
        pl.BlockSpec((1, MEM_LEN, WIDTH), lambda b, t: (b, 0, 0)),
    ] + [full(w) for w in weights] + [full(p) for p in params]
    out_specs = [
        pl.BlockSpec((1, tb, D_MODEL), lambda b, t: (b, t, 0)),
        pl.BlockSpec((1, HEADS, HEAD_DIM, HEAD_DIM), lambda b, t: (b, 0, 0, 0)),
        pl.BlockSpec((1, CONV_WIDTH - 1, CONV_CH), lambda b, t: (b, 0, 0)),
    ]
    out_shape = [
        jax.ShapeDtypeStruct((bsz, seq, D_MODEL), F32),
        jax.ShapeDtypeStruct((bsz, HEADS, HEAD_DIM, HEAD_DIM), F32),
        jax.ShapeDtypeStruct((bsz, CONV_WIDTH - 1, CONV_CH), F32),
    ]
    return pl.pallas_call(
        _prompt_kernel,
        grid=(bsz, nt),
        in_specs=in_specs,
        out_specs=out_specs,
        out_shape=out_shape,
        scratch_shapes=[pltpu.VMEM((CONV_CH // LANES, tb + SUBLANES, LANES), F32),
                        pltpu.VMEM((HEADS, HEAD_DIM, HEAD_DIM), F32),
                        pltpu.VMEM((DELTA_CHUNK + tb, LANES), F32)],
        compiler_params=pltpu.CompilerParams(dimension_semantics=("arbitrary", "arbitrary"),
                                             vmem_limit_bytes=VMEM_LIMIT),
        name="prompt_layer",
    )(x, mkb, mvb, *weights, *params)


def _sample_kernel(x_ref, ck_ref, cv_ref, s0_ref, cb_ref, w_ref, wbg_ref,
                   normg_ref, convw_ref, alog_ref, dtb_ref, anormg_ref, lng_ref, lnb_ref, ws_ref, bs_ref,
                   wbra_ref, wbrb_ref, wbrc_ref, bgate_ref, wout_ref, fng_ref,
                   y_ref, sd_ref, sc_ref, vn_ref,
                   hb_scr, pre_scr, rest_scr, bg_scr, ya_scr, yc_scr):
    sb = SAMPLE_BLOCK
    i = pl.program_id(0)
    nsteps = pl.num_programs(0)

    @pl.when(i == 0)
    def _():
        hb = (_rms(x_ref[:, 0, :]) * normg_ref[...]).astype(BF16)
        hb_scr[...] = hb
        pre_scr[...] = _dot(hb, w_ref[:, :CONV_CH])
        rest_scr[...] = _dot(hb, _rest(w_ref, 0, REST_MG))
        bg_scr[...] = _dot(hb, wbg_ref[...])
        bu = rest_scr[:, REST_B:REST_B + WIDTH]
        vn = _layernorm(rest_scr[:, REST_B + WIDTH:REST_B + 2 * WIDTH], lng_ref[...], lnb_ref[...])
        vn_ref[:, 0, :] = vn
        bgate = rest_scr[:, REST_B + 2 * WIDTH:REST_B + 3 * WIDTH]
        gw = WIDTH // MLP_GROUPS
        s = jnp.concatenate(
            [vn[:, g * gw:(g + 1) * gw] * ws_ref[g, 0:1, 0:1] + bs_ref[g, 0:1, :]
             for g in range(MLP_GROUPS)], axis=1)
        rest_scr[:, REST_B:REST_B + WIDTH] = bu * s * jax.nn.silu(bgate)

    r0 = pl.multiple_of(i * sb, sb)
    rows = pl.ds(r0, sb)
    pre = pre_scr[rows, :]
    beta, g = _beta_and_log_decay(bg_scr[rows, :], alog_ref[...], dtb_ref[...])
    decay = jnp.exp(g)
    agate = rest_scr[rows, REST_AGATE:REST_AGATE + WIDTH]
    cq = rest_scr[rows, REST_C:REST_C + WIDTH]
    cgate = rest_scr[rows, REST_C + WIDTH:REST_C + 2 * WIDTH]

    own_head = ((_iota2((SUBLANES, MEM_LEN * HEADS), 1) % HEADS)
                == (_iota2((SUBLANES, MEM_LEN * HEADS), 0) % HEADS))

    scores = []
    for s in range(sb):
        qh = jnp.concatenate([cq[s:s + 1, hd * HEAD_DIM:(hd + 1) * HEAD_DIM] for hd in range(HEADS)]
                             + [jnp.zeros((SUBLANES - HEADS, HEAD_DIM), F32)], axis=0)
        scores.append(_dot_nt(qh, ck_ref[s]) * (HEAD_DIM ** -0.5))

    carried = [cb_ref[j] for j in range(CONV_WIDTH - 1)]
    conv = pre * convw_ref[CONV_WIDTH - 1:CONV_WIDTH, :]
    for j in range(CONV_WIDTH - 1):
        conv = conv + carried[j] * convw_ref[j:j + 1, :]
    for j in range(1, CONV_WIDTH - 1):
        sc_ref[j - 1] = carried[j]
    sc_ref[CONV_WIDTH - 2] = pre
    qkv = jax.nn.silu(conv)

    row8 = _iota2((sb, HEAD_DIM), 0)
    row16 = _iota2((2 * sb, HEAD_DIM), 0)
    qs, ks, vs, kq_s = [], [], [], []
    for hd in range(HEADS):
        lo = hd * HEAD_DIM
        qs.append(_l2norm(qkv[:, lo:lo + HEAD_DIM], HEAD_DIM ** -0.5))
        ks.append(_l2norm(qkv[:, WIDTH + lo:WIDTH + lo + HEAD_DIM]))
        vs.append(qkv[:, 2 * WIDTH + lo:2 * WIDTH + lo + HEAD_DIM])
        kq = jnp.concatenate([ks[hd], qs[hd]], axis=0).astype(BF16)
        kq_s.append([_dot(kq, s0_ref[s, hd]) for s in range(sb)])
    o_heads = []
    for hd in range(HEADS):
        lo = hd * HEAD_DIM
        sk = jnp.zeros((sb, HEAD_DIM), F32)
        sq = jnp.zeros((sb, HEAD_DIM), F32)
        for s in range(sb):
            sk = jnp.where(row8 == s, kq_s[hd][s][:sb], sk)
            sq = jnp.where(row8 == s, kq_s[hd][s][sb:], sq)
        a = decay[:, HEADS + hd:HEADS + hd + 1]
        v_new = beta[:, hd:hd + 1] * (vs[hd] - a * sk)
        qk = jnp.sum(qs[hd] * ks[hd], axis=-1, keepdims=True)
        o = a * sq + qk * v_new
        o_heads.append(_rms(o) * anormg_ref[...] * jax.nn.silu(agate[:, lo:lo + HEAD_DIM]))
        kt = jnp.concatenate([ks[hd], jnp.zeros((HEAD_DIM - sb, HEAD_DIM), F32)], axis=0).T.astype(BF16)
        v_pad = jnp.concatenate([v_new, jnp.zeros((sb, HEAD_DIM), F32)], axis=0)
        zeros = jnp.zeros((HEAD_DIM - 2 * sb, HEAD_DIM), BF16)
        for s in range(sb):
            only_s = jnp.concatenate([jnp.where(row16 == s, v_pad, 0.0).astype(BF16), zeros], axis=0)
            sd_ref[s, hd] = a[s:s + 1, :] * s0_ref[s, hd] + _dot(kt, only_s)
    ya_scr[rows, :] = jnp.concatenate(o_heads, axis=1)

    probs, sums = [], []
    for s in range(sb):
        sc = jnp.where(own_head, scores[s], -1e30)
        sc = sc - jnp.max(sc, axis=-1, keepdims=True)
        probs.append(jnp.where(own_head, jnp.exp(sc), 0.0))
        sums.append(jnp.sum(probs[s], axis=-1, keepdims=True))
    yc_rows = []
    for s in range(sb):
        oc = _dot(probs[s], cv_ref[s]) / sums[s]
        oc = jnp.concatenate([oc[hd:hd + 1, :] for hd in range(HEADS)], axis=1)
        yc_rows.append(oc * jax.nn.silu(cgate[s:s + 1, :]))
    yc_scr[rows, :] = jnp.concatenate(yc_rows, axis=0)

    @pl.when(i == nsteps - 1)
    def _():
        y_ref[:, 0, :] = _merge_and_project(x_ref[:, 0, :], hb_scr[...], ya_scr[...],
                                        rest_scr[:, REST_B:REST_B + WIDTH], yc_scr[...], w_ref,
                                        bgate_ref, wbra_ref, wbrb_ref, wbrc_ref, wout_ref, fng_ref)


def _sample_layer(x, cache_k, cache_v, s0, cb, weights, params):
    n = x.shape[0]
    sb = SAMPLE_BLOCK
    full = lambda a: pl.BlockSpec(a.shape, lambda i: (0,) * a.ndim, pipeline_mode=pl.Buffered(1))
    in_specs = [
        full(x),
        pl.BlockSpec((sb, MEM_LEN * HEADS, HEAD_DIM), lambda i: (i, 0, 0)),
        pl.BlockSpec((sb, MEM_LEN * HEADS, HEAD_DIM), lambda i: (i, 0, 0)),
        pl.BlockSpec((sb, HEADS, HEAD_DIM, HEAD_DIM), lambda i: (i, 0, 0, 0)),
        pl.BlockSpec((CONV_WIDTH - 1, sb, CONV_CH), lambda i: (0, i, 0)),
    ] + [full(w) for w in weights] + [full(p) for p in params]
    out_specs = [
        pl.BlockSpec((n, 1, D_MODEL), lambda i: (0, 0, 0)),
        pl.BlockSpec((sb, HEADS, HEAD_DIM, HEAD_DIM), lambda i: (i, 0, 0, 0)),
        pl.BlockSpec((CONV_WIDTH - 1, sb, CONV_CH), lambda i: (0, i, 0)),
        pl.BlockSpec((n, 1, WIDTH), lambda i: (0, 0, 0)),
    ]
    out_shape = [
        jax.ShapeDtypeStruct((n, 1, D_MODEL), F32),
        jax.ShapeDtypeStruct((n, HEADS, HEAD_DIM, HEAD_DIM), F32),
        jax.ShapeDtypeStruct((CONV_WIDTH - 1, n, CONV_CH), F32),
        jax.ShapeDtypeStruct((n, 1, WIDTH), F32),
    ]
    return pl.pallas_call(
        _sample_kernel,
        grid=(n // sb,),
        in_specs=in_specs,
        out_specs=out_specs,
        out_shape=out_shape,
        scratch_shapes=[pltpu.VMEM((n, D_MODEL), BF16),
                        pltpu.VMEM((n, CONV_CH), F32),
                        pltpu.VMEM((n, REST_MG), F32),
                        pltpu.VMEM((n, LANES), F32),
                        pltpu.VMEM((n, WIDTH), F32),
                        pltpu.VMEM((n, WIDTH), F32)],
        compiler_params=pltpu.CompilerParams(dimension_semantics=("arbitrary",),
                                             vmem_limit_bytes=VMEM_LIMIT),
        name="sample_layer",
    )(x, cache_k, cache_v, s0, cb, *weights, *params)


PACK_BLOCK = 768


def _pack_kernel(a_ref, b_ref, o_ref, bg_ref):
    j = pl.program_id(0)
    first_rest = CONV_CH // PACK_BLOCK
    a = a_ref[...]
    skipped = jnp.concatenate([a[2 * HEADS:], b_ref[...]], axis=0)
    rows = jnp.where(j < first_rest, a, skipped)
    o_ref[...] = rows.T.astype(BF16)

    @pl.when(j == first_rest)
    def _():
        logits = jnp.concatenate([a[:2 * HEADS], jnp.zeros((LANES - 2 * HEADS, D_MODEL), F32)], axis=0)
        bg_ref[...] = logits.T.astype(BF16)


def _pack_input_projection(w_t):
    assert 2 * HEADS == SUBLANES and CONV_CH % PACK_BLOCK == 0 and PACKED_COLS % PACK_BLOCK == 0
    return pl.pallas_call(
        _pack_kernel,
        grid=(PACKED_COLS // PACK_BLOCK,),
        in_specs=[pl.BlockSpec((PACK_BLOCK, D_MODEL), lambda j: (j, 0)),
                  pl.BlockSpec((SUBLANES, D_MODEL), lambda j: ((j + 1) * (PACK_BLOCK // SUBLANES), 0))],
        out_specs=[pl.BlockSpec((D_MODEL, PACK_BLOCK), lambda j: (0, j)),
                   pl.BlockSpec((D_MODEL, LANES), lambda j: (0, 0))],
        out_shape=[jax.ShapeDtypeStruct((D_MODEL, PACKED_COLS), BF16),
                   jax.ShapeDtypeStruct((D_MODEL, LANES), BF16)],
        compiler_params=pltpu.CompilerParams(dimension_semantics=("arbitrary",),
                                             vmem_limit_bytes=VMEM_LIMIT),
        name="pack_input_projection",
    )(w_t, w_t)


def _lanes_4_to_7(vec):
    return jnp.zeros((1, LANES), F32).at[0, HEADS:2 * HEADS].set(vec)


def kernel(x_prompt, x_sample, cache_mem_k, cache_mem_v, state_delta, state_conv, mem_prompt, norm_g, w_in, conv_w, a_log, dt_bias, a_norm_g, ln_v_g, ln_v_b, w_spatial, b_spatial, mem_norm_g, w_mem_kv, w_br_a, w_br_b, w_br_c, b_gate, w_out, final_norm_g):
    depth = norm_g.shape[0]
    assert depth == 1, "single-layer step"
    bsz, seq, _ = x_prompt.shape
    nsmp = x_sample.shape[0]
    assert x_sample.shape[1] == 1 and seq % PROMPT_BLOCK == 0 and nsmp % SAMPLE_BLOCK == 0
    assert w_in.shape[2] == CONV_CH + 2 * HEADS + REST_COLS

    weights = _pack_input_projection(w_in[0].T)
    params = (
        norm_g[0][None, :], conv_w[0], _lanes_4_to_7(a_log[0]), _lanes_4_to_7(dt_bias[0]),
        a_norm_g[0][None, :], ln_v_g[0][None, :], ln_v_b[0][None, :], w_spatial[0],
        jnp.broadcast_to(b_spatial[0][:, :, None], (MLP_GROUPS, MLP_CHUNK, WIDTH // MLP_GROUPS)),
        w_br_a[0].astype(BF16), w_br_b[0].astype(BF16), w_br_c[0].astype(BF16),
        b_gate[0].reshape(1, N_BRANCH * D_MODEL), w_out[0].astype(BF16), final_norm_g[None, :],
    )

    mk, mv, mkb, mvb = _memory_kv(mem_prompt.reshape(bsz * MEM_LEN, D_MODEL), mem_norm_g[0][None, :],
                                  w_mem_kv[0])
    y_p, sd_p, sc_p = _prompt_layer(x_prompt, mkb.reshape(bsz, MEM_LEN, WIDTH),
                                    mvb.reshape(bsz, MEM_LEN, WIDTH), weights, params)
    y_s, sd_s, sc_s, vn_s = _sample_layer(
        x_sample, cache_mem_k.reshape(nsmp, MEM_LEN * HEADS, HEAD_DIM),
        cache_mem_v.reshape(nsmp, MEM_LEN * HEADS, HEAD_DIM), state_delta.reshape(state_delta.shape[1:]),
        jnp.transpose(state_conv[0], (1, 0, 2)), weights, params)

    kv_shape = (1, bsz, MEM_LEN, HEADS, HEAD_DIM)
    return (y_p, y_s, sd_p[None], sc_p[None], mk.reshape(kv_shape),
            mv.reshape(kv_shape), sd_s[None], jnp.transpose(sc_s, (1, 0, 2))[None], vn_s[None])
```

```python
import math

import jax
import jax.numpy as jnp
from jax import lax
from jax.experimental import pallas as pl
from jax.experimental.pallas import tpu as pltpu

F32 = jnp.float32
BF16 = jnp.bfloat16

D_MODEL = 1024
HEADS = 4
HEAD_DIM = 128
WIDTH = HEADS * HEAD_DIM
CONV_WIDTH = 4
CONV_CH = 3 * WIDTH
MLP_GROUPS = 4
MLP_CHUNK = 128
MEM_LEN = 256
N_BRANCH = 3
EPS = 1e-6

LANES = 128
SUBLANES = 8

REST_AGATE = 0
REST_B = REST_AGATE + WIDTH
REST_C = REST_B + 3 * WIDTH
REST_MG = REST_C + 2 * WIDTH
REST_COLS = REST_MG + N_BRANCH * D_MODEL
PACKED_COLS = CONV_CH + REST_COLS

PROMPT_BLOCK = 512
DELTA_CHUNK = 64
SAMPLE_BLOCK = 8
VMEM_LIMIT = 56 * 1024 * 1024


def _dot(a, b):
    return jnp.dot(a.astype(BF16), b.astype(BF16), preferred_element_type=F32)


def _dot_nt(a, b):
    return lax.dot_general(a.astype(BF16), b.astype(BF16), (((1,), (1,)), ((), ())),
                           preferred_element_type=F32)


def _rms(x):
    return x * lax.rsqrt(jnp.mean(x * x, axis=-1, keepdims=True) + EPS)


def _l2norm(x, scale=1.0):
    return x * (lax.rsqrt(jnp.sum(x * x, axis=-1, keepdims=True) + EPS) * scale)


def _softplus(x):
    return jnp.maximum(x, 0.0) + jnp.log1p(jnp.exp(-jnp.abs(x)))


def _iota2(shape, dim):
    return lax.broadcasted_iota(jnp.int32, shape, dim)


def _memkv_kernel(mem_ref, g_ref, w_ref, k_ref, v_ref, kb_ref, vb_ref):
    xn = _rms(mem_ref[...]) * g_ref[...]
    kv = _dot(xn, w_ref[...])
    k = kv[:, :WIDTH]
    v = kv[:, WIDTH:]
    npos = k.shape[0]
    for hd in range(HEADS):
        k_ref[pl.ds(hd, npos, stride=HEADS), :] = k[:, hd * HEAD_DIM:(hd + 1) * HEAD_DIM]
        v_ref[pl.ds(hd, npos, stride=HEADS), :] = v[:, hd * HEAD_DIM:(hd + 1) * HEAD_DIM]
    kb_ref[...] = k.astype(BF16)
    vb_ref[...] = v.astype(BF16)


def _memory_kv(mem2d, mem_norm_g, w_mem_kv):
    rows = mem2d.shape[0]
    blk = 512
    full = lambda shape: pl.BlockSpec(shape, lambda i: (0,) * len(shape))
    row_spec = lambda width: pl.BlockSpec((blk, width), lambda i: (i, 0))
    by_head = pl.BlockSpec((blk * HEADS, HEAD_DIM), lambda i: (i, 0))
    return pl.pallas_call(
        _memkv_kernel,
        grid=(rows // blk,),
        in_specs=[row_spec(D_MODEL), full((1, D_MODEL)), full((D_MODEL, 2 * WIDTH))],
        out_specs=[by_head] * 2 + [row_spec(WIDTH)] * 2,
        out_shape=[jax.ShapeDtypeStruct((rows * HEADS, HEAD_DIM), F32)] * 2
        + [jax.ShapeDtypeStruct((rows, WIDTH), BF16)] * 2,
        compiler_params=pltpu.CompilerParams(dimension_semantics=("arbitrary",),
                                             vmem_limit_bytes=VMEM_LIMIT),
        name="memory_kv",
    )(mem2d, mem_norm_g, w_mem_kv)


def _beta_and_log_decay(bg, alog, dtb):
    beta = jax.nn.sigmoid(bg)
    g = -jnp.exp(alog) * _softplus(bg + dtb)
    return beta, g


def _spatial_weights(ws_ref):
    tril = _iota2((MLP_CHUNK, MLP_CHUNK), 0) >= _iota2((MLP_CHUNK, MLP_CHUNK), 1)
    return [jnp.where(tril, ws_ref[g], 0.0).astype(BF16) for g in range(MLP_GROUPS)]


def _layernorm(v, g, b):
    mu = jnp.mean(v, axis=-1, keepdims=True)
    vc = v - mu
    return vc * lax.rsqrt(jnp.mean(vc * vc, axis=-1, keepdims=True) + EPS) * g + b


def _rest(w_ref, lo, hi):
    return w_ref[:, CONV_CH + lo:CONV_CH + hi]


def _merge_and_project(x, hb, ya, yb, yc, w_ref, bgate_ref, wbra_ref, wbrb_ref, wbrc_ref, wout_ref,
                       fng_ref):
    merged = None
    for i, (yi, wbr) in enumerate(((ya, wbra_ref), (yb, wbrb_ref), (yc, wbrc_ref))):
        lo = i * D_MODEL
        gate = jax.nn.sigmoid(_dot(hb, _rest(w_ref, REST_MG + lo, REST_MG + lo + D_MODEL))
                              + bgate_ref[:, lo:lo + D_MODEL])
        term = gate * _dot(yi, wbr[...])
        merged = term if merged is None else merged + term
    out = x + _dot(merged, wout_ref[...])
    return _rms(out) * fng_ref[...]


def _chunk_cumsum(g, ck, scr):
    n = g.shape[0]
    pad = ck // 2
    pos = _iota2(g.shape, 0) % ck
    scr[0:pad, :] = jnp.zeros((pad, LANES), F32)
    d = 1
    while d < ck:
        scr[pad:pad + n, :] = g
        g = g + jnp.where(pos >= d, scr[pad - d:pad - d + n, :], 0.0)
        d *= 2
    return g


def _delta_chunk_terms(q, k, v, beta_col, gc_col, gc_row):
    n = q.shape[0]
    half = n // 2
    ri = _iota2((n, n), 0)
    ci = _iota2((n, n), 1)
    same = (ri < half) == (ci < half)
    kt = k.T
    decay = jnp.exp(jnp.where(ri >= ci, gc_col - gc_row, -1e30))
    kb = k * beta_col
    egc = jnp.exp(gc_col)
    low = jnp.where(ri > ci, _dot(kb, kt) * decay, 0.0)
    diag = jnp.where(same, low, 0.0)
    last = gc_row[:, n - 1:n]
    return dict(
        neg_low=-(diag[:half] + diag[half:]),
        off=jnp.where(same, 0.0, low),
        attn=jnp.where(ri >= ci, _dot(q, kt) * decay, 0.0),
        rhs=jnp.concatenate([v * beta_col, kb * egc], axis=1),
        qd=q * egc,
        kdt=kt * jnp.exp(last - gc_row),
        gl=jnp.exp(last),
    )


def _block_diag2(packed, dtype=BF16):
    n = packed.shape[0]
    left = _iota2(packed.shape, 1) < n
    return jnp.concatenate([jnp.where(left, packed, 0.0), jnp.where(left, 0.0, packed)],
                           axis=0).astype(dtype)


def _prompt_kernel(x_ref, mk_ref, mv_ref, w_ref, wbg_ref, normg_ref, convw_ref,
                   alog_ref, dtb_ref, anormg_ref, lng_ref, lnb_ref, ws_ref, bs_ref, wbra_ref, wbrb_ref,
                   wbrc_ref, bgate_ref, wout_ref, fng_ref,
                   y_ref, sd_ref, sc_ref,
                   convbuf, state, cum_scr):
    tb = PROMPT_BLOCK
    ck = DELTA_CHUNK
    nck = tb // ck
    t = pl.program_id(1)

    @pl.when(t == 0)
    def _():
        convbuf[:, 0:SUBLANES, :] = jnp.zeros((CONV_CH // LANES, SUBLANES, LANES), F32)
        state[...] = jnp.zeros_like(state)

    hb = (_rms(x_ref[0]) * normg_ref[...]).astype(BF16)

    pre = _dot(hb, w_ref[:, :CONV_CH])
    bg_col = _dot(hb, wbg_ref[...])
    pb = _dot(hb, _rest(w_ref, REST_B, REST_B + 3 * WIDTH))
    pc = _dot(hb, _rest(w_ref, REST_C, REST_C + 2 * WIDTH))
    agate = _dot(hb, _rest(w_ref, REST_AGATE, REST_AGATE + WIDTH))

    conv_tiles, tails = [], []
    for ct in range(CONV_CH // LANES):
        cols = slice(ct * LANES, (ct + 1) * LANES)
        convbuf[ct, SUBLANES:SUBLANES + tb, :] = pre[:, cols]
        acc = pre[:, cols] * convw_ref[CONV_WIDTH - 1:CONV_WIDTH, cols]
        for j in range(CONV_WIDTH - 1):
            shift = CONV_WIDTH - 1 - j
            acc = acc + convbuf[ct, SUBLANES - shift:SUBLANES - shift + tb, :] * convw_ref[j:j + 1, cols]
        conv_tiles.append(acc)
        tails.append(convbuf[ct, tb:tb + SUBLANES, :])
        convbuf[ct, 0:SUBLANES, :] = tails[ct]
    sc_ref[0] = jnp.concatenate(tails, axis=1)[SUBLANES - (CONV_WIDTH - 1):, :]
    qkv = jax.nn.silu(jnp.concatenate(conv_tiles, axis=1))

    beta_col, g_col = _beta_and_log_decay(bg_col, alog_ref[...], dtb_ref[...])
    n2 = 2 * ck
    assert nck % 2 == 0 and n2 == LANES
    npair = nck // 2
    gc_col = _chunk_cumsum(g_col, n2, cum_scr)
    gc_cols = [gc_col[j * n2:(j + 1) * n2, :] for j in range(npair)]
    gc_rows = [gc.T[:SUBLANES, :] for gc in gc_cols]

    groups = [(j, hd) for j in range(npair) for hd in range(HEADS)]
    qn, kn, vs = [], [], []
    for hd in range(HEADS):
        lo = hd * HEAD_DIM
        qn.append(_l2norm(qkv[:, lo:lo + HEAD_DIM], HEAD_DIM ** -0.5))
        kn.append(_l2norm(qkv[:, WIDTH + lo:WIDTH + lo + HEAD_DIM]))
        vs.append(qkv[:, 2 * WIDTH + lo:2 * WIDTH + lo + HEAD_DIM])
    terms = {}
    for j, hd in groups:
        rows = slice(j * n2, (j + 1) * n2)
        terms[j, hd] = _delta_chunk_terms(
            qn[hd][rows], kn[hd][rows], vs[hd][rows], beta_col[rows, hd:hd + 1],
            gc_cols[j][:, HEADS + hd:HEADS + hd + 1], gc_rows[j][HEADS + hd:HEADS + hd + 1, :])

    eye2 = (_iota2((ck, n2), 0) == _iota2((ck, n2), 1) % ck).astype(F32)
    powers = [terms[g]["neg_low"] for g in groups]
    invs = [eye2 + p for p in powers]
    powers = [_dot(p, _block_diag2(p)) for p in powers]

    def inverse_level(last=False):
        nonlocal powers, invs
        if last:
            invs = [inv + _dot(inv, _block_diag2(p)) for inv, p in zip(invs, powers)]
            return
        prods = [_dot(jnp.concatenate([inv, p], axis=0), _block_diag2(p)) for inv, p in zip(invs, powers)]
        invs = [inv + r[:ck] for inv, r in zip(invs, prods)]
        powers = [r[ck:] for r in prods]

    levels = 0
    while 2 ** (levels + 1) < ck:
        levels += 1
    assert levels == 5, "the side work below is placed for five squaring levels"

    n_slots = 2 * nck
    gate_cols = 2 * D_MODEL // nck
    n_gate = N_BRANCH * D_MODEL // gate_cols
    out_parts = 4
    out_cols = D_MODEL // out_parts
    n_early = n_gate + 2 * out_parts - n_slots
    assert 0 <= n_early <= levels and D_MODEL % gate_cols == 0
    gate_parts = []
    out_b, out_c = [], []

    def gate_piece(j):
        lo = j * gate_cols
        gate_parts.append(jax.nn.sigmoid(_dot(hb, _rest(w_ref, REST_MG + lo, REST_MG + lo + gate_cols))
                                         + bgate_ref[:, lo:lo + gate_cols]))

    side = [lambda j=j: gate_piece(j) for j in range(n_gate)]
    side += [lambda j=j: out_b.append(_dot(yb, wbrb_ref[:, j * out_cols:(j + 1) * out_cols]))
             for j in range(out_parts)]
    side += [lambda j=j: out_c.append(_dot(yc, wbrc_ref[:, j * out_cols:(j + 1) * out_cols]))
             for j in range(out_parts)]
    early, side = side[:n_early], side[n_early:]

    def inverse_level_and_filler(last=False):
        inverse_level(last)
        if early:
            early.pop(0)()

    scores = [_dot_nt(pc[:, hd * HEAD_DIM:(hd + 1) * HEAD_DIM], mk_ref[0, :, hd * HEAD_DIM:(hd + 1) * HEAD_DIM])
              for hd in range(HEADS)]
    inverse_level_and_filler()
    bu = pb[:, :WIDTH]
    vn = _layernorm(pb[:, WIDTH:2 * WIDTH], lng_ref[...], lnb_ref[...])
    ws = _spatial_weights(ws_ref)
    gw = WIDTH // MLP_GROUPS
    s_rows = []
    for n in range(tb // MLP_CHUNK):
        r0 = n * MLP_CHUNK
        s_rows.append(jnp.concatenate(
            [_dot(ws[g], vn[r0:r0 + MLP_CHUNK, g * gw:(g + 1) * gw]) + bs_ref[g]
             for g in range(MLP_GROUPS)], axis=1))
    yb = bu * jnp.concatenate(s_rows, axis=0) * jax.nn.silu(pb[:, 2 * WIDTH:])
    inverse_level_and_filler()
    inverse_level_and_filler()
    oc = []
    for hd in range(HEADS):
        p = jnp.exp2((scores[hd] - jnp.max(scores[hd], axis=-1, keepdims=True))
                     * (HEAD_DIM ** -0.5 * math.log2(math.e)))
        oc.append(_dot(p, mv_ref[0, :, hd * HEAD_DIM:(hd + 1) * HEAD_DIM])
                  / jnp.sum(p, axis=-1, keepdims=True))
    yc = jnp.concatenate(oc, axis=1) * jax.nn.silu(pc[:, WIDTH:])
    inverse_level_and_filler()
    inverse_level_and_filler(last=True)
    bds = [_block_diag2(inv, F32) for inv in invs]
    crosses = [_dot(bd, terms[g]["off"]) for g, bd in zip(groups, bds)]
    fulls = [bd - _dot(cr, bd) for bd, cr in zip(bds, crosses)]
    uws = {g: _dot(full, terms[g]["rhs"]) for g, full in zip(groups, fulls)}

    s_heads = [state[hd] for hd in range(HEADS)]
    o_chunks = [[] for _ in range(HEADS)]
    per_stage = len(side) // (2 * npair)
    assert per_stage * 2 * npair == len(side)
    for j in range(npair):
        ws_qs = []
        for hd in range(HEADS):
            wq = jnp.concatenate([uws[j, hd][:, HEAD_DIM:], terms[j, hd]["qd"]], axis=0)
            ws_qs.append(_dot(wq, s_heads[hd]))
        for _ in range(per_stage):
            side.pop(0)()
        for hd in range(HEADS):
            tm = terms[j, hd]
            v_new = uws[j, hd][:, :HEAD_DIM] - ws_qs[hd][:n2]
            o_chunks[hd].append(ws_qs[hd][n2:] + _dot(tm["attn"], v_new))
            s_heads[hd] = s_heads[hd] * tm["gl"] + _dot(tm["kdt"], v_new)
        for _ in range(per_stage):
            side.pop(0)()
    for hd in range(HEADS):
        state[hd] = s_heads[hd]
    sd_ref[0] = state[...]

    o_heads = []
    for hd in range(HEADS):
        lo = hd * HEAD_DIM
        o = jnp.concatenate(o_chunks[hd], axis=0)
        o_heads.append(_rms(o) * anormg_ref[...] * jax.nn.silu(agate[:, lo:lo + HEAD_DIM]))
    branch_out = [_dot(jnp.concatenate(o_heads, axis=1), wbra_ref[...]),
                  jnp.concatenate(out_b, axis=1), jnp.concatenate(out_c, axis=1)]
    gates = jnp.concatenate(gate_parts, axis=1)
    merged = None
    for i in range(N_BRANCH):
        term = gates[:, i * D_MODEL:(i + 1) * D_MODEL] * branch_out[i]
        merged = term if merged is None else merged + term
    out = x_ref[0] + _dot(merged, wout_ref[...])
    y_ref[0] = _rms(out) * fng_ref[...]


def _prompt_layer(x, mkb, mvb, weights, params):
    bsz, seq, _ = x.shape
    tb = PROMPT_BLOCK
    nt = seq // tb
    full = lambda a: pl.BlockSpec(a.shape, lambda b, t: (0,) * a.ndim, pipeline_mode=pl.Buffered(1))
    in_specs = [
        pl.BlockSpec((1, tb, D_MODEL), lambda b, t: (b, t, 0)),
        pl.BlockSpec((1, MEM_LEN, WIDTH), lambda b, t: (b, 0, 0)),
        pl.BlockSpec((1, MEM_LEN, WIDTH), lambda b, t: (b, 0, 0)),
    ] + [full(w) for w in weights] + [full(p) for p in params]
    out_specs = [
        pl.BlockSpec((1, tb, D_MODEL), lambda b, t: (b, t, 0)),
        pl.BlockSpec((1, HEADS, HEAD_DIM, HEAD_DIM), lambda b, t: (b, 0, 0, 0)),
        pl.BlockSpec((1, CONV_WIDTH - 1, CONV_CH), lambda b, t: (b, 0, 0)),
    ]
    out_shape = [
        jax.ShapeDtypeStruct((bsz, seq, D_MODEL), F32),
        jax.ShapeDtypeStruct((bsz, HEADS, HEAD_DIM, HEAD_DIM), F32),
        jax.ShapeDtypeStruct((bsz, CONV_WIDTH - 1, CONV_CH), F32),
    ]
    return pl.pallas_call(
        _prompt_kernel,
        grid=(bsz, nt),
        in_specs=in_specs,
        out_specs=out_specs,
        out_shape=out_shape,
        scratch_shapes=[pltpu.VMEM((CONV_CH // LANES, tb + SUBLANES, LANES), F32),
                        pltpu.VMEM((HEADS, HEAD_DIM, HEAD_DIM), F32),
                        pltpu.VMEM((DELTA_CHUNK + tb, LANES), F32)],
        compiler_params=pltpu.CompilerParams(dimension_semantics=("arbitrary", "arbitrary"),
                                             vmem_limit_bytes=VMEM_LIMIT),
        name="prompt_layer",
    )(x, mkb, mvb, *weights, *params)


def _sample_kernel(x_ref, ck_ref, cv_ref, s0_ref, cb_ref, w_ref, wbg_ref,
                   normg_ref, convw_ref, alog_ref, dtb_ref, anormg_ref, lng_ref, lnb_ref, ws_ref, bs_ref,
                   wbra_ref, wbrb_ref, wbrc_ref, bgate_ref, wout_ref, fng_ref,
                   y_ref, sd_ref, sc_ref, vn_ref,
                   hb_scr, pre_scr, rest_scr, bg_scr, ya_scr, yc_scr):
    sb = SAMPLE_BLOCK
    i = pl.program_id(0)
    nsteps = pl.num_programs(0)

    @pl.when(i == 0)
    def _():
        hb = (_rms(x_ref[:, 0, :]) * normg_ref[...]).astype(BF16)
        hb_scr[...] = hb
        pre_scr[...] = _dot(hb, w_ref[:, :CONV_CH])
        rest_scr[...] = _dot(hb, _rest(w_ref, 0, REST_MG))
        bg_scr[...] = _dot(hb, wbg_ref[...])
        bu = rest_scr[:, REST_B:REST_B + WIDTH]
        vn = _layernorm(rest_scr[:, REST_B + WIDTH:REST_B + 2 * WIDTH], lng_ref[...], lnb_ref[...])
        vn_ref[:, 0, :] = vn
        bgate = rest_scr[:, REST_B + 2 * WIDTH:REST_B + 3 * WIDTH]
        gw = WIDTH // MLP_GROUPS
        s = jnp.concatenate(
            [vn[:, g * gw:(g + 1) * gw] * ws_ref[g, 0:1, 0:1] + bs_ref[g, 0:1, :]
             for g in range(MLP_GROUPS)], axis=1)
        rest_scr[:, REST_B:REST_B + WIDTH] = bu * s * jax.nn.silu(bgate)

    r0 = pl.multiple_of(i * sb, sb)
    rows = pl.ds(r0, sb)
    pre = pre_scr[rows, :]
    beta, g = _beta_and_log_decay(bg_scr[rows, :], alog_ref[...], dtb_ref[...])
    decay = jnp.exp(g)
    agate = rest_scr[rows, REST_AGATE:REST_AGATE + WIDTH]
    cq = rest_scr[rows, REST_C:REST_C + WIDTH]
    cgate = rest_scr[rows, REST_C + WIDTH:REST_C + 2 * WIDTH]

    own_head = ((_iota2((SUBLANES, MEM_LEN * HEADS), 1) % HEADS)
                == (_iota2((SUBLANES, MEM_LEN * HEADS), 0) % HEADS))

    scores = []
    for s in range(sb):
        qh = jnp.concatenate([cq[s:s + 1, hd * HEAD_DIM:(hd + 1) * HEAD_DIM] for hd in range(HEADS)]
                             + [jnp.zeros((SUBLANES - HEADS, HEAD_DIM), F32)], axis=0)
        scores.append(_dot_nt(qh, ck_ref[s]) * (HEAD_DIM ** -0.5))

    carried = [cb_ref[j] for j in range(CONV_WIDTH - 1)]
    conv = pre * convw_ref[CONV_WIDTH - 1:CONV_WIDTH, :]
    for j in range(CONV_WIDTH - 1):
        conv = conv + carried[j] * convw_ref[j:j + 1, :]
    for j in range(1, CONV_WIDTH - 1):
        sc_ref[j - 1] = carried[j]
    sc_ref[CONV_WIDTH - 2] = pre
    qkv = jax.nn.silu(conv)

    row8 = _iota2((sb, HEAD_DIM), 0)
    row16 = _iota2((2 * sb, HEAD_DIM), 0)
    qs, ks, vs, kq_s = [], [], [], []
    for hd in range(HEADS):
        lo = hd * HEAD_DIM
        qs.append(_l2norm(qkv[:, lo:lo + HEAD_DIM], HEAD_DIM ** -0.5))
        ks.append(_l2norm(qkv[:, WIDTH + lo:WIDTH + lo + HEAD_DIM]))
        vs.append(qkv[:, 2 * WIDTH + lo:2 * WIDTH + lo + HEAD_DIM])
        kq = jnp.concatenate([ks[hd], qs[hd]], axis=0).astype(BF16)
        kq_s.append([_dot(kq, s0_ref[s, hd]) for s in range(sb)])
    o_heads = []
    for hd in range(HEADS):
        lo = hd * HEAD_DIM
        sk = jnp.zeros((sb, HEAD_DIM), F32)
        sq = jnp.zeros((sb, HEAD_DIM), F32)
        for s in range(sb):
            sk = jnp.where(row8 == s, kq_s[hd][s][:sb], sk)
            sq = jnp.where(row8 == s, kq_s[hd][s][sb:], sq)
        a = decay[:, HEADS + hd:HEADS + hd + 1]
        v_new = beta[:, hd:hd + 1] * (vs[hd] - a * sk)
        qk = jnp.sum(qs[hd] * ks[hd], axis=-1, keepdims=True)
        o = a * sq + qk * v_new
        o_heads.append(_rms(o) * anormg_ref[...] * jax.nn.silu(agate[:, lo:lo + HEAD_DIM]))
        kt = jnp.concatenate([ks[hd], jnp.zeros((HEAD_DIM - sb, HEAD_DIM), F32)], axis=0).T.astype(BF16)
        v_pad = jnp.concatenate([v_new, jnp.zeros((sb, HEAD_DIM), F32)], axis=0)
        zeros = jnp.zeros((HEAD_DIM - 2 * sb, HEAD_DIM), BF16)
        for s in range(sb):
            only_s = jnp.concatenate([jnp.where(row16 == s, v_pad, 0.0).astype(BF16), zeros], axis=0)
            sd_ref[s, hd] = a[s:s + 1, :] * s0_ref[s, hd] + _dot(kt, only_s)
    ya_scr[rows, :] = jnp.concatenate(o_heads, axis=1)

    probs, sums = [], []
    for s in range(sb):
        sc = jnp.where(own_head, scores[s], -1e30)
        sc = sc - jnp.max(sc, axis=-1, keepdims=True)
        probs.append(jnp.where(own_head, jnp.exp(sc), 0.0))
        sums.append(jnp.sum(probs[s], axis=-1, keepdims=True))
    yc_rows = []
    for s in range(sb):
        oc = _dot(probs[s], cv_ref[s]) / sums[s]
        oc = jnp.concatenate([oc[hd:hd + 1, :] for hd in range(HEADS)], axis=1)
        yc_rows.append(oc * jax.nn.silu(cgate[s:s + 1, :]))
    yc_scr[rows, :] = jnp.concatenate(yc_rows, axis=0)

    @pl.when(i == nsteps - 1)
    def _():
        y_ref[:, 0, :] = _merge_and_project(x_ref[:, 0, :], hb_scr[...], ya_scr[...],
                                        rest_scr[:, REST_B:REST_B + WIDTH], yc_scr[...], w_ref,
                                        bgate_ref, wbra_ref, wbrb_ref, wbrc_ref, wout_ref, fng_ref)


def _sample_layer(x, cache_k, cache_v, s0, cb, weights, params):
    n = x.shape[0]
    sb = SAMPLE_BLOCK
    full = lambda a: pl.BlockSpec(a.shape, lambda i: (0,) * a.ndim, pipeline_mode=pl.Buffered(1))
    in_specs = [
        full(x),
        pl.BlockSpec((sb, MEM_LEN * HEADS, HEAD_DIM), lambda i: (i, 0, 0)),
        pl.BlockSpec((sb, MEM_LEN * HEADS, HEAD_DIM), lambda i: (i, 0, 0)),
        pl.BlockSpec((sb, HEADS, HEAD_DIM, HEAD_DIM), lambda i: (i, 0, 0, 0)),
        pl.BlockSpec((CONV_WIDTH - 1, sb, CONV_CH), lambda i: (0, i, 0)),
    ] + [full(w) for w in weights] + [full(p) for p in params]
    out_specs = [
        pl.BlockSpec((n, 1, D_MODEL), lambda i: (0, 0, 0)),
        pl.BlockSpec((sb, HEADS, HEAD_DIM, HEAD_DIM), lambda i: (i, 0, 0, 0)),
        pl.BlockSpec((CONV_WIDTH - 1, sb, CONV_CH), lambda i: (0, i, 0)),
        pl.BlockSpec((n, 1, WIDTH), lambda i: (0, 0, 0)),
    ]
    out_shape = [
        jax.ShapeDtypeStruct((n, 1, D_MODEL), F32),
        jax.ShapeDtypeStruct((n, HEADS, HEAD_DIM, HEAD_DIM), F32),
        jax.ShapeDtypeStruct((CONV_WIDTH - 1, n, CONV_CH), F32),
        jax.ShapeDtypeStruct((n, 1, WIDTH), F32),
    ]
    return pl.pallas_call(
        _sample_kernel,
        grid=(n // sb,),
        in_specs=in_specs,
        out_specs=out_specs,
        out_shape=out_shape,
        scratch_shapes=[pltpu.VMEM((n, D_MODEL), BF16),
                        pltpu.VMEM((n, CONV_CH), F32),
                        pltpu.VMEM((n, REST_MG), F32),
                        pltpu.VMEM((n, LANES), F32),
                        pltpu.VMEM((n, WIDTH), F32),
                        pltpu.VMEM((n, WIDTH), F32)],
        compiler_params=pltpu.CompilerParams(dimension_semantics=("arbitrary",),
                                             vmem_limit_bytes=VMEM_LIMIT),
        name="sample_layer",
    )(x, cache_k, cache_v, s0, cb, *weights, *params)


PACK_BLOCK = 1536


def _pack_kernel(a_ref, b_ref, o_ref, bg_ref):
    j = pl.program_id(0)
    first_rest = CONV_CH // PACK_BLOCK
    a = a_ref[...]
    skipped = jnp.concatenate([a[2 * HEADS:], b_ref[...]], axis=0)
    rows = jnp.where(j < first_rest, a, skipped)
    o_ref[...] = rows.T.astype(BF16)

    @pl.when(j == first_rest)
    def _():
        logits = jnp.concatenate([a[:2 * HEADS], jnp.zeros((LANES - 2 * HEADS, D_MODEL), F32)], axis=0)
        bg_ref[...] = logits.T.astype(BF16)


def _pack_input_projection(w_t):
    assert 2 * HEADS == SUBLANES and CONV_CH % PACK_BLOCK == 0 and PACKED_COLS % PACK_BLOCK == 0
    return pl.pallas_call(
        _pack_kernel,
        grid=(PACKED_COLS // PACK_BLOCK,),
        in_specs=[pl.BlockSpec((PACK_BLOCK, D_MODEL), lambda j: (j, 0)),
                  pl.BlockSpec((SUBLANES, D_MODEL), lambda j: ((j + 1) * (PACK_BLOCK // SUBLANES), 0))],
        out_specs=[pl.BlockSpec((D_MODEL, PACK_BLOCK), lambda j: (0, j)),
                   pl.BlockSpec((D_MODEL, LANES), lambda j: (0, 0))],
        out_shape=[jax.ShapeDtypeStruct((D_MODEL, PACKED_COLS), BF16),
                   jax.ShapeDtypeStruct((D_MODEL, LANES), BF16)],
        compiler_params=pltpu.CompilerParams(dimension_semantics=("arbitrary",),
                                             vmem_limit_bytes=VMEM_LIMIT),
        name="pack_input_projection",
    )(w_t, w_t)


def _lanes_4_to_7(vec):
    return jnp.zeros((1, LANES), F32).at[0, HEADS:2 * HEADS].set(vec)


def kernel(x_prompt, x_sample, cache_mem_k, cache_mem_v, state_delta, state_conv, mem_prompt, norm_g, w_in, conv_w, a_log, dt_bias, a_norm_g, ln_v_g, ln_v_b, w_spatial, b_spatial, mem_norm_g, w_mem_kv, w_br_a, w_br_b, w_br_c, b_gate, w_out, final_norm_g):
    depth = norm_g.shape[0]
    assert depth == 1, "single-layer step"
    bsz, seq, _ = x_prompt.shape
    nsmp = x_sample.shape[0]
    assert x_sample.shape[1] == 1 and seq % PROMPT_BLOCK == 0 and nsmp % SAMPLE_BLOCK == 0
    assert w_in.shape[2] == CONV_CH + 2 * HEADS + REST_COLS

    weights = _pack_input_projection(w_in[0].T)
    params = (
        norm_g[0][None, :], conv_w[0], _lanes_4_to_7(a_log[0]), _lanes_4_to_7(dt_bias[0]),
        a_norm_g[0][None, :], ln_v_g[0][None, :], ln_v_b[0][None, :], w_spatial[0],
        jnp.broadcast_to(b_spatial[0][:, :, None], (MLP_GROUPS, MLP_CHUNK, WIDTH // MLP_GROUPS)),
        w_br_a[0].astype(BF16), w_br_b[0].astype(BF16), w_br_c[0].astype(BF16),
        b_gate[0].reshape(1, N_BRANCH * D_MODEL), w_out[0].astype(BF16), final_norm_g[None, :],
    )

    mk, mv, mkb, mvb = _memory_kv(mem_prompt.reshape(bsz * MEM_LEN, D_MODEL), mem_norm_g[0][None, :],
                                  w_mem_kv[0])
    y_p, sd_p, sc_p = _prompt_layer(x_prompt, mkb.reshape(bsz, MEM_LEN, WIDTH),
                                    mvb.reshape(bsz, MEM_LEN, WIDTH), weights, params)
    y_s, sd_s, sc_s, vn_s = _sample_layer(
        x_sample, cache_mem_k.reshape(nsmp, MEM_LEN * HEADS, HEAD_DIM),
        cache_mem_v.reshape(nsmp, MEM_LEN * HEADS, HEAD_DIM), state_delta.reshape(state_delta.shape[1:]),
        jnp.transpose(state_conv[0], (1, 0, 2)), weights, params)

    kv_shape = (1, bsz, MEM_LEN, HEADS, HEAD_DIM)
    return (y_p, y_s, sd_p[None], sc_p[None], mk.reshape(kv_shape),
            mv.reshape(kv_shape), sd_s[None], jnp.transpose(sc_s, (1, 0, 2))[None], vn_s[None])
```

```python
import math

import jax
import jax.numpy as jnp
from jax import lax
from jax.experimental import pallas as pl
from jax.experimental.pallas import tpu as pltpu

F32 = jnp.float32
BF16 = jnp.bfloat16

D_MODEL = 1024
HEADS = 4
HEAD_DIM = 128
WIDTH = HEADS * HEAD_DIM
CONV_WIDTH = 4
CONV_CH = 3 * WIDTH
MLP_GROUPS = 4
MLP_CHUNK = 128
MEM_LEN = 256
N_BRANCH = 3
EPS = 1e-6

LANES = 128
SUBLANES = 8

REST_AGATE = 0
REST_B = REST_AGATE + WIDTH
REST_C = REST_B + 3 * WIDTH
REST_MG = REST_C + 2 * WIDTH
REST_COLS = REST_MG + N_BRANCH * D_MODEL
PACKED_COLS = CONV_CH + REST_COLS

PROMPT_BLOCK = 512
DELTA_CHUNK = 64
SAMPLE_BLOCK = 8
MEMKV_BLOCK = 512
MASKED = -1e30
VMEM_LIMIT = 56 * 1024 * 1024


def _dot(a, b):
    return jnp.dot(a.astype(BF16), b.astype(BF16), preferred_element_type=F32)


def _dot_nt(a, b):
    return lax.dot_general(a.astype(BF16), b.astype(BF16), (((1,), (1,)), ((), ())),
                           preferred_element_type=F32)


def _rms(x):
    return x * lax.rsqrt(jnp.mean(x * x, axis=-1, keepdims=True) + EPS)


def _l2norm(x, scale=1.0):
    return x * (lax.rsqrt(jnp.sum(x * x, axis=-1, keepdims=True) + EPS) * scale)


def _softplus(x):
    return jnp.maximum(x, 0.0) + jnp.log1p(jnp.exp(-jnp.abs(x)))


def _iota2(shape, dim):
    return lax.broadcasted_iota(jnp.int32, shape, dim)


def _memkv_kernel(mem_ref, g_ref, w_ref, k_ref, v_ref, kb_ref, vb_ref):
    xn = _rms(mem_ref[...]) * g_ref[...]
    kv = _dot(xn, w_ref[...])
    k = kv[:, :WIDTH]
    v = kv[:, WIDTH:]
    npos = k.shape[0]
    for hd in range(HEADS):
        k_ref[pl.ds(hd, npos, stride=HEADS), :] = k[:, hd * HEAD_DIM:(hd + 1) * HEAD_DIM]
        v_ref[pl.ds(hd, npos, stride=HEADS), :] = v[:, hd * HEAD_DIM:(hd + 1) * HEAD_DIM]
    kb_ref[...] = k.astype(BF16)
    vb_ref[...] = v.astype(BF16)


def _memory_kv(mem2d, mem_norm_g, w_mem_kv):
    rows = mem2d.shape[0]
    blk = MEMKV_BLOCK
    full = lambda shape: pl.BlockSpec(shape, lambda i: (0,) * len(shape))
    row_spec = lambda width: pl.BlockSpec((blk, width), lambda i: (i, 0))
    by_head = pl.BlockSpec((blk * HEADS, HEAD_DIM), lambda i: (i, 0))
    return pl.pallas_call(
        _memkv_kernel,
        grid=(rows // blk,),
        in_specs=[row_spec(D_MODEL), full((1, D_MODEL)), full((D_MODEL, 2 * WIDTH))],
        out_specs=[by_head] * 2 + [row_spec(WIDTH)] * 2,
        out_shape=[jax.ShapeDtypeStruct((rows * HEADS, HEAD_DIM), F32)] * 2
        + [jax.ShapeDtypeStruct((rows, WIDTH), BF16)] * 2,
        compiler_params=pltpu.CompilerParams(dimension_semantics=("arbitrary",),
                                             vmem_limit_bytes=VMEM_LIMIT),
        name="memory_kv",
    )(mem2d, mem_norm_g, w_mem_kv)


def _beta_and_log_decay(bg, alog, dtb):
    beta = jax.nn.sigmoid(bg)
    g = -jnp.exp(alog) * _softplus(bg + dtb)
    return beta, g


def _spatial_weights(ws_ref):
    tril = _iota2((MLP_CHUNK, MLP_CHUNK), 0) >= _iota2((MLP_CHUNK, MLP_CHUNK), 1)
    return [jnp.where(tril, ws_ref[g], 0.0).astype(BF16) for g in range(MLP_GROUPS)]


def _layernorm(v, g, b):
    mu = jnp.mean(v, axis=-1, keepdims=True)
    vc = v - mu
    return vc * lax.rsqrt(jnp.mean(vc * vc, axis=-1, keepdims=True) + EPS) * g + b


def _rest(w_ref, lo, hi):
    return w_ref[:, CONV_CH + lo:CONV_CH + hi]


def _merge_and_project(x, hb, ya, yb, yc, w_ref, bgate_ref, wbra_ref, wbrb_ref, wbrc_ref, wout_ref,
                       fng_ref):
    merged = None
    for i, (yi, wbr) in enumerate(((ya, wbra_ref), (yb, wbrb_ref), (yc, wbrc_ref))):
        lo = i * D_MODEL
        gate = jax.nn.sigmoid(_dot(hb, _rest(w_ref, REST_MG + lo, REST_MG + lo + D_MODEL))
                              + bgate_ref[:, lo:lo + D_MODEL])
        term = gate * _dot(yi, wbr[...])
        merged = term if merged is None else merged + term
    out = x + _dot(merged, wout_ref[...])
    return _rms(out) * fng_ref[...]


def _chunk_cumsum(g, ck, scr):
    n = g.shape[0]
    pad = ck // 2
    pos = _iota2(g.shape, 0) % ck
    scr[0:pad, :] = jnp.zeros((pad, LANES), F32)
    d = 1
    while d < ck:
        scr[pad:pad + n, :] = g
        g = g + jnp.where(pos >= d, scr[pad - d:pad - d + n, :], 0.0)
        d *= 2
    return g


def _delta_chunk_terms(q, k, v, beta_col, gc_col, gc_row):
    n = q.shape[0]
    half = n // 2
    ri = _iota2((n, n), 0)
    ci = _iota2((n, n), 1)
    same = (ri < half) == (ci < half)
    kt = k.T
    decay = jnp.exp(jnp.where(ri >= ci, gc_col - gc_row, MASKED))
    kb = k * beta_col
    egc = jnp.exp(gc_col)
    low = jnp.where(ri > ci, _dot(kb, kt) * decay, 0.0)
    diag = jnp.where(same, low, 0.0)
    last = gc_row[:, n - 1:n]
    return dict(
        neg_low=-(diag[:half] + diag[half:]),
        off=jnp.where(same, 0.0, low),
        attn=jnp.where(ri >= ci, _dot(q, kt) * decay, 0.0),
        rhs=jnp.concatenate([v * beta_col, kb * egc], axis=1),
        qd=q * egc,
        kdt=kt * jnp.exp(last - gc_row),
        gl=jnp.exp(last),
    )


def _block_diag2(packed, dtype=BF16):
    n = packed.shape[0]
    left = _iota2(packed.shape, 1) < n
    return jnp.concatenate([jnp.where(left, packed, 0.0), jnp.where(left, 0.0, packed)],
                           axis=0).astype(dtype)


def _prompt_kernel(x_ref, mk_ref, mv_ref, w_ref, wbg_ref, normg_ref, convw_ref,
                   alog_ref, dtb_ref, anormg_ref, lng_ref, lnb_ref, ws_ref, bs_ref, wbra_ref, wbrb_ref,
                   wbrc_ref, bgate_ref, wout_ref, fng_ref,
                   y_ref, sd_ref, sc_ref,
                   convbuf, state, cum_scr):
    tb = PROMPT_BLOCK
    ck = DELTA_CHUNK
    nck = tb // ck
    b = pl.program_id(0)
    t = pl.program_id(1)

    @pl.when((b == 0) & (t == 0))
    def _():
        sc_ref[...] = jnp.zeros_like(sc_ref)

    @pl.when(t == 0)
    def _():
        convbuf[:, 0:SUBLANES, :] = jnp.zeros((CONV_CH // LANES, SUBLANES, LANES), F32)
        state[...] = jnp.zeros_like(state)

    hb = (_rms(x_ref[0]) * normg_ref[...]).astype(BF16)

    pre = _dot(hb, w_ref[:, :CONV_CH])
    bg_col = _dot(hb, wbg_ref[...])
    pb = _dot(hb, _rest(w_ref, REST_B, REST_B + 3 * WIDTH))
    pc = _dot(hb, _rest(w_ref, REST_C, REST_C + 2 * WIDTH))
    agate = _dot(hb, _rest(w_ref, REST_AGATE, REST_AGATE + WIDTH))

    conv_tiles, tails = [], []
    for ct in range(CONV_CH // LANES):
        cols = slice(ct * LANES, (ct + 1) * LANES)
        convbuf[ct, SUBLANES:SUBLANES + tb, :] = pre[:, cols]
        acc = pre[:, cols] * convw_ref[CONV_WIDTH - 1:CONV_WIDTH, cols]
        for j in range(CONV_WIDTH - 1):
            shift = CONV_WIDTH - 1 - j
            acc = acc + convbuf[ct, SUBLANES - shift:SUBLANES - shift + tb, :] * convw_ref[j:j + 1, cols]
        conv_tiles.append(acc)
        tails.append(convbuf[ct, tb:tb + SUBLANES, :])
        convbuf[ct, 0:SUBLANES, :] = tails[ct]
    last_rows = jnp.concatenate(tails, axis=1)[SUBLANES - (CONV_WIDTH - 1):, :]
    mine = _iota2(sc_ref.shape[1:], 0) == b
    for j in range(CONV_WIDTH - 1):
        sc_ref[j] = jnp.where(mine, last_rows[j:j + 1, :], sc_ref[j])
    qkv = jax.nn.silu(jnp.concatenate(conv_tiles, axis=1))

    beta_col, g_col = _beta_and_log_decay(bg_col, alog_ref[...], dtb_ref[...])
    n2 = 2 * ck
    assert nck % 2 == 0 and n2 == LANES
    npair = nck // 2
    gc_col = _chunk_cumsum(g_col, n2, cum_scr)
    gc_cols = [gc_col[j * n2:(j + 1) * n2, :] for j in range(npair)]
    gc_rows = [gc.T[:SUBLANES, :] for gc in gc_cols]

    groups = [(j, hd) for j in range(npair) for hd in range(HEADS)]
    qn, kn, vs = [], [], []
    for hd in range(HEADS):
        lo = hd * HEAD_DIM
        qn.append(_l2norm(qkv[:, lo:lo + HEAD_DIM], HEAD_DIM ** -0.5))
        kn.append(_l2norm(qkv[:, WIDTH + lo:WIDTH + lo + HEAD_DIM]))
        vs.append(qkv[:, 2 * WIDTH + lo:2 * WIDTH + lo + HEAD_DIM])
    terms = {}
    for j, hd in groups:
        rows = slice(j * n2, (j + 1) * n2)
        terms[j, hd] = _delta_chunk_terms(
            qn[hd][rows], kn[hd][rows], vs[hd][rows], beta_col[rows, hd:hd + 1],
            gc_cols[j][:, HEADS + hd:HEADS + hd + 1], gc_rows[j][HEADS + hd:HEADS + hd + 1, :])

    eye2 = (_iota2((ck, n2), 0) == _iota2((ck, n2), 1) % ck).astype(F32)
    powers = [terms[g]["neg_low"] for g in groups]
    invs = [eye2 + p for p in powers]
    powers = [_dot(p, _block_diag2(p)) for p in powers]

    def inverse_level(last=False):
        nonlocal powers, invs
        if last:
            invs = [inv + _dot(inv, _block_diag2(p)) for inv, p in zip(invs, powers)]
            return
        prods = [_dot(jnp.concatenate([inv, p], axis=0), _block_diag2(p)) for inv, p in zip(invs, powers)]
        invs = [inv + r[:ck] for inv, r in zip(invs, prods)]
        powers = [r[ck:] for r in prods]

    levels = 0
    while 2 ** (levels + 1) < ck:
        levels += 1
    assert levels == 5, "the side work below is placed for five squaring levels"

    n_slots = 2 * nck
    gate_cols = 2 * D_MODEL // nck
    n_gate = N_BRANCH * D_MODEL // gate_cols
    out_parts = 4
    out_cols = D_MODEL // out_parts
    n_early = n_gate + 2 * out_parts - n_slots
    assert 0 <= n_early <= levels and D_MODEL % gate_cols == 0
    gate_parts = []
    out_b, out_c = [], []

    def gate_piece(j):
        lo = j * gate_cols
        gate_parts.append(jax.nn.sigmoid(_dot(hb, _rest(w_ref, REST_MG + lo, REST_MG + lo + gate_cols))
                                         + bgate_ref[:, lo:lo + gate_cols]))

    side = [lambda j=j: gate_piece(j) for j in range(n_gate)]
    side += [lambda j=j: out_b.append(_dot(yb, wbrb_ref[:, j * out_cols:(j + 1) * out_cols]))
             for j in range(out_parts)]
    side += [lambda j=j: out_c.append(_dot(yc, wbrc_ref[:, j * out_cols:(j + 1) * out_cols]))
             for j in range(out_parts)]
    early, side = side[:n_early], side[n_early:]

    def inverse_level_and_filler(last=False):
        inverse_level(last)
        if early:
            early.pop(0)()

    scores = [_dot_nt(pc[:, hd * HEAD_DIM:(hd + 1) * HEAD_DIM], mk_ref[0, :, hd * HEAD_DIM:(hd + 1) * HEAD_DIM])
              for hd in range(HEADS)]
    inverse_level_and_filler()
    bu = pb[:, :WIDTH]
    vn = _layernorm(pb[:, WIDTH:2 * WIDTH], lng_ref[...], lnb_ref[...])
    ws = _spatial_weights(ws_ref)
    gw = WIDTH // MLP_GROUPS
    s_rows = []
    for n in range(tb // MLP_CHUNK):
        r0 = n * MLP_CHUNK
        s_rows.append(jnp.concatenate(
            [_dot(ws[g], vn[r0:r0 + MLP_CHUNK, g * gw:(g + 1) * gw]) + bs_ref[g]
             for g in range(MLP_GROUPS)], axis=1))
    yb = bu * jnp.concatenate(s_rows, axis=0) * jax.nn.silu(pb[:, 2 * WIDTH:])
    inverse_level_and_filler()
    inverse_level_and_filler()
    oc = []
    for hd in range(HEADS):
        p = jnp.exp2((scores[hd] - jnp.max(scores[hd], axis=-1, keepdims=True))
                     * (HEAD_DIM ** -0.5 * math.log2(math.e)))
        oc.append(_dot(p, mv_ref[0, :, hd * HEAD_DIM:(hd + 1) * HEAD_DIM])
                  / jnp.sum(p, axis=-1, keepdims=True))
    yc = jnp.concatenate(oc, axis=1) * jax.nn.silu(pc[:, WIDTH:])
    inverse_level_and_filler()
    inverse_level_and_filler(last=True)
    bds = [_block_diag2(inv, F32) for inv in invs]
    crosses = [_dot(bd, terms[g]["off"]) for g, bd in zip(groups, bds)]
    fulls = [bd - _dot(cr, bd) for bd, cr in zip(bds, crosses)]
    uws = {g: _dot(full, terms[g]["rhs"]) for g, full in zip(groups, fulls)}

    s_heads = [state[hd] for hd in range(HEADS)]
    o_chunks = [[] for _ in range(HEADS)]
    per_stage = len(side) // (2 * npair)
    assert per_stage * 2 * npair == len(side)
    for j in range(npair):
        ws_qs = []
        for hd in range(HEADS):
            wq = jnp.concatenate([uws[j, hd][:, HEAD_DIM:], terms[j, hd]["qd"]], axis=0)
            ws_qs.append(_dot(wq, s_heads[hd]))
        for _ in range(per_stage):
            side.pop(0)()
        for hd in range(HEADS):
            tm = terms[j, hd]
            v_new = uws[j, hd][:, :HEAD_DIM] - ws_qs[hd][:n2]
            o_chunks[hd].append(ws_qs[hd][n2:] + _dot(tm["attn"], v_new))
            s_heads[hd] = s_heads[hd] * tm["gl"] + _dot(tm["kdt"], v_new)
        for _ in range(per_stage):
            side.pop(0)()
    for hd in range(HEADS):
        state[hd] = s_heads[hd]
    sd_ref[0] = state[...]

    o_heads = []
    for hd in range(HEADS):
        lo = hd * HEAD_DIM
        o = jnp.concatenate(o_chunks[hd], axis=0)
        o_heads.append(_rms(o) * anormg_ref[...] * jax.nn.silu(agate[:, lo:lo + HEAD_DIM]))
    branch_out = [_dot(jnp.concatenate(o_heads, axis=1), wbra_ref[...]),
                  jnp.concatenate(out_b, axis=1), jnp.concatenate(out_c, axis=1)]
    gates = jnp.concatenate(gate_parts, axis=1)
    merged = None
    for i in range(N_BRANCH):
        term = gates[:, i * D_MODEL:(i + 1) * D_MODEL] * branch_out[i]
        merged = term if merged is None else merged + term
    out = x_ref[0] + _dot(merged, wout_ref[...])
    y_ref[0] = _rms(out) * fng_ref[...]


def _prompt_layer(x, mkb, mvb, weights, params):
    bsz, seq, _ = x.shape
    tb = PROMPT_BLOCK
    nt = seq // tb
    full = lambda a: pl.BlockSpec(a.shape, lambda b, t: (0,) * a.ndim, pipeline_mode=pl.Buffered(1))
    in_specs = [
        pl.BlockSpec((1, tb, D_MODEL), lambda b, t: (b, t, 0)),
        pl.BlockSpec((1, MEM_LEN, WIDTH), lambda b, t: (b, 0, 0)),
        pl.BlockSpec((1, MEM_LEN, WIDTH), lambda b, t: (b, 0, 0)),
    ] + [full(w) for w in weights] + [full(p) for p in params]
    out_specs = [
        pl.BlockSpec((1, tb, D_MODEL), lambda b, t: (b, t, 0)),
        pl.BlockSpec((1, HEADS, HEAD_DIM, HEAD_DIM), lambda b, t: (b, 0, 0, 0)),
        pl.BlockSpec((CONV_WIDTH - 1, bsz, CONV_CH), lambda b, t: (0, 0, 0)),
    ]
    out_shape = [
        jax.ShapeDtypeStruct((bsz, seq, D_MODEL), F32),
        jax.ShapeDtypeStruct((bsz, HEADS, HEAD_DIM, HEAD_DIM), F32),
        jax.ShapeDtypeStruct((CONV_WIDTH - 1, bsz, CONV_CH), F32),
    ]
    return pl.pallas_call(
        _prompt_kernel,
        grid=(bsz, nt),
        in_specs=in_specs,
        out_specs=out_specs,
        out_shape=out_shape,
        scratch_shapes=[pltpu.VMEM((CONV_CH // LANES, tb + SUBLANES, LANES), F32),
                        pltpu.VMEM((HEADS, HEAD_DIM, HEAD_DIM), F32),
                        pltpu.VMEM((DELTA_CHUNK + tb, LANES), F32)],
        compiler_params=pltpu.CompilerParams(dimension_semantics=("arbitrary", "arbitrary"),
                                             vmem_limit_bytes=VMEM_LIMIT),
        name="prompt_layer",
    )(x, mkb, mvb, *weights, *params)


def _sample_kernel(x_ref, ck_ref, cv_ref, s0_ref, cb_ref, w_ref, wbg_ref,
                   normg_ref, convw_ref, alog_ref, dtb_ref, anormg_ref, lng_ref, lnb_ref, ws_ref, bs_ref,
                   wbra_ref, wbrb_ref, wbrc_ref, bgate_ref, wout_ref, fng_ref,
                   y_ref, sd_ref, sc_ref, vn_ref,
                   hb_scr, pre_scr, rest_scr, bg_scr, ya_scr, yc_scr):
    sb = SAMPLE_BLOCK
    i = pl.program_id(0)
    nsteps = pl.num_programs(0)

    @pl.when(i == 0)
    def _():
        hb = (_rms(x_ref[:, 0, :]) * normg_ref[...]).astype(BF16)
        hb_scr[...] = hb
        pre_scr[...] = _dot(hb, w_ref[:, :CONV_CH])
        rest_scr[...] = _dot(hb, _rest(w_ref, 0, REST_MG))
        bg_scr[...] = _dot(hb, wbg_ref[...])
        bu = rest_scr[:, REST_B:REST_B + WIDTH]
        vn = _layernorm(rest_scr[:, REST_B + WIDTH:REST_B + 2 * WIDTH], lng_ref[...], lnb_ref[...])
        vn_ref[:, 0, :] = vn
        bgate = rest_scr[:, REST_B + 2 * WIDTH:REST_B + 3 * WIDTH]
        gw = WIDTH // MLP_GROUPS
        s = jnp.concatenate(
            [vn[:, g * gw:(g + 1) * gw] * ws_ref[g, 0:1, 0:1] + bs_ref[g, 0:1, :]
             for g in range(MLP_GROUPS)], axis=1)
        rest_scr[:, REST_B:REST_B + WIDTH] = bu * s * jax.nn.silu(bgate)

    r0 = pl.multiple_of(i * sb, sb)
    rows = pl.ds(r0, sb)
    pre = pre_scr[rows, :]
    beta, g = _beta_and_log_decay(bg_scr[rows, :], alog_ref[...], dtb_ref[...])
    decay = jnp.exp(g)
    agate = rest_scr[rows, REST_AGATE:REST_AGATE + WIDTH]
    cq = rest_scr[rows, REST_C:REST_C + WIDTH]
    cgate = rest_scr[rows, REST_C + WIDTH:REST_C + 2 * WIDTH]

    own_head = ((_iota2((SUBLANES, MEM_LEN * HEADS), 1) % HEADS)
                == (_iota2((SUBLANES, MEM_LEN * HEADS), 0) % HEADS))

    scores = []
    for s in range(sb):
        qh = jnp.concatenate([cq[s:s + 1, hd * HEAD_DIM:(hd + 1) * HEAD_DIM] for hd in range(HEADS)]
                             + [jnp.zeros((SUBLANES - HEADS, HEAD_DIM), F32)], axis=0)
        scores.append(_dot_nt(qh, ck_ref[s]) * (HEAD_DIM ** -0.5))

    carried = [cb_ref[j] for j in range(CONV_WIDTH - 1)]
    conv = pre * convw_ref[CONV_WIDTH - 1:CONV_WIDTH, :]
    for j in range(CONV_WIDTH - 1):
        conv = conv + carried[j] * convw_ref[j:j + 1, :]
    for j in range(1, CONV_WIDTH - 1):
        sc_ref[j - 1] = carried[j]
    sc_ref[CONV_WIDTH - 2] = pre
    qkv = jax.nn.silu(conv)

    row8 = _iota2((sb, HEAD_DIM), 0)
    row16 = _iota2((2 * sb, HEAD_DIM), 0)
    qs, ks, vs, kq_s = [], [], [], []
    for hd in range(HEADS):
        lo = hd * HEAD_DIM
        qs.append(_l2norm(qkv[:, lo:lo + HEAD_DIM], HEAD_DIM ** -0.5))
        ks.append(_l2norm(qkv[:, WIDTH + lo:WIDTH + lo + HEAD_DIM]))
        vs.append(qkv[:, 2 * WIDTH + lo:2 * WIDTH + lo + HEAD_DIM])
        kq = jnp.concatenate([ks[hd], qs[hd]], axis=0).astype(BF16)
        kq_s.append([_dot(kq, s0_ref[s, hd]) for s in range(sb)])
    o_heads = []
    for hd in range(HEADS):
        lo = hd * HEAD_DIM
        sk = jnp.zeros((sb, HEAD_DIM), F32)
        sq = jnp.zeros((sb, HEAD_DIM), F32)
        for s in range(sb):
            sk = jnp.where(row8 == s, kq_s[hd][s][:sb], sk)
            sq = jnp.where(row8 == s, kq_s[hd][s][sb:], sq)
        a = decay[:, HEADS + hd:HEADS + hd + 1]
        v_new = beta[:, hd:hd + 1] * (vs[hd] - a * sk)
        qk = jnp.sum(qs[hd] * ks[hd], axis=-1, keepdims=True)
        o = a * sq + qk * v_new
        o_heads.append(_rms(o) * anormg_ref[...] * jax.nn.silu(agate[:, lo:lo + HEAD_DIM]))
        kt = jnp.concatenate([ks[hd], jnp.zeros((HEAD_DIM - sb, HEAD_DIM), F32)], axis=0).T.astype(BF16)
        v_pad = jnp.concatenate([v_new, jnp.zeros((sb, HEAD_DIM), F32)], axis=0)
        zeros = jnp.zeros((HEAD_DIM - 2 * sb, HEAD_DIM), BF16)
        for s in range(sb):
            only_s = jnp.concatenate([jnp.where(row16 == s, v_pad, 0.0).astype(BF16), zeros], axis=0)
            sd_ref[s, hd] = a[s:s + 1, :] * s0_ref[s, hd] + _dot(kt, only_s)
    ya_scr[rows, :] = jnp.concatenate(o_heads, axis=1)

    probs, sums = [], []
    for s in range(sb):
        sc = jnp.where(own_head, scores[s], MASKED)
        sc = sc - jnp.max(sc, axis=-1, keepdims=True)
        probs.append(jnp.where(own_head, jnp.exp(sc), 0.0))
        sums.append(jnp.sum(probs[s], axis=-1, keepdims=True))
    yc_rows = []
    for s in range(sb):
        oc = _dot(probs[s], cv_ref[s]) / sums[s]
        oc = jnp.concatenate([oc[hd:hd + 1, :] for hd in range(HEADS)], axis=1)
        yc_rows.append(oc * jax.nn.silu(cgate[s:s + 1, :]))
    yc_scr[rows, :] = jnp.concatenate(yc_rows, axis=0)

    @pl.when(i == nsteps - 1)
    def _():
        y_ref[:, 0, :] = _merge_and_project(x_ref[:, 0, :], hb_scr[...], ya_scr[...],
                                        rest_scr[:, REST_B:REST_B + WIDTH], yc_scr[...], w_ref,
                                        bgate_ref, wbra_ref, wbrb_ref, wbrc_ref, wout_ref, fng_ref)


def _sample_layer(x, cache_k, cache_v, s0, cb, weights, params):
    n = x.shape[0]
    sb = SAMPLE_BLOCK
    full = lambda a: pl.BlockSpec(a.shape, lambda i: (0,) * a.ndim, pipeline_mode=pl.Buffered(1))
    in_specs = [
        full(x),
        pl.BlockSpec((sb, MEM_LEN * HEADS, HEAD_DIM), lambda i: (i, 0, 0)),
        pl.BlockSpec((sb, MEM_LEN * HEADS, HEAD_DIM), lambda i: (i, 0, 0)),
        pl.BlockSpec((sb, HEADS, HEAD_DIM, HEAD_DIM), lambda i: (i, 0, 0, 0)),
        pl.BlockSpec((CONV_WIDTH - 1, sb, CONV_CH), lambda i: (0, i, 0)),
    ] + [full(w) for w in weights] + [full(p) for p in params]
    out_specs = [
        pl.BlockSpec((n, 1, D_MODEL), lambda i: (0, 0, 0)),
        pl.BlockSpec((sb, HEADS, HEAD_DIM, HEAD_DIM), lambda i: (i, 0, 0, 0)),
        pl.BlockSpec((CONV_WIDTH - 1, sb, CONV_CH), lambda i: (0, i, 0)),
        pl.BlockSpec((n, 1, WIDTH), lambda i: (0, 0, 0)),
    ]
    out_shape = [
        jax.ShapeDtypeStruct((n, 1, D_MODEL), F32),
        jax.ShapeDtypeStruct((n, HEADS, HEAD_DIM, HEAD_DIM), F32),
        jax.ShapeDtypeStruct((CONV_WIDTH - 1, n, CONV_CH), F32),
        jax.ShapeDtypeStruct((n, 1, WIDTH), F32),
    ]
    return pl.pallas_call(
        _sample_kernel,
        grid=(n // sb,),
        in_specs=in_specs,
        out_specs=out_specs,
        out_shape=out_shape,
        scratch_shapes=[pltpu.VMEM((n, D_MODEL), BF16),
                        pltpu.VMEM((n, CONV_CH), F32),
                        pltpu.VMEM((n, REST_MG), F32),
                        pltpu.VMEM((n, LANES), F32),
                        pltpu.VMEM((n, WIDTH), F32),
                        pltpu.VMEM((n, WIDTH), F32)],
        compiler_params=pltpu.CompilerParams(dimension_semantics=("arbitrary",),
                                             vmem_limit_bytes=VMEM_LIMIT),
        name="sample_layer",
    )(x, cache_k, cache_v, s0, cb, *weights, *params)


PACK_BLOCK = 1536


def _pack_kernel(a_ref, b_ref, o_ref, bg_ref):
    j = pl.program_id(0)
    first_rest = CONV_CH // PACK_BLOCK
    a = a_ref[...]
    skipped = jnp.concatenate([a[2 * HEADS:], b_ref[...]], axis=0)
    rows = jnp.where(j < first_rest, a, skipped)
    o_ref[...] = rows.T.astype(BF16)

    @pl.when(j == first_rest)
    def _():
        logits = jnp.concatenate([a[:2 * HEADS], jnp.zeros((LANES - 2 * HEADS, D_MODEL), F32)], axis=0)
        bg_ref[...] = logits.T.astype(BF16)


def _pack_input_projection(w_t):
    assert 2 * HEADS == SUBLANES and CONV_CH % PACK_BLOCK == 0 and PACKED_COLS % PACK_BLOCK == 0
    return pl.pallas_call(
        _pack_kernel,
        grid=(PACKED_COLS // PACK_BLOCK,),
        in_specs=[pl.BlockSpec((PACK_BLOCK, D_MODEL), lambda j: (j, 0)),
                  pl.BlockSpec((SUBLANES, D_MODEL), lambda j: ((j + 1) * (PACK_BLOCK // SUBLANES), 0))],
        out_specs=[pl.BlockSpec((D_MODEL, PACK_BLOCK), lambda j: (0, j)),
                   pl.BlockSpec((D_MODEL, LANES), lambda j: (0, 0))],
        out_shape=[jax.ShapeDtypeStruct((D_MODEL, PACKED_COLS), BF16),
                   jax.ShapeDtypeStruct((D_MODEL, LANES), BF16)],
        compiler_params=pltpu.CompilerParams(dimension_semantics=("arbitrary",),
                                             vmem_limit_bytes=VMEM_LIMIT),
        name="pack_input_projection",
    )(w_t, w_t)


def _lanes_4_to_7(vec):
    return jnp.zeros((1, LANES), F32).at[0, HEADS:2 * HEADS].set(vec)


def kernel(x_prompt, x_sample, cache_mem_k, cache_mem_v, state_delta, state_conv, mem_prompt, norm_g, w_in, conv_w, a_log, dt_bias, a_norm_g, ln_v_g, ln_v_b, w_spatial, b_spatial, mem_norm_g, w_mem_kv, w_br_a, w_br_b, w_br_c, b_gate, w_out, final_norm_g):
    depth = norm_g.shape[0]
    assert depth == 1, "single-layer step"
    bsz, seq, _ = x_prompt.shape
    nsmp = x_sample.shape[0]
    assert x_sample.shape[1] == 1 and seq % PROMPT_BLOCK == 0 and nsmp % SAMPLE_BLOCK == 0
    assert w_in.shape[2] == CONV_CH + 2 * HEADS + REST_COLS

    weights = _pack_input_projection(w_in[0].T)
    params = (
        norm_g[0][None, :], conv_w[0], _lanes_4_to_7(a_log[0]), _lanes_4_to_7(dt_bias[0]),
        a_norm_g[0][None, :], ln_v_g[0][None, :], ln_v_b[0][None, :], w_spatial[0],
        jnp.broadcast_to(b_spatial[0][:, :, None], (MLP_GROUPS, MLP_CHUNK, WIDTH // MLP_GROUPS)),
        w_br_a[0].astype(BF16), w_br_b[0].astype(BF16), w_br_c[0].astype(BF16),
        b_gate[0].reshape(1, N_BRANCH * D_MODEL), w_out[0].astype(BF16), final_norm_g[None, :],
    )

    mk, mv, mkb, mvb = _memory_kv(mem_prompt.reshape(bsz * MEM_LEN, D_MODEL), mem_norm_g[0][None, :],
                                  w_mem_kv[0])
    y_p, sd_p, sc_p = _prompt_layer(x_prompt, mkb.reshape(bsz, MEM_LEN, WIDTH),
                                    mvb.reshape(bsz, MEM_LEN, WIDTH), weights, params)
    y_s, sd_s, sc_s, vn_s = _sample_layer(
        x_sample, cache_mem_k.reshape(nsmp, MEM_LEN * HEADS, HEAD_DIM),
        cache_mem_v.reshape(nsmp, MEM_LEN * HEADS, HEAD_DIM), state_delta.reshape(state_delta.shape[1:]),
        jnp.transpose(state_conv[0], (1, 0, 2)), weights, params)

    kv_shape = (1, bsz, MEM_LEN, HEADS, HEAD_DIM)
    return (y_p, y_s, sd_p[None], jnp.transpose(sc_p, (1, 0, 2))[None], mk.reshape(kv_shape),
            mv.reshape(kv_shape), sd_s[None], jnp.transpose(sc_s, (1, 0, 2))[None], vn_s[None])
```

```python
import math

import jax
import jax.numpy as jnp
from jax import lax
from jax.experimental import pallas as pl
from jax.experimental.pallas import tpu as pltpu

F32 = jnp.float32
BF16 = jnp.bfloat16

D_MODEL = 1024
HEADS = 4
HEAD_DIM = 128
WIDTH = HEADS * HEAD_DIM
CONV_WIDTH = 4
CONV_CH = 3 * WIDTH
MLP_GROUPS = 4
MLP_CHUNK = 128
MEM_LEN = 256
N_BRANCH = 3
EPS = 1e-6

LANES = 128
SUBLANES = 8

REST_AGATE = 0
REST_B = REST_AGATE + WIDTH
REST_C = REST_B + 3 * WIDTH
REST_MG = REST_C + 2 * WIDTH
REST_COLS = REST_MG + N_BRANCH * D_MODEL
PACKED_COLS = CONV_CH + REST_COLS

PROMPT_BLOCK = 512
DELTA_CHUNK = 64
SAMPLE_BLOCK = 8
MEMKV_BLOCK = 512
MASKED = -1e30
VMEM_LIMIT = 56 * 1024 * 1024


def _dot(a, b):
    return jnp.dot(a.astype(BF16), b.astype(BF16), preferred_element_type=F32)


def _dot_nt(a, b):
    return lax.dot_general(a.astype(BF16), b.astype(BF16), (((1,), (1,)), ((), ())),
                           preferred_element_type=F32)


def _rms(x):
    return x * lax.rsqrt(jnp.mean(x * x, axis=-1, keepdims=True) + EPS)


def _l2norm(x, scale=1.0):
    return x * (lax.rsqrt(jnp.sum(x * x, axis=-1, keepdims=True) + EPS) * scale)


def _softplus(x):
    return jnp.maximum(x, 0.0) + jnp.log1p(jnp.exp(-jnp.abs(x)))


def _iota2(shape, dim):
    return lax.broadcasted_iota(jnp.int32, shape, dim)


def _memkv_kernel(mem_ref, g_ref, w_ref, k_ref, v_ref, kb_ref, vb_ref):
    xn = _rms(mem_ref[...]) * g_ref[...]
    kv = _dot(xn, w_ref[...])
    k = kv[:, :WIDTH]
    v = kv[:, WIDTH:]
    npos = k.shape[0]
    for hd in range(HEADS):
        k_ref[pl.ds(hd, npos, stride=HEADS), :] = k[:, hd * HEAD_DIM:(hd + 1) * HEAD_DIM]
        v_ref[pl.ds(hd, npos, stride=HEADS), :] = v[:, hd * HEAD_DIM:(hd + 1) * HEAD_DIM]
    kb_ref[...] = k.astype(BF16)
    vb_ref[...] = v.astype(BF16)


def _memory_kv(mem2d, mem_norm_g, w_mem_kv):
    rows = mem2d.shape[0]
    blk = MEMKV_BLOCK
    full = lambda shape: pl.BlockSpec(shape, lambda i: (0,) * len(shape))
    row_spec = lambda width: pl.BlockSpec((blk, width), lambda i: (i, 0))
    by_head = pl.BlockSpec((blk * HEADS, HEAD_DIM), lambda i: (i, 0))
    return pl.pallas_call(
        _memkv_kernel,
        grid=(rows // blk,),
        in_specs=[row_spec(D_MODEL), full((1, D_MODEL)), full((D_MODEL, 2 * WIDTH))],
        out_specs=[by_head] * 2 + [row_spec(WIDTH)] * 2,
        out_shape=[jax.ShapeDtypeStruct((rows * HEADS, HEAD_DIM), F32)] * 2
        + [jax.ShapeDtypeStruct((rows, WIDTH), BF16)] * 2,
        compiler_params=pltpu.CompilerParams(dimension_semantics=("arbitrary",),
                                             vmem_limit_bytes=VMEM_LIMIT),
        name="memory_kv",
    )(mem2d, mem_norm_g, w_mem_kv)


def _beta_and_log_decay(bg, alog, dtb):
    beta = jax.nn.sigmoid(bg)
    g = -jnp.exp(alog) * _softplus(bg + dtb)
    return beta, g


def _spatial_weights(ws_ref):
    tril = _iota2((MLP_CHUNK, MLP_CHUNK), 0) >= _iota2((MLP_CHUNK, MLP_CHUNK), 1)
    return [jnp.where(tril, ws_ref[g], 0.0).astype(BF16) for g in range(MLP_GROUPS)]


def _layernorm(v, g, b):
    mu = jnp.mean(v, axis=-1, keepdims=True)
    vc = v - mu
    return vc * lax.rsqrt(jnp.mean(vc * vc, axis=-1, keepdims=True) + EPS) * g + b


def _rest(w_ref, lo, hi):
    return w_ref[:, CONV_CH + lo:CONV_CH + hi]


def _merge_and_project(x, hb, ya, yb, yc, w_ref, bgate_ref, wbra_ref, wbrb_ref, wbrc_ref, wout_ref,
                       fng_ref):
    merged = None
    for i, (yi, wbr) in enumerate(((ya, wbra_ref), (yb, wbrb_ref), (yc, wbrc_ref))):
        lo = i * D_MODEL
        gate = jax.nn.sigmoid(_dot(hb, _rest(w_ref, REST_MG + lo, REST_MG + lo + D_MODEL))
                              + bgate_ref[:, lo:lo + D_MODEL])
        term = gate * _dot(yi, wbr[...])
        merged = term if merged is None else merged + term
    out = x + _dot(merged, wout_ref[...])
    return _rms(out) * fng_ref[...]


def _chunk_cumsum(g, ck, scr):
    n = g.shape[0]
    pad = ck // 2
    pos = _iota2(g.shape, 0) % ck
    scr[0:pad, :] = jnp.zeros((pad, LANES), F32)
    d = 1
    while d < ck:
        scr[pad:pad + n, :] = g
        g = g + jnp.where(pos >= d, scr[pad - d:pad - d + n, :], 0.0)
        d *= 2
    return g


def _delta_chunk_terms(q, k, v, beta_col, gc_col, gc_row):
    n = q.shape[0]
    half = n // 2
    ri = _iota2((n, n), 0)
    ci = _iota2((n, n), 1)
    same = (ri < half) == (ci < half)
    kt = k.T
    decay = jnp.exp(jnp.where(ri >= ci, gc_col - gc_row, MASKED))
    kb = k * beta_col
    egc = jnp.exp(gc_col)
    low = jnp.where(ri > ci, _dot(kb, kt) * decay, 0.0)
    diag = jnp.where(same, low, 0.0)
    last = gc_row[:, n - 1:n]
    return dict(
        neg_low=-(diag[:half] + diag[half:]),
        off=jnp.where(same, 0.0, low),
        attn=jnp.where(ri >= ci, _dot(q, kt) * decay, 0.0),
        rhs=jnp.concatenate([v * beta_col, kb * egc], axis=1),
        qd=q * egc,
        kdt=kt * jnp.exp(last - gc_row),
        gl=jnp.exp(last),
    )


def _block_diag2(packed, dtype=BF16):
    n = packed.shape[0]
    left = _iota2(packed.shape, 1) < n
    return jnp.concatenate([jnp.where(left, packed, 0.0), jnp.where(left, 0.0, packed)],
                           axis=0).astype(dtype)


def _prompt_kernel(x_ref, mk_ref, mv_ref, w_ref, wbg_ref, normg_ref, convw_ref,
                   alog_ref, dtb_ref, anormg_ref, lng_ref, lnb_ref, ws_ref, bs_ref, wbra_ref, wbrb_ref,
                   wbrc_ref, bgate_ref, wout_ref, fng_ref,
                   y_ref, sd_ref, sc_ref,
                   convbuf, state, cum_scr):
    tb = PROMPT_BLOCK
    ck = DELTA_CHUNK
    nck = tb // ck
    b = pl.program_id(0)
    t = pl.program_id(1)

    @pl.when((b == 0) & (t == 0))
    def _():
        sc_ref[...] = jnp.zeros_like(sc_ref)

    @pl.when(t == 0)
    def _():
        convbuf[:, 0:SUBLANES, :] = jnp.zeros((CONV_CH // LANES, SUBLANES, LANES), F32)
        state[...] = jnp.zeros_like(state)

    hb = (_rms(x_ref[0]) * normg_ref[...]).astype(BF16)

    pre = _dot(hb, w_ref[:, :CONV_CH])
    bg_col = _dot(hb, wbg_ref[...])
    pb = _dot(hb, _rest(w_ref, REST_B, REST_B + 3 * WIDTH))
    pc = _dot(hb, _rest(w_ref, REST_C, REST_C + 2 * WIDTH))
    agate = _dot(hb, _rest(w_ref, REST_AGATE, REST_AGATE + WIDTH))

    conv_tiles, tails = [], []
    for ct in range(CONV_CH // LANES):
        cols = slice(ct * LANES, (ct + 1) * LANES)
        convbuf[ct, SUBLANES:SUBLANES + tb, :] = pre[:, cols]
        acc = pre[:, cols] * convw_ref[CONV_WIDTH - 1:CONV_WIDTH, cols]
        for j in range(CONV_WIDTH - 1):
            shift = CONV_WIDTH - 1 - j
            acc = acc + convbuf[ct, SUBLANES - shift:SUBLANES - shift + tb, :] * convw_ref[j:j + 1, cols]
        conv_tiles.append(acc)
        tails.append(convbuf[ct, tb:tb + SUBLANES, :])
        convbuf[ct, 0:SUBLANES, :] = tails[ct]
    last_rows = jnp.concatenate(tails, axis=1)[SUBLANES - (CONV_WIDTH - 1):, :]
    mine = _iota2(sc_ref.shape[1:], 0) == b
    for j in range(CONV_WIDTH - 1):
        sc_ref[j] = jnp.where(mine, last_rows[j:j + 1, :], sc_ref[j])
    qkv = jax.nn.silu(jnp.concatenate(conv_tiles, axis=1))

    beta_col, g_col = _beta_and_log_decay(bg_col, alog_ref[...], dtb_ref[...])
    n2 = 2 * ck
    assert nck % 2 == 0 and n2 == LANES
    npair = nck // 2
    gc_col = _chunk_cumsum(g_col, n2, cum_scr)
    gc_cols = [gc_col[j * n2:(j + 1) * n2, :] for j in range(npair)]
    gc_rows = [gc.T[:SUBLANES, :] for gc in gc_cols]

    groups = [(j, hd) for j in range(npair) for hd in range(HEADS)]
    qn, kn, vs = [], [], []
    for hd in range(HEADS):
        lo = hd * HEAD_DIM
        qn.append(_l2norm(qkv[:, lo:lo + HEAD_DIM], HEAD_DIM ** -0.5))
        kn.append(_l2norm(qkv[:, WIDTH + lo:WIDTH + lo + HEAD_DIM]))
        vs.append(qkv[:, 2 * WIDTH + lo:2 * WIDTH + lo + HEAD_DIM])
    terms = {}
    for j, hd in groups:
        rows = slice(j * n2, (j + 1) * n2)
        terms[j, hd] = _delta_chunk_terms(
            qn[hd][rows], kn[hd][rows], vs[hd][rows], beta_col[rows, hd:hd + 1],
            gc_cols[j][:, HEADS + hd:HEADS + hd + 1], gc_rows[j][HEADS + hd:HEADS + hd + 1, :])

    eye2 = (_iota2((ck, n2), 0) == _iota2((ck, n2), 1) % ck).astype(F32)
    powers = [terms[g]["neg_low"] for g in groups]
    invs = [eye2 + p for p in powers]
    powers = [_dot(p, _block_diag2(p)) for p in powers]

    def inverse_level(last=False):
        nonlocal powers, invs
        if last:
            invs = [inv + _dot(inv, _block_diag2(p)) for inv, p in zip(invs, powers)]
            return
        prods = [_dot(jnp.concatenate([inv, p], axis=0), _block_diag2(p)) for inv, p in zip(invs, powers)]
        invs = [inv + r[:ck] for inv, r in zip(invs, prods)]
        powers = [r[ck:] for r in prods]

    levels = 0
    while 2 ** (levels + 1) < ck:
        levels += 1
    assert levels == 5, "the side work below is placed for five squaring levels"

    n_slots = 2 * nck
    gate_cols = 2 * D_MODEL // nck
    n_gate = N_BRANCH * D_MODEL // gate_cols
    out_parts = 4
    out_cols = D_MODEL // out_parts
    n_early = n_gate + 2 * out_parts - n_slots
    assert 0 <= n_early <= levels and D_MODEL % gate_cols == 0
    gate_parts = []
    out_b, out_c = [], []

    def gate_piece(j):
        lo = j * gate_cols
        gate_parts.append(jax.nn.sigmoid(_dot(hb, _rest(w_ref, REST_MG + lo, REST_MG + lo + gate_cols))
                                         + bgate_ref[:, lo:lo + gate_cols]))

    side = [lambda j=j: gate_piece(j) for j in range(n_gate)]
    side += [lambda j=j: out_b.append(_dot(yb, wbrb_ref[:, j * out_cols:(j + 1) * out_cols]))
             for j in range(out_parts)]
    side += [lambda j=j: out_c.append(_dot(yc, wbrc_ref[:, j * out_cols:(j + 1) * out_cols]))
             for j in range(out_parts)]
    early, side = side[:n_early], side[n_early:]

    def inverse_level_and_filler(last=False):
        inverse_level(last)
        if early:
            early.pop(0)()

    scores = [_dot_nt(pc[:, hd * HEAD_DIM:(hd + 1) * HEAD_DIM], mk_ref[0, :, hd * HEAD_DIM:(hd + 1) * HEAD_DIM])
              for hd in range(HEADS)]
    inverse_level_and_filler()
    bu = pb[:, :WIDTH]
    vn = _layernorm(pb[:, WIDTH:2 * WIDTH], lng_ref[...], lnb_ref[...])
    ws = _spatial_weights(ws_ref)
    gw = WIDTH // MLP_GROUPS
    s_rows = []
    for n in range(tb // MLP_CHUNK):
        r0 = n * MLP_CHUNK
        s_rows.append(jnp.concatenate(
            [_dot(ws[g], vn[r0:r0 + MLP_CHUNK, g * gw:(g + 1) * gw]) + bs_ref[g]
             for g in range(MLP_GROUPS)], axis=1))
    yb = bu * jnp.concatenate(s_rows, axis=0) * jax.nn.silu(pb[:, 2 * WIDTH:])
    inverse_level_and_filler()
    inverse_level_and_filler()
    oc = []
    for hd in range(HEADS):
        p = jnp.exp2((scores[hd] - jnp.max(scores[hd], axis=-1, keepdims=True))
                     * (HEAD_DIM ** -0.5 * math.log2(math.e)))
        oc.append(_dot(p, mv_ref[0, :, hd * HEAD_DIM:(hd + 1) * HEAD_DIM])
                  / jnp.sum(p, axis=-1, keepdims=True))
    yc = jnp.concatenate(oc, axis=1) * jax.nn.silu(pc[:, WIDTH:])
    inverse_level_and_filler()
    inverse_level_and_filler(last=True)
    bds = [_block_diag2(inv, F32) for inv in invs]
    crosses = [_dot(bd, terms[g]["off"]) for g, bd in zip(groups, bds)]
    fulls = [bd - _dot(cr, bd) for bd, cr in zip(bds, crosses)]
    uws = {g: _dot(full, terms[g]["rhs"]) for g, full in zip(groups, fulls)}

    s_heads = [state[hd] for hd in range(HEADS)]
    o_chunks = [[] for _ in range(HEADS)]
    per_stage = len(side) // (2 * npair)
    assert per_stage * 2 * npair == len(side)
    for j in range(npair):
        ws_qs = []
        for hd in range(HEADS):
            wq = jnp.concatenate([uws[j, hd][:, HEAD_DIM:], terms[j, hd]["qd"]], axis=0)
            ws_qs.append(_dot(wq, s_heads[hd]))
        for _ in range(per_stage):
            side.pop(0)()
        for hd in range(HEADS):
            tm = terms[j, hd]
            v_new = uws[j, hd][:, :HEAD_DIM] - ws_qs[hd][:n2]
            o_chunks[hd].append(ws_qs[hd][n2:] + _dot(tm["attn"], v_new))
            s_heads[hd] = s_heads[hd] * tm["gl"] + _dot(tm["kdt"], v_new)
        for _ in range(per_stage):
            side.pop(0)()
    for hd in range(HEADS):
        state[hd] = s_heads[hd]
    sd_ref[0] = state[...]

    o_heads = []
    for hd in range(HEADS):
        lo = hd * HEAD_DIM
        o = jnp.concatenate(o_chunks[hd], axis=0)
        o_heads.append(_rms(o) * anormg_ref[...] * jax.nn.silu(agate[:, lo:lo + HEAD_DIM]))
    branch_out = [_dot(jnp.concatenate(o_heads, axis=1), wbra_ref[...]),
                  jnp.concatenate(out_b, axis=1), jnp.concatenate(out_c, axis=1)]
    gates = jnp.concatenate(gate_parts, axis=1)
    merged = None
    for i in range(N_BRANCH):
        term = gates[:, i * D_MODEL:(i + 1) * D_MODEL] * branch_out[i]
        merged = term if merged is None else merged + term
    out = x_ref[0] + _dot(merged, wout_ref[...])
    y_ref[0] = _rms(out) * fng_ref[...]


def _prompt_layer(x, mkb, mvb, weights, params):
    bsz, seq, _ = x.shape
    tb = PROMPT_BLOCK
    nt = seq // tb
    full = lambda a: pl.BlockSpec(a.shape, lambda b, t: (0,) * a.ndim, pipeline_mode=pl.Buffered(1))
    in_specs = [
        pl.BlockSpec((1, tb, D_MODEL), lambda b, t: (b, t, 0)),
        pl.BlockSpec((1, MEM_LEN, WIDTH), lambda b, t: (b, 0, 0)),
        pl.BlockSpec((1, MEM_LEN, WIDTH), lambda b, t: (b, 0, 0)),
    ] + [full(w) for w in weights] + [full(p) for p in params]
    out_specs = [
        pl.BlockSpec((1, tb, D_MODEL), lambda b, t: (b, t, 0)),
        pl.BlockSpec((1, HEADS, HEAD_DIM, HEAD_DIM), lambda b, t: (b, 0, 0, 0)),
        pl.BlockSpec((CONV_WIDTH - 1, bsz, CONV_CH), lambda b, t: (0, 0, 0)),
    ]
    out_shape = [
        jax.ShapeDtypeStruct((bsz, seq, D_MODEL), F32),
        jax.ShapeDtypeStruct((bsz, HEADS, HEAD_DIM, HEAD_DIM), F32),
        jax.ShapeDtypeStruct((CONV_WIDTH - 1, bsz, CONV_CH), F32),
    ]
    return pl.pallas_call(
        _prompt_kernel,
        grid=(bsz, nt),
        in_specs=in_specs,
        out_specs=out_specs,
        out_shape=out_shape,
        scratch_shapes=[pltpu.VMEM((CONV_CH // LANES, tb + SUBLANES, LANES), F32),
                        pltpu.VMEM((HEADS, HEAD_DIM, HEAD_DIM), F32),
                        pltpu.VMEM((DELTA_CHUNK + tb, LANES), F32)],
        compiler_params=pltpu.CompilerParams(dimension_semantics=("arbitrary", "arbitrary"),
                                             vmem_limit_bytes=VMEM_LIMIT),
        name="prompt_layer",
    )(x, mkb, mvb, *weights, *params)


def _sample_kernel(x_ref, ck_ref, cv_ref, s0_ref, cb_ref, w_ref, wbg_ref,
                   normg_ref, convw_ref, alog_ref, dtb_ref, anormg_ref, lng_ref, lnb_ref, ws_ref, bs_ref,
                   wbra_ref, wbrb_ref, wbrc_ref, bgate_ref, wout_ref, fng_ref,
                   y_ref, sd_ref, sc_ref, vn_ref,
                   hb_scr, pre_scr, rest_scr, bg_scr, ya_scr, yc_scr):
    sb = SAMPLE_BLOCK
    i = pl.program_id(0)
    nsteps = pl.num_programs(0)

    @pl.when(i == 0)
    def _():
        hb = (_rms(x_ref[:, 0, :]) * normg_ref[...]).astype(BF16)
        hb_scr[...] = hb
        pre_scr[...] = _dot(hb, w_ref[:, :CONV_CH])
        rest_scr[...] = _dot(hb, _rest(w_ref, 0, REST_MG))
        bg_scr[...] = _dot(hb, wbg_ref[...])
        bu = rest_scr[:, REST_B:REST_B + WIDTH]
        vn = _layernorm(rest_scr[:, REST_B + WIDTH:REST_B + 2 * WIDTH], lng_ref[...], lnb_ref[...])
        vn_ref[:, 0, :] = vn
        bgate = rest_scr[:, REST_B + 2 * WIDTH:REST_B + 3 * WIDTH]
        gw = WIDTH // MLP_GROUPS
        s = jnp.concatenate(
            [vn[:, g * gw:(g + 1) * gw] * ws_ref[g, 0:1, 0:1] + bs_ref[g, 0:1, :]
             for g in range(MLP_GROUPS)], axis=1)
        rest_scr[:, REST_B:REST_B + WIDTH] = bu * s * jax.nn.silu(bgate)

    r0 = pl.multiple_of(i * sb, sb)
    rows = pl.ds(r0, sb)
    pre = pre_scr[rows, :]
    beta, g = _beta_and_log_decay(bg_scr[rows, :], alog_ref[...], dtb_ref[...])
    decay = jnp.exp(g)
    agate = rest_scr[rows, REST_AGATE:REST_AGATE + WIDTH]
    cq = rest_scr[rows, REST_C:REST_C + WIDTH]
    cgate = rest_scr[rows, REST_C + WIDTH:REST_C + 2 * WIDTH]

    own_head = ((_iota2((SUBLANES, MEM_LEN * HEADS), 1) % HEADS)
                == (_iota2((SUBLANES, MEM_LEN * HEADS), 0) % HEADS))

    scores = []
    for s in range(sb):
        qh = jnp.concatenate([cq[s:s + 1, hd * HEAD_DIM:(hd + 1) * HEAD_DIM] for hd in range(HEADS)]
                             + [jnp.zeros((SUBLANES - HEADS, HEAD_DIM), F32)], axis=0)
        scores.append(_dot_nt(qh, ck_ref[s]) * (HEAD_DIM ** -0.5))

    carried = [cb_ref[j] for j in range(CONV_WIDTH - 1)]
    conv = pre * convw_ref[CONV_WIDTH - 1:CONV_WIDTH, :]
    for j in range(CONV_WIDTH - 1):
        conv = conv + carried[j] * convw_ref[j:j + 1, :]
    for j in range(1, CONV_WIDTH - 1):
        sc_ref[j - 1] = carried[j]
    sc_ref[CONV_WIDTH - 2] = pre
    qkv = jax.nn.silu(conv)

    row8 = _iota2((sb, HEAD_DIM), 0)
    row16 = _iota2((2 * sb, HEAD_DIM), 0)
    qs, ks, vs, kq_s = [], [], [], []
    for hd in range(HEADS):
        lo = hd * HEAD_DIM
        qs.append(_l2norm(qkv[:, lo:lo + HEAD_DIM], HEAD_DIM ** -0.5))
        ks.append(_l2norm(qkv[:, WIDTH + lo:WIDTH + lo + HEAD_DIM]))
        vs.append(qkv[:, 2 * WIDTH + lo:2 * WIDTH + lo + HEAD_DIM])
        kq = jnp.concatenate([ks[hd], qs[hd]], axis=0).astype(BF16)
        kq_s.append([_dot(kq, s0_ref[s, hd]) for s in range(sb)])
    o_heads = []
    for hd in range(HEADS):
        lo = hd * HEAD_DIM
        sk = jnp.zeros((sb, HEAD_DIM), F32)
        sq = jnp.zeros((sb, HEAD_DIM), F32)
        for s in range(sb):
            sk = jnp.where(row8 == s, kq_s[hd][s][:sb], sk)
            sq = jnp.where(row8 == s, kq_s[hd][s][sb:], sq)
        a = decay[:, HEADS + hd:HEADS + hd + 1]
        v_new = beta[:, hd:hd + 1] * (vs[hd] - a * sk)
        qk = jnp.sum(qs[hd] * ks[hd], axis=-1, keepdims=True)
        o = a * sq + qk * v_new
        o_heads.append(_rms(o) * anormg_ref[...] * jax.nn.silu(agate[:, lo:lo + HEAD_DIM]))
        kt = jnp.concatenate([ks[hd], jnp.zeros((HEAD_DIM - sb, HEAD_DIM), F32)], axis=0).T.astype(BF16)
        v_pad = jnp.concatenate([v_new, jnp.zeros((sb, HEAD_DIM), F32)], axis=0)
        zeros = jnp.zeros((HEAD_DIM - 2 * sb, HEAD_DIM), BF16)
        for s in range(sb):
            only_s = jnp.concatenate([jnp.where(row16 == s, v_pad, 0.0).astype(BF16), zeros], axis=0)
            sd_ref[s, hd] = a[s:s + 1, :] * s0_ref[s, hd] + _dot(kt, only_s)
    ya_scr[rows, :] = jnp.concatenate(o_heads, axis=1)

    probs, sums = [], []
    for s in range(sb):
        sc = jnp.where(own_head, scores[s], MASKED)
        sc = sc - jnp.max(sc, axis=-1, keepdims=True)
        probs.append(jnp.where(own_head, jnp.exp(sc), 0.0))
        sums.append(jnp.sum(probs[s], axis=-1, keepdims=True))
    yc_rows = []
    for s in range(sb):
        oc = _dot(probs[s], cv_ref[s]) / sums[s]
        oc = jnp.concatenate([oc[hd:hd + 1, :] for hd in range(HEADS)], axis=1)
        yc_rows.append(oc * jax.nn.silu(cgate[s:s + 1, :]))
    yc_scr[rows, :] = jnp.concatenate(yc_rows, axis=0)

    @pl.when(i == nsteps - 1)
    def _():
        y_ref[:, 0, :] = _merge_and_project(x_ref[:, 0, :], hb_scr[...], ya_scr[...],
                                        rest_scr[:, REST_B:REST_B + WIDTH], yc_scr[...], w_ref,
                                        bgate_ref, wbra_ref, wbrb_ref, wbrc_ref, wout_ref, fng_ref)


def _sample_layer(x, cache_k, cache_v, s0, cb, weights, params):
    n = x.shape[0]
    sb = SAMPLE_BLOCK
    full = lambda a: pl.BlockSpec(a.shape, lambda i: (0,) * a.ndim, pipeline_mode=pl.Buffered(1))
    in_specs = [
        full(x),
        pl.BlockSpec((sb, MEM_LEN * HEADS, HEAD_DIM), lambda i: (i, 0, 0)),
        pl.BlockSpec((sb, MEM_LEN * HEADS, HEAD_DIM), lambda i: (i, 0, 0)),
        pl.BlockSpec((sb, HEADS, HEAD_DIM, HEAD_DIM), lambda i: (i, 0, 0, 0)),
        pl.BlockSpec((CONV_WIDTH - 1, sb, CONV_CH), lambda i: (0, i, 0)),
    ] + [full(w) for w in weights] + [full(p) for p in params]
    out_specs = [
        pl.BlockSpec((n, 1, D_MODEL), lambda i: (0, 0, 0)),
        pl.BlockSpec((sb, HEADS, HEAD_DIM, HEAD_DIM), lambda i: (i, 0, 0, 0)),
        pl.BlockSpec((CONV_WIDTH - 1, sb, CONV_CH), lambda i: (0, i, 0)),
        pl.BlockSpec((n, 1, WIDTH), lambda i: (0, 0, 0)),
    ]
    out_shape = [
        jax.ShapeDtypeStruct((n, 1, D_MODEL), F32),
        jax.ShapeDtypeStruct((n, HEADS, HEAD_DIM, HEAD_DIM), F32),
        jax.ShapeDtypeStruct((CONV_WIDTH - 1, n, CONV_CH), F32),
        jax.ShapeDtypeStruct((n, 1, WIDTH), F32),
    ]
    return pl.pallas_call(
        _sample_kernel,
        grid=(n // sb,),
        in_specs=in_specs,
        out_specs=out_specs,
        out_shape=out_shape,
        scratch_shapes=[pltpu.VMEM((n, D_MODEL), BF16),
                        pltpu.VMEM((n, CONV_CH), F32),
                        pltpu.VMEM((n, REST_MG), F32),
                        pltpu.VMEM((n, LANES), F32),
                        pltpu.VMEM((n, WIDTH), F32),
                        pltpu.VMEM((n, WIDTH), F32)],
        compiler_params=pltpu.CompilerParams(dimension_semantics=("arbitrary",),
                                             vmem_limit_bytes=VMEM_LIMIT),
        name="sample_layer",
    )(x, cache_k, cache_v, s0, cb, *weights, *params)


PACK_BLOCK = 1536


def _pack_kernel(a_ref, b_ref, o_ref, bg_ref):
    j = pl.program_id(0)
    first_rest = CONV_CH // PACK_BLOCK
    a = a_ref[...]
    skipped = jnp.concatenate([a[2 * HEADS:], b_ref[...]], axis=0)
    rows = jnp.where(j < first_rest, a, skipped)
    o_ref[...] = rows.T.astype(BF16)

    @pl.when(j == first_rest)
    def _():
        logits = jnp.concatenate([a[:2 * HEADS], jnp.zeros((LANES - 2 * HEADS, D_MODEL), F32)], axis=0)
        bg_ref[...] = logits.T.astype(BF16)


def _pack_input_projection(w_t):
    assert 2 * HEADS == SUBLANES and CONV_CH % PACK_BLOCK == 0 and PACKED_COLS % PACK_BLOCK == 0
    return pl.pallas_call(
        _pack_kernel,
        grid=(PACKED_COLS // PACK_BLOCK,),
        in_specs=[pl.BlockSpec((PACK_BLOCK, D_MODEL), lambda j: (j, 0)),
                  pl.BlockSpec((SUBLANES, D_MODEL), lambda j: ((j + 1) * (PACK_BLOCK // SUBLANES), 0))],
        out_specs=[pl.BlockSpec((D_MODEL, PACK_BLOCK), lambda j: (0, j)),
                   pl.BlockSpec((D_MODEL, LANES), lambda j: (0, 0))],
        out_shape=[jax.ShapeDtypeStruct((D_MODEL, PACKED_COLS), BF16),
                   jax.ShapeDtypeStruct((D_MODEL, LANES), BF16)],
        compiler_params=pltpu.CompilerParams(dimension_semantics=("arbitrary",),
                                             vmem_limit_bytes=VMEM_LIMIT),
        name="pack_input_projection",
    )(w_t, w_t)


def _lanes_4_to_7(vec):
    return jnp.concatenate([jnp.zeros((1, HEADS), F32), vec[None, :].astype(F32),
                            jnp.zeros((1, LANES - 2 * HEADS), F32)], axis=1)


def kernel(x_prompt, x_sample, cache_mem_k, cache_mem_v, state_delta, state_conv, mem_prompt, norm_g, w_in, conv_w, a_log, dt_bias, a_norm_g, ln_v_g, ln_v_b, w_spatial, b_spatial, mem_norm_g, w_mem_kv, w_br_a, w_br_b, w_br_c, b_gate, w_out, final_norm_g):
    depth = norm_g.shape[0]
    assert depth == 1, "single-layer step"
    bsz, seq, _ = x_prompt.shape
    nsmp = x_sample.shape[0]
    assert x_sample.shape[1] == 1 and seq % PROMPT_BLOCK == 0 and nsmp % SAMPLE_BLOCK == 0
    assert w_in.shape[2] == CONV_CH + 2 * HEADS + REST_COLS

    weights = _pack_input_projection(w_in[0].T)
    params = (
        norm_g[0][None, :], conv_w[0], _lanes_4_to_7(a_log[0]), _lanes_4_to_7(dt_bias[0]),
        a_norm_g[0][None, :], ln_v_g[0][None, :], ln_v_b[0][None, :], w_spatial[0],
        jnp.broadcast_to(b_spatial[0][:, :, None], (MLP_GROUPS, MLP_CHUNK, WIDTH // MLP_GROUPS)),
        w_br_a[0].astype(BF16), w_br_b[0].astype(BF16), w_br_c[0].astype(BF16),
        b_gate[0].reshape(1, N_BRANCH * D_MODEL), w_out[0].astype(BF16), final_norm_g[None, :],
    )

    mk, mv, mkb, mvb = _memory_kv(mem_prompt.reshape(bsz * MEM_LEN, D_MODEL), mem_norm_g[0][None, :],
                                  w_mem_kv[0])
    y_p, sd_p, sc_p = _prompt_layer(x_prompt, mkb.reshape(bsz, MEM_LEN, WIDTH),
                                    mvb.reshape(bsz, MEM_LEN, WIDTH), weights, params)
    y_s, sd_s, sc_s, vn_s = _sample_layer(
        x_sample, cache_mem_k.reshape(nsmp, MEM_LEN * HEADS, HEAD_DIM),
        cache_mem_v.reshape(nsmp, MEM_LEN * HEADS, HEAD_DIM), state_delta.reshape(state_delta.shape[1:]),
        jnp.transpose(state_conv[0], (1, 0, 2)), weights, params)

    kv_shape = (1, bsz, MEM_LEN, HEADS, HEAD_DIM)
    return (y_p, y_s, sd_p[None], jnp.transpose(sc_p, (1, 0, 2))[None], mk.reshape(kv_shape),
            mv.reshape(kv_shape), sd_s[None], jnp.transpose(sc_s, (1, 0, 2))[None], vn_s[None])
```

```python
import math

import jax
import jax.numpy as jnp
from jax import lax
from jax.experimental import pallas as pl
from jax.experimental.pallas import tpu as pltpu

F32 = jnp.float32
BF16 = jnp.bfloat16

D_MODEL = 1024
HEADS = 4
HEAD_DIM = 128
WIDTH = HEADS * HEAD_DIM
CONV_WIDTH = 4
CONV_CH = 3 * WIDTH
MLP_GROUPS = 4
MLP_CHUNK = 128
MEM_LEN = 256
N_BRANCH = 3
EPS = 1e-6

LANES = 128
SUBLANES = 8

REST_AGATE = 0
REST_B = REST_AGATE + WIDTH
REST_C = REST_B + 3 * WIDTH
REST_MG = REST_C + 2 * WIDTH
REST_COLS = REST_MG + N_BRANCH * D_MODEL
PACKED_COLS = CONV_CH + REST_COLS

PROMPT_BLOCK = 512
DELTA_CHUNK = 64
SAMPLE_BLOCK = 8
MEMKV_BLOCK = 512
MASKED = -1e30
VMEM_LIMIT = 56 * 1024 * 1024


def _dot(a, b):
    return jnp.dot(a.astype(BF16), b.astype(BF16), preferred_element_type=F32)


def _dot_nt(a, b):
    return lax.dot_general(a.astype(BF16), b.astype(BF16), (((1,), (1,)), ((), ())),
                           preferred_element_type=F32)


def _rms(x):
    return x * lax.rsqrt(jnp.mean(x * x, axis=-1, keepdims=True) + EPS)


def _l2norm(x, scale=1.0):
    return x * (lax.rsqrt(jnp.sum(x * x, axis=-1, keepdims=True) + EPS) * scale)


def _softplus(x):
    return jnp.maximum(x, 0.0) + jnp.log1p(jnp.exp(-jnp.abs(x)))


def _iota2(shape, dim):
    return lax.broadcasted_iota(jnp.int32, shape, dim)


def _memkv_kernel(mem_ref, g_ref, w_ref, k_ref, v_ref, kb_ref, vb_ref):
    xn = _rms(mem_ref[...]) * g_ref[...]
    kv = _dot(xn, w_ref[...])
    k = kv[:, :WIDTH]
    v = kv[:, WIDTH:]
    npos = k.shape[0]
    for hd in range(HEADS):
        k_ref[pl.ds(hd, npos, stride=HEADS), :] = k[:, hd * HEAD_DIM:(hd + 1) * HEAD_DIM]
        v_ref[pl.ds(hd, npos, stride=HEADS), :] = v[:, hd * HEAD_DIM:(hd + 1) * HEAD_DIM]
    kb_ref[...] = k.astype(BF16)
    vb_ref[...] = v.astype(BF16)


def _memory_kv(mem2d, mem_norm_g, w_mem_kv):
    rows = mem2d.shape[0]
    blk = MEMKV_BLOCK
    full = lambda shape: pl.BlockSpec(shape, lambda i: (0,) * len(shape))
    row_spec = lambda width: pl.BlockSpec((blk, width), lambda i: (i, 0))
    by_head = pl.BlockSpec((blk * HEADS, HEAD_DIM), lambda i: (i, 0))
    return pl.pallas_call(
        _memkv_kernel,
        grid=(rows // blk,),
        in_specs=[row_spec(D_MODEL), full((1, D_MODEL)), full((D_MODEL, 2 * WIDTH))],
        out_specs=[by_head] * 2 + [row_spec(WIDTH)] * 2,
        out_shape=[jax.ShapeDtypeStruct((rows * HEADS, HEAD_DIM), F32)] * 2
        + [jax.ShapeDtypeStruct((rows, WIDTH), BF16)] * 2,
        compiler_params=pltpu.CompilerParams(dimension_semantics=("arbitrary",),
                                             vmem_limit_bytes=VMEM_LIMIT),
        name="memory_kv",
    )(mem2d, mem_norm_g, w_mem_kv)


def _beta_and_log_decay(bg, alog, dtb):
    beta = jax.nn.sigmoid(bg)
    g = -jnp.exp(alog) * _softplus(bg + dtb)
    return beta, g


def _spatial_weights(ws_ref):
    tril = _iota2((MLP_CHUNK, MLP_CHUNK), 0) >= _iota2((MLP_CHUNK, MLP_CHUNK), 1)
    return [jnp.where(tril, ws_ref[g], 0.0).astype(BF16) for g in range(MLP_GROUPS)]


def _layernorm(v, g, b):
    mu = jnp.mean(v, axis=-1, keepdims=True)
    vc = v - mu
    return vc * lax.rsqrt(jnp.mean(vc * vc, axis=-1, keepdims=True) + EPS) * g + b


def _rest(w_ref, lo, hi):
    return w_ref[:, CONV_CH + lo:CONV_CH + hi]


def _merge_and_project(x, hb, ya, yb, yc, w_ref, bgate_ref, wbra_ref, wbrb_ref, wbrc_ref, wout_ref,
                       fng_ref):
    merged = None
    for i, (yi, wbr) in enumerate(((ya, wbra_ref), (yb, wbrb_ref), (yc, wbrc_ref))):
        lo = i * D_MODEL
        gate = jax.nn.sigmoid(_dot(hb, _rest(w_ref, REST_MG + lo, REST_MG + lo + D_MODEL))
                              + bgate_ref[:, lo:lo + D_MODEL])
        term = gate * _dot(yi, wbr[...])
        merged = term if merged is None else merged + term
    out = x + _dot(merged, wout_ref[...])
    return _rms(out) * fng_ref[...]


def _chunk_cumsum(g, ck, scr):
    n = g.shape[0]
    pad = ck // 2
    pos = _iota2(g.shape, 0) % ck
    scr[0:pad, :] = jnp.zeros((pad, LANES), F32)
    d = 1
    while d < ck:
        scr[pad:pad + n, :] = g
        g = g + jnp.where(pos >= d, scr[pad - d:pad - d + n, :], 0.0)
        d *= 2
    return g


def _delta_chunk_terms(q, k, v, beta_col, gc_col, gc_row):
    n = q.shape[0]
    half = n // 2
    ri = _iota2((n, n), 0)
    ci = _iota2((n, n), 1)
    same = (ri < half) == (ci < half)
    kt = k.T
    decay = jnp.exp(jnp.where(ri >= ci, gc_col - gc_row, MASKED))
    kb = k * beta_col
    egc = jnp.exp(gc_col)
    low = jnp.where(ri > ci, _dot(kb, kt) * decay, 0.0)
    diag = jnp.where(same, low, 0.0)
    last = gc_row[:, n - 1:n]
    return dict(
        neg_low=-(diag[:half] + diag[half:]),
        off=jnp.where(same, 0.0, low),
        attn=jnp.where(ri >= ci, _dot(q, kt) * decay, 0.0),
        rhs=jnp.concatenate([v * beta_col, kb * egc], axis=1),
        qd=q * egc,
        kdt=kt * jnp.exp(last - gc_row),
        gl=jnp.exp(last),
    )


def _block_diag2(packed, dtype=BF16):
    n = packed.shape[0]
    left = _iota2(packed.shape, 1) < n
    return jnp.concatenate([jnp.where(left, packed, 0.0), jnp.where(left, 0.0, packed)],
                           axis=0).astype(dtype)


def _prompt_kernel(x_ref, mk_ref, mv_ref, w_ref, wbg_ref, normg_ref, convw_ref,
                   alog_ref, dtb_ref, anormg_ref, lng_ref, lnb_ref, ws_ref, bs_ref, wbra_ref, wbrb_ref,
                   wbrc_ref, bgate_ref, wout_ref, fng_ref,
                   y_ref, sd_ref, sc_ref,
                   convbuf, state, cum_scr):
    tb = PROMPT_BLOCK
    ck = DELTA_CHUNK
    nck = tb // ck
    b = pl.program_id(0)
    t = pl.program_id(1)

    @pl.when((b == 0) & (t == 0))
    def _():
        sc_ref[...] = jnp.zeros_like(sc_ref)

    @pl.when(t == 0)
    def _():
        convbuf[:, 0:SUBLANES, :] = jnp.zeros((CONV_CH // LANES, SUBLANES, LANES), F32)
        state[...] = jnp.zeros_like(state)

    hb = (_rms(x_ref[0]) * normg_ref[...]).astype(BF16)

    pre = _dot(hb, w_ref[:, :CONV_CH])
    bg_col = _dot(hb, wbg_ref[...])
    pb = _dot(hb, _rest(w_ref, REST_B, REST_B + 3 * WIDTH))
    pc = _dot(hb, _rest(w_ref, REST_C, REST_C + 2 * WIDTH))
    agate = _dot(hb, _rest(w_ref, REST_AGATE, REST_AGATE + WIDTH))

    conv_tiles, tails = [], []
    for ct in range(CONV_CH // LANES):
        cols = slice(ct * LANES, (ct + 1) * LANES)
        convbuf[ct, SUBLANES:SUBLANES + tb, :] = pre[:, cols]
        acc = pre[:, cols] * convw_ref[CONV_WIDTH - 1:CONV_WIDTH, cols]
        for j in range(CONV_WIDTH - 1):
            shift = CONV_WIDTH - 1 - j
            acc = acc + convbuf[ct, SUBLANES - shift:SUBLANES - shift + tb, :] * convw_ref[j:j + 1, cols]
        conv_tiles.append(acc)
        tails.append(convbuf[ct, tb:tb + SUBLANES, :])
        convbuf[ct, 0:SUBLANES, :] = tails[ct]
    last_rows = jnp.concatenate(tails, axis=1)[SUBLANES - (CONV_WIDTH - 1):, :]
    mine = _iota2(sc_ref.shape[1:], 0) == b
    for j in range(CONV_WIDTH - 1):
        sc_ref[j] = jnp.where(mine, last_rows[j:j + 1, :], sc_ref[j])
    qkv = jax.nn.silu(jnp.concatenate(conv_tiles, axis=1))

    beta_col, g_col = _beta_and_log_decay(bg_col, alog_ref[...], dtb_ref[...])
    n2 = 2 * ck
    assert nck % 2 == 0 and n2 == LANES
    npair = nck // 2
    gc_col = _chunk_cumsum(g_col, n2, cum_scr)
    gc_cols = [gc_col[j * n2:(j + 1) * n2, :] for j in range(npair)]
    gc_rows = [gc.T[:SUBLANES, :] for gc in gc_cols]

    groups = [(j, hd) for j in range(npair) for hd in range(HEADS)]
    qn, kn, vs = [], [], []
    for hd in range(HEADS):
        lo = hd * HEAD_DIM
        qn.append(_l2norm(qkv[:, lo:lo + HEAD_DIM], HEAD_DIM ** -0.5))
        kn.append(_l2norm(qkv[:, WIDTH + lo:WIDTH + lo + HEAD_DIM]))
        vs.append(qkv[:, 2 * WIDTH + lo:2 * WIDTH + lo + HEAD_DIM])
    terms = {}
    for j, hd in groups:
        rows = slice(j * n2, (j + 1) * n2)
        terms[j, hd] = _delta_chunk_terms(
            qn[hd][rows], kn[hd][rows], vs[hd][rows], beta_col[rows, hd:hd + 1],
            gc_cols[j][:, HEADS + hd:HEADS + hd + 1], gc_rows[j][HEADS + hd:HEADS + hd + 1, :])

    eye2 = (_iota2((ck, n2), 0) == _iota2((ck, n2), 1) % ck).astype(F32)
    powers = [terms[g]["neg_low"] for g in groups]
    invs = [eye2 + p for p in powers]
    powers = [_dot(p, _block_diag2(p)) for p in powers]

    def inverse_level(last=False):
        nonlocal powers, invs
        if last:
            invs = [inv + _dot(inv, _block_diag2(p)) for inv, p in zip(invs, powers)]
            return
        prods = [_dot(jnp.concatenate([inv, p], axis=0), _block_diag2(p)) for inv, p in zip(invs, powers)]
        invs = [inv + r[:ck] for inv, r in zip(invs, prods)]
        powers = [r[ck:] for r in prods]

    levels = 0
    while 2 ** (levels + 1) < ck:
        levels += 1
    assert levels == 5, "the side work below is placed for five squaring levels"

    n_slots = 2 * nck
    gate_cols = 2 * D_MODEL // nck
    n_gate = N_BRANCH * D_MODEL // gate_cols
    out_parts = 4
    out_cols = D_MODEL // out_parts
    n_early = levels + 1
    assert D_MODEL % gate_cols == 0
    gate_parts = []
    out_b, out_c = [], []

    def gate_piece(j):
        lo = j * gate_cols
        gate_parts.append(jax.nn.sigmoid(_dot(hb, _rest(w_ref, REST_MG + lo, REST_MG + lo + gate_cols))
                                         + bgate_ref[:, lo:lo + gate_cols]))

    side = [lambda j=j: gate_piece(j) for j in range(n_gate)]
    side += [lambda j=j: out_b.append(_dot(yb, wbrb_ref[:, j * out_cols:(j + 1) * out_cols]))
             for j in range(out_parts)]
    side += [lambda j=j: out_c.append(_dot(yc, wbrc_ref[:, j * out_cols:(j + 1) * out_cols]))
             for j in range(out_parts)]
    early, side = side[:n_early], side[n_early:]

    def inverse_level_and_filler(last=False):
        inverse_level(last)
        if early:
            early.pop(0)()

    early.pop(0)()
    scores = [_dot_nt(pc[:, hd * HEAD_DIM:(hd + 1) * HEAD_DIM], mk_ref[0, :, hd * HEAD_DIM:(hd + 1) * HEAD_DIM])
              for hd in range(HEADS)]
    inverse_level_and_filler()
    bu = pb[:, :WIDTH]
    vn = _layernorm(pb[:, WIDTH:2 * WIDTH], lng_ref[...], lnb_ref[...])
    ws = _spatial_weights(ws_ref)
    gw = WIDTH // MLP_GROUPS
    s_rows = []
    for n in range(tb // MLP_CHUNK):
        r0 = n * MLP_CHUNK
        s_rows.append(jnp.concatenate(
            [_dot(ws[g], vn[r0:r0 + MLP_CHUNK, g * gw:(g + 1) * gw]) + bs_ref[g]
             for g in range(MLP_GROUPS)], axis=1))
    yb = bu * jnp.concatenate(s_rows, axis=0) * jax.nn.silu(pb[:, 2 * WIDTH:])
    inverse_level_and_filler()
    inverse_level_and_filler()
    oc = []
    for hd in range(HEADS):
        p = jnp.exp2((scores[hd] - jnp.max(scores[hd], axis=-1, keepdims=True))
                     * (HEAD_DIM ** -0.5 * math.log2(math.e)))
        oc.append(_dot(p, mv_ref[0, :, hd * HEAD_DIM:(hd + 1) * HEAD_DIM])
                  / jnp.sum(p, axis=-1, keepdims=True))
    yc = jnp.concatenate(oc, axis=1) * jax.nn.silu(pc[:, WIDTH:])
    inverse_level_and_filler()
    inverse_level_and_filler(last=True)
    bds = [_block_diag2(inv, F32) for inv in invs]
    crosses = [_dot(bd, terms[g]["off"]) for g, bd in zip(groups, bds)]
    fulls = [bd - _dot(cr, bd) for bd, cr in zip(bds, crosses)]
    uws = {g: _dot(full, terms[g]["rhs"]) for g, full in zip(groups, fulls)}

    s_heads = [state[hd] for hd in range(HEADS)]
    o_chunks = [[] for _ in range(HEADS)]
    n_stages = 2 * npair
    shares = [len(side) // n_stages + (1 if i < len(side) % n_stages else 0) for i in range(n_stages)]
    for j in range(npair):
        ws_qs = []
        for hd in range(HEADS):
            wq = jnp.concatenate([uws[j, hd][:, HEAD_DIM:], terms[j, hd]["qd"]], axis=0)
            ws_qs.append(_dot(wq, s_heads[hd]))
        for _ in range(shares[2 * j]):
            side.pop(0)()
        for hd in range(HEADS):
            tm = terms[j, hd]
            v_new = uws[j, hd][:, :HEAD_DIM] - ws_qs[hd][:n2]
            o_chunks[hd].append(ws_qs[hd][n2:] + _dot(tm["attn"], v_new))
            s_heads[hd] = s_heads[hd] * tm["gl"] + _dot(tm["kdt"], v_new)
        for _ in range(shares[2 * j + 1]):
            side.pop(0)()
    for hd in range(HEADS):
        state[hd] = s_heads[hd]
    sd_ref[0] = state[...]

    o_heads = []
    for hd in range(HEADS):
        lo = hd * HEAD_DIM
        o = jnp.concatenate(o_chunks[hd], axis=0)
        o_heads.append(_rms(o) * anormg_ref[...] * jax.nn.silu(agate[:, lo:lo + HEAD_DIM]))
    branch_out = [_dot(jnp.concatenate(o_heads, axis=1), wbra_ref[...]),
                  jnp.concatenate(out_b, axis=1), jnp.concatenate(out_c, axis=1)]
    gates = jnp.concatenate(gate_parts, axis=1)
    merged = None
    for i in range(N_BRANCH):
        term = gates[:, i * D_MODEL:(i + 1) * D_MODEL] * branch_out[i]
        merged = term if merged is None else merged + term
    out = x_ref[0] + _dot(merged, wout_ref[...])
    y_ref[0] = _rms(out) * fng_ref[...]


def _prompt_layer(x, mkb, mvb, weights, params):
    bsz, seq, _ = x.shape
    tb = PROMPT_BLOCK
    nt = seq // tb
    full = lambda a: pl.BlockSpec(a.shape, lambda b, t: (0,) * a.ndim, pipeline_mode=pl.Buffered(1))
    in_specs = [
        pl.BlockSpec((1, tb, D_MODEL), lambda b, t: (b, t, 0)),
        pl.BlockSpec((1, MEM_LEN, WIDTH), lambda b, t: (b, 0, 0)),
        pl.BlockSpec((1, MEM_LEN, WIDTH), lambda b, t: (b, 0, 0)),
    ] + [full(w) for w in weights] + [full(p) for p in params]
    out_specs = [
        pl.BlockSpec((1, tb, D_MODEL), lambda b, t: (b, t, 0)),
        pl.BlockSpec((1, HEADS, HEAD_DIM, HEAD_DIM), lambda b, t: (b, 0, 0, 0)),
        pl.BlockSpec((CONV_WIDTH - 1, bsz, CONV_CH), lambda b, t: (0, 0, 0)),
    ]
    out_shape = [
        jax.ShapeDtypeStruct((bsz, seq, D_MODEL), F32),
        jax.ShapeDtypeStruct((bsz, HEADS, HEAD_DIM, HEAD_DIM), F32),
        jax.ShapeDtypeStruct((CONV_WIDTH - 1, bsz, CONV_CH), F32),
    ]
    return pl.pallas_call(
        _prompt_kernel,
        grid=(bsz, nt),
        in_specs=in_specs,
        out_specs=out_specs,
        out_shape=out_shape,
        scratch_shapes=[pltpu.VMEM((CONV_CH // LANES, tb + SUBLANES, LANES), F32),
                        pltpu.VMEM((HEADS, HEAD_DIM, HEAD_DIM), F32),
                        pltpu.VMEM((DELTA_CHUNK + tb, LANES), F32)],
        compiler_params=pltpu.CompilerParams(dimension_semantics=("arbitrary", "arbitrary"),
                                             vmem_limit_bytes=VMEM_LIMIT),
        name="prompt_layer",
    )(x, mkb, mvb, *weights, *params)


def _sample_kernel(x_ref, ck_ref, cv_ref, s0_ref, cb_ref, w_ref, wbg_ref,
                   normg_ref, convw_ref, alog_ref, dtb_ref, anormg_ref, lng_ref, lnb_ref, ws_ref, bs_ref,
                   wbra_ref, wbrb_ref, wbrc_ref, bgate_ref, wout_ref, fng_ref,
                   y_ref, sd_ref, sc_ref, vn_ref,
                   hb_scr, pre_scr, rest_scr, bg_scr, ya_scr, yc_scr):
    sb = SAMPLE_BLOCK
    i = pl.program_id(0)
    nsteps = pl.num_programs(0)

    @pl.when(i == 0)
    def _():
        hb = (_rms(x_ref[:, 0, :]) * normg_ref[...]).astype(BF16)
        hb_scr[...] = hb
        pre_scr[...] = _dot(hb, w_ref[:, :CONV_CH])
        rest_scr[...] = _dot(hb, _rest(w_ref, 0, REST_MG))
        bg_scr[...] = _dot(hb, wbg_ref[...])
        bu = rest_scr[:, REST_B:REST_B + WIDTH]
        vn = _layernorm(rest_scr[:, REST_B + WIDTH:REST_B + 2 * WIDTH], lng_ref[...], lnb_ref[...])
        vn_ref[:, 0, :] = vn
        bgate = rest_scr[:, REST_B + 2 * WIDTH:REST_B + 3 * WIDTH]
        gw = WIDTH // MLP_GROUPS
        s = jnp.concatenate(
            [vn[:, g * gw:(g + 1) * gw] * ws_ref[g, 0:1, 0:1] + bs_ref[g, 0:1, :]
             for g in range(MLP_GROUPS)], axis=1)
        rest_scr[:, REST_B:REST_B + WIDTH] = bu * s * jax.nn.silu(bgate)

    r0 = pl.multiple_of(i * sb, sb)
    rows = pl.ds(r0, sb)
    pre = pre_scr[rows, :]
    beta, g = _beta_and_log_decay(bg_scr[rows, :], alog_ref[...], dtb_ref[...])
    decay = jnp.exp(g)
    agate = rest_scr[rows, REST_AGATE:REST_AGATE + WIDTH]
    cq = rest_scr[rows, REST_C:REST_C + WIDTH]
    cgate = rest_scr[rows, REST_C + WIDTH:REST_C + 2 * WIDTH]

    own_head = ((_iota2((SUBLANES, MEM_LEN * HEADS), 1) % HEADS)
                == (_iota2((SUBLANES, MEM_LEN * HEADS), 0) % HEADS))

    scores = []
    for s in range(sb):
        qh = jnp.concatenate([cq[s:s + 1, hd * HEAD_DIM:(hd + 1) * HEAD_DIM] for hd in range(HEADS)]
                             + [jnp.zeros((SUBLANES - HEADS, HEAD_DIM), F32)], axis=0)
        scores.append(_dot_nt(qh, ck_ref[s]) * (HEAD_DIM ** -0.5))

    carried = [cb_ref[j] for j in range(CONV_WIDTH - 1)]
    conv = pre * convw_ref[CONV_WIDTH - 1:CONV_WIDTH, :]
    for j in range(CONV_WIDTH - 1):
        conv = conv + carried[j] * convw_ref[j:j + 1, :]
    for j in range(1, CONV_WIDTH - 1):
        sc_ref[j - 1] = carried[j]
    sc_ref[CONV_WIDTH - 2] = pre
    qkv = jax.nn.silu(conv)

    row8 = _iota2((sb, HEAD_DIM), 0)
    row16 = _iota2((2 * sb, HEAD_DIM), 0)
    qs, ks, vs, kq_s = [], [], [], []
    for hd in range(HEADS):
        lo = hd * HEAD_DIM
        qs.append(_l2norm(qkv[:, lo:lo + HEAD_DIM], HEAD_DIM ** -0.5))
        ks.append(_l2norm(qkv[:, WIDTH + lo:WIDTH + lo + HEAD_DIM]))
        vs.append(qkv[:, 2 * WIDTH + lo:2 * WIDTH + lo + HEAD_DIM])
        kq = jnp.concatenate([ks[hd], qs[hd]], axis=0).astype(BF16)
        kq_s.append([_dot(kq, s0_ref[s, hd]) for s in range(sb)])
    o_heads = []
    for hd in range(HEADS):
        lo = hd * HEAD_DIM
        sk = jnp.zeros((sb, HEAD_DIM), F32)
        sq = jnp.zeros((sb, HEAD_DIM), F32)
        for s in range(sb):
            sk = jnp.where(row8 == s, kq_s[hd][s][:sb], sk)
            sq = jnp.where(row8 == s, kq_s[hd][s][sb:], sq)
        a = decay[:, HEADS + hd:HEADS + hd + 1]
        v_new = beta[:, hd:hd + 1] * (vs[hd] - a * sk)
        qk = jnp.sum(qs[hd] * ks[hd], axis=-1, keepdims=True)
        o = a * sq + qk * v_new
        o_heads.append(_rms(o) * anormg_ref[...] * jax.nn.silu(agate[:, lo:lo + HEAD_DIM]))
        kt = jnp.concatenate([ks[hd], jnp.zeros((HEAD_DIM - sb, HEAD_DIM), F32)], axis=0).T.astype(BF16)
        v_pad = jnp.concatenate([v_new, jnp.zeros((sb, HEAD_DIM), F32)], axis=0)
        zeros = jnp.zeros((HEAD_DIM - 2 * sb, HEAD_DIM), BF16)
        for s in range(sb):
            only_s = jnp.concatenate([jnp.where(row16 == s, v_pad, 0.0).astype(BF16), zeros], axis=0)
            sd_ref[s, hd] = a[s:s + 1, :] * s0_ref[s, hd] + _dot(kt, only_s)
    ya_scr[rows, :] = jnp.concatenate(o_heads, axis=1)

    probs, sums = [], []
    for s in range(sb):
        sc = jnp.where(own_head, scores[s], MASKED)
        sc = sc - jnp.max(sc, axis=-1, keepdims=True)
        probs.append(jnp.where(own_head, jnp.exp(sc), 0.0))
        sums.append(jnp.sum(probs[s], axis=-1, keepdims=True))
    yc_rows = []
    for s in range(sb):
        oc = _dot(probs[s], cv_ref[s]) / sums[s]
        oc = jnp.concatenate([oc[hd:hd + 1, :] for hd in range(HEADS)], axis=1)
        yc_rows.append(oc * jax.nn.silu(cgate[s:s + 1, :]))
    yc_scr[rows, :] = jnp.concatenate(yc_rows, axis=0)

    @pl.when(i == nsteps - 1)
    def _():
        y_ref[:, 0, :] = _merge_and_project(x_ref[:, 0, :], hb_scr[...], ya_scr[...],
                                        rest_scr[:, REST_B:REST_B + WIDTH], yc_scr[...], w_ref,
                                        bgate_ref, wbra_ref, wbrb_ref, wbrc_ref, wout_ref, fng_ref)


def _sample_layer(x, cache_k, cache_v, s0, cb, weights, params):
    n = x.shape[0]
    sb = SAMPLE_BLOCK
    full = lambda a: pl.BlockSpec(a.shape, lambda i: (0,) * a.ndim, pipeline_mode=pl.Buffered(1))
    in_specs = [
        full(x),
        pl.BlockSpec((sb, MEM_LEN * HEADS, HEAD_DIM), lambda i: (i, 0, 0)),
        pl.BlockSpec((sb, MEM_LEN * HEADS, HEAD_DIM), lambda i: (i, 0, 0)),
        pl.BlockSpec((sb, HEADS, HEAD_DIM, HEAD_DIM), lambda i: (i, 0, 0, 0)),
        pl.BlockSpec((CONV_WIDTH - 1, sb, CONV_CH), lambda i: (0, i, 0)),
    ] + [full(w) for w in weights] + [full(p) for p in params]
    out_specs = [
        pl.BlockSpec((n, 1, D_MODEL), lambda i: (0, 0, 0)),
        pl.BlockSpec((sb, HEADS, HEAD_DIM, HEAD_DIM), lambda i: (i, 0, 0, 0)),
        pl.BlockSpec((CONV_WIDTH - 1, sb, CONV_CH), lambda i: (0, i, 0)),
        pl.BlockSpec((n, 1, WIDTH), lambda i: (0, 0, 0)),
    ]
    out_shape = [
        jax.ShapeDtypeStruct((n, 1, D_MODEL), F32),
        jax.ShapeDtypeStruct((n, HEADS, HEAD_DIM, HEAD_DIM), F32),
        jax.ShapeDtypeStruct((CONV_WIDTH - 1, n, CONV_CH), F32),
        jax.ShapeDtypeStruct((n, 1, WIDTH), F32),
    ]
    return pl.pallas_call(
        _sample_kernel,
        grid=(n // sb,),
        in_specs=in_specs,
        out_specs=out_specs,
        out_shape=out_shape,
        scratch_shapes=[pltpu.VMEM((n, D_MODEL), BF16),
                        pltpu.VMEM((n, CONV_CH), F32),
                        pltpu.VMEM((n, REST_MG), F32),
                        pltpu.VMEM((n, LANES), F32),
                        pltpu.VMEM((n, WIDTH), F32),
                        pltpu.VMEM((n, WIDTH), F32)],
        compiler_params=pltpu.CompilerParams(dimension_semantics=("arbitrary",),
                                             vmem_limit_bytes=VMEM_LIMIT),
        name="sample_layer",
    )(x, cache_k, cache_v, s0, cb, *weights, *params)


PACK_BLOCK = 1536


def _pack_kernel(a_ref, b_ref, o_ref, bg_ref):
    j = pl.program_id(0)
    first_rest = CONV_CH // PACK_BLOCK
    a = a_ref[...]
    skipped = jnp.concatenate([a[2 * HEADS:], b_ref[...]], axis=0)
    rows = jnp.where(j < first_rest, a, skipped)
    o_ref[...] = rows.T.astype(BF16)

    @pl.when(j == first_rest)
    def _():
        logits = jnp.concatenate([a[:2 * HEADS], jnp.zeros((LANES - 2 * HEADS, D_MODEL), F32)], axis=0)
        bg_ref[...] = logits.T.astype(BF16)


def _pack_input_projection(w_t):
    assert 2 * HEADS == SUBLANES and CONV_CH % PACK_BLOCK == 0 and PACKED_COLS % PACK_BLOCK == 0
    return pl.pallas_call(
        _pack_kernel,
        grid=(PACKED_COLS // PACK_BLOCK,),
        in_specs=[pl.BlockSpec((PACK_BLOCK, D_MODEL), lambda j: (j, 0)),
                  pl.BlockSpec((SUBLANES, D_MODEL), lambda j: ((j + 1) * (PACK_BLOCK // SUBLANES), 0))],
        out_specs=[pl.BlockSpec((D_MODEL, PACK_BLOCK), lambda j: (0, j)),
                   pl.BlockSpec((D_MODEL, LANES), lambda j: (0, 0))],
        out_shape=[jax.ShapeDtypeStruct((D_MODEL, PACKED_COLS), BF16),
                   jax.ShapeDtypeStruct((D_MODEL, LANES), BF16)],
        compiler_params=pltpu.CompilerParams(dimension_semantics=("arbitrary",),
                                             vmem_limit_bytes=VMEM_LIMIT),
        name="pack_input_projection",
    )(w_t, w_t)


def _lanes_4_to_7(vec):
    return jnp.concatenate([jnp.zeros((1, HEADS), F32), vec[None, :].astype(F32),
                            jnp.zeros((1, LANES - 2 * HEADS), F32)], axis=1)


def kernel(x_prompt, x_sample, cache_mem_k, cache_mem_v, state_delta, state_conv, mem_prompt, norm_g, w_in, conv_w, a_log, dt_bias, a_norm_g, ln_v_g, ln_v_b, w_spatial, b_spatial, mem_norm_g, w_mem_kv, w_br_a, w_br_b, w_br_c, b_gate, w_out, final_norm_g):
    depth = norm_g.shape[0]
    assert depth == 1, "single-layer step"
    bsz, seq, _ = x_prompt.shape
    nsmp = x_sample.shape[0]
    assert x_sample.shape[1] == 1 and seq % PROMPT_BLOCK == 0 and nsmp % SAMPLE_BLOCK == 0
    assert w_in.shape[2] == CONV_CH + 2 * HEADS + REST_COLS

    weights = _pack_input_projection(w_in[0].T)
    params = (
        norm_g[0][None, :], conv_w[0], _lanes_4_to_7(a_log[0]), _lanes_4_to_7(dt_bias[0]),
        a_norm_g[0][None, :], ln_v_g[0][None, :], ln_v_b[0][None, :], w_spatial[0],
        jnp.broadcast_to(b_spatial[0][:, :, None], (MLP_GROUPS, MLP_CHUNK, WIDTH // MLP_GROUPS)),
        w_br_a[0].astype(BF16), w_br_b[0].astype(BF16), w_br_c[0].astype(BF16),
        b_gate[0].reshape(1, N_BRANCH * D_MODEL), w_out[0].astype(BF16), final_norm_g[None, :],
    )

    mk, mv, mkb, mvb = _memory_kv(mem_prompt.reshape(bsz * MEM_LEN, D_MODEL), mem_norm_g[0][None, :],
                                  w_mem_kv[0])
    y_p, sd_p, sc_p = _prompt_layer(x_prompt, mkb.reshape(bsz, MEM_LEN, WIDTH),
                                    mvb.reshape(bsz, MEM_LEN, WIDTH), weights, params)
    y_s, sd_s, sc_s, vn_s = _sample_layer(
        x_sample, cache_mem_k.reshape(nsmp, MEM_LEN * HEADS, HEAD_DIM),
        cache_mem_v.reshape(nsmp, MEM_LEN * HEADS, HEAD_DIM), state_delta.reshape(state_delta.shape[1:]),
        jnp.transpose(state_conv[0], (1, 0, 2)), weights, params)

    kv_shape = (1, bsz, MEM_LEN, HEADS, HEAD_DIM)
    return (y_p, y_s, sd_p[None], jnp.transpose(sc_p, (1, 0, 2))[None], mk.reshape(kv_shape),
            mv.reshape(kv_shape), sd_s[None], jnp.transpose(sc_s, (1, 0, 2))[None], vn_s[None])
```
